```python
import jax, jax.numpy as jnp
from jax import lax
import numpy as np

D_MODEL = 1024
BATCH = 16
SEQ = 2048
DEPTH = 4

HEAD_DIM = 128
A_HEADS = D_MODEL // HEAD_DIM
B_HEADS = D_MODEL // HEAD_DIM
HGRN_CHUNK = 64
MOBA_BLOCK = 256
MOBA_TOPK = 3
MOBA_QCHUNK = 16
D_FF = ((8 * D_MODEL // 3 + 255) // 256) * 256
N_EXPERTS = 8
MOE_TOPK = 2
D_FF_EXPERT = 7 * D_MODEL // 2
N_A = DEPTH // 2
N_B = DEPTH - N_A
N_DENSE = (DEPTH + 1) // 2
N_MOE = DEPTH // 2
EPS = 1e-6

kernel_name = "yoco_hgrn2_moba_moe_trunk"


def rmsnorm(x, g):
    xf = x.astype(jnp.float32)
    y = xf * lax.rsqrt(jnp.mean(xf * xf, axis=-1, keepdims=True) + EPS)
    return (y * g.astype(jnp.float32)).astype(x.dtype)


def alibi_slopes(n_heads):
    return jnp.asarray(2.0 ** (-8.0 * np.arange(1, n_heads + 1) / n_heads), dtype=jnp.float32)


def hgrn2_mixer(h, w_in, lb, out_gain, w_o):
    B, S, D = h.shape
    H, dk, C = A_HEADS, HEAD_DIM, HGRN_CHUNK
    NC = S // C
    q, fz, i_in, g = jnp.split(h @ w_in, 4, axis=-1)
    q = jax.nn.silu(q.astype(jnp.float32))
    fz = fz.astype(jnp.float32)
    v = i_in.astype(jnp.float32)
    lb = lb.astype(jnp.float32)
    log_f = jnp.logaddexp(jnp.log(lb), jnp.log1p(-lb) + jax.nn.log_sigmoid(fz))
    k = (1.0 - lb) * jax.nn.sigmoid(-fz)

    def to_chunks(t):
        return t.reshape(B, NC, C, H, dk).transpose(0, 3, 1, 2, 4)

    q, k, v, log_f = to_chunks(q), to_chunks(k), to_chunks(v), to_chunks(log_f)
    b = jnp.cumsum(log_f, axis=3)
    b_ref = b[:, :, :, C // 2 - 1:C // 2, :]
    b_last = b[:, :, :, -1:, :]
    a = jnp.einsum('bhnid,bhnjd->bhnij', q * jnp.exp(b - b_ref), k * jnp.exp(b_ref - b))
    causal = jnp.tril(jnp.ones((C, C), dtype=bool))
    a = jnp.where(causal, a, 0.0)
    o = jnp.einsum('bhnij,bhnjv->bhniv', a, v)
    ds = jnp.einsum('bhnjd,bhnjv->nbhdv', k * jnp.exp(b_last - b), v)
    decay = jnp.exp(b_last[:, :, :, 0, :]).transpose(2, 0, 1, 3)

    def step(state, inp):
        dec, d_s = inp
        return dec[..., None] * state + d_s, state

    _, s_prev = lax.scan(step, jnp.zeros((B, H, dk, dk), jnp.float32), (decay, ds))
    o = o + jnp.einsum('bhnid,nbhdv->bhniv', q * jnp.exp(b), s_prev)
    o = o * lax.rsqrt(jnp.mean(o * o, axis=-1, keepdims=True) + EPS)
    o = o.transpose(0, 2, 3, 1, 4).reshape(B, S, D)
    o = o * out_gain.astype(jnp.float32) * jax.nn.silu(g.astype(jnp.float32))
    return o.astype(h.dtype) @ w_o


def shared_kv(x, g, w_kv):
    B, S, D = x.shape
    H, dh, BS = B_HEADS, HEAD_DIM, MOBA_BLOCK
    nb = -(-S // BS)
    s_pad = nb * BS
    k, v = jnp.split(rmsnorm(x, g) @ w_kv, 2, axis=-1)

    def heads(t):
        t = t.reshape(B, S, H, dh).transpose(0, 2, 1, 3)
        return jnp.pad(t, ((0, 0), (0, 0), (0, s_pad - S), (0, 0)))

    k, v = heads(k), heads(v)
    k_mean = jnp.mean(k.reshape(B, H, nb, BS, dh).astype(jnp.float32), axis=3)
    return k, v, k_mean


def moba_mixer(h, w_q, w_o, k_pad, v_pad, k_mean):
    B, S, D = h.shape
    H, dh, BS, QC = B_HEADS, HEAD_DIM, MOBA_BLOCK, MOBA_QCHUNK
    nb = k_mean.shape[2]
    scale = HEAD_DIM ** -0.5
    q = (h @ w_q).reshape(B, S, H, dh).transpose(0, 2, 1, 3)
    pos = jnp.arange(S)
    q_blk = pos // BS
    gate = jnp.einsum('bhsd,bhnd->bhsn', q.astype(jnp.float32), k_mean)
    past = jnp.arange(nb)[None, :] < q_blk[:, None]
    gate = jnp.where(past, gate, -jnp.inf)
    kk = min(MOBA_TOPK, nb)
    _, sel = lax.top_k(gate, kk)
    valid = jnp.arange(kk)[None, :] < q_blk[:, None]
    k_blocks = k_pad.reshape(B, H, nb, BS, dh)
    v_blocks = v_pad.reshape(B, H, nb, BS, dh)
    slopes = alibi_slopes(H)
    b_idx = jnp.arange(B)[:, None, None]
    h_idx = jnp.arange(H)[None, :, None]
    r = jnp.arange(BS)

    def attend_chunk(c):
        start = c * QC
        qc = lax.dynamic_slice_in_dim(q, start, QC, axis=2)
        selc = lax.dynamic_slice_in_dim(sel, start, QC, axis=2)
        validc = lax.dynamic_slice_in_dim(valid, start, QC, axis=0)
        tq = start + jnp.arange(QC)
        own_start = (start // BS) * BS
        k_own = lax.dynamic_slice_in_dim(k_pad, own_start, BS, axis=2)
        v_own = lax.dynamic_slice_in_dim(v_pad, own_start, BS, axis=2)
        dist_own = tq[:, None] - (own_start + r)[None, :]
        s_own = (jnp.einsum('bhqd,bhrd->bhqr', qc, k_own).astype(jnp.float32) * scale
                 - slopes[:, None, None] * dist_own)
        scores = [jnp.where(dist_own >= 0, s_own, -jnp.inf)]
        for i in range(kk):
            blk = selc[..., i]
            kg = k_blocks[b_idx, h_idx, blk]
            dist = tq[:, None] - (blk[..., None] * BS + r)
            s = (jnp.einsum('bhqd,bhqrd->bhqr', qc, kg).astype(jnp.float32) * scale
                 - slopes[None, :, None, None] * dist)
            scores.append(jnp.where(validc[:, i][None, None, :, None], s, -jnp.inf))
        p = jax.nn.softmax(jnp.concatenate(scores, axis=-1), axis=-1).astype(v_pad.dtype)
        p = p.reshape(B, H, QC, kk + 1, BS)
        out = jnp.einsum('bhqr,bhrd->bhqd', p[:, :, :, 0], v_own)
        for i in range(kk):
            vg = v_blocks[b_idx, h_idx, selc[..., i]]
            out = out + jnp.einsum('bhqr,bhqrd->bhqd', p[:, :, :, i + 1], vg)
        return out

    outs = lax.map(attend_chunk, jnp.arange(S // QC))
    o = outs.transpose(1, 0, 3, 2, 4).reshape(B, S, D)
    return o @ w_o


def swiglu(h, w_gu, w_down):
    gte, up = jnp.split(h @ w_gu, 2, axis=-1)
    return (jax.nn.silu(gte) * up) @ w_down


def moe_ffn(h, w_router, w_gu, w_down):
    B, S, D = h.shape
    hf = h.reshape(B * S, D)
    logits = (hf @ w_router).astype(jnp.float32)
    top_vals, top_idx = lax.top_k(logits, MOE_TOPK)
    top_w = jax.nn.softmax(top_vals, axis=-1)
    gates = jnp.sum(jax.nn.one_hot(top_idx, N_EXPERTS, dtype=jnp.float32) * top_w[..., None], axis=1)
    gates = gates.astype(h.dtype)
    y = jnp.zeros_like(hf)
    for e in range(N_EXPERTS):
        y = y + gates[:, e:e + 1] * swiglu(hf, w_gu[e], w_down[e])
    return y.reshape(B, S, D)


def setup_inputs(seed: int = 0) -> dict:
    key = jax.random.key(seed)
    ks = jax.random.split(key, 20)
    f32 = jnp.float32

    def dense(k, shape, fan_in):
        return jax.random.normal(k, shape, f32) * (fan_in ** -0.5)

    def gain(k, shape):
        return 1.0 + 0.02 * jax.random.normal(k, shape, f32)

    D = D_MODEL
    return {
        "x": jax.random.normal(ks[0], (BATCH, SEQ, D), f32),
        "attn_norm": gain(ks[1], (DEPTH, D)),
        "ffn_norm": gain(ks[2], (DEPTH, D)),
        "hgrn_w_in": dense(ks[3], (N_A, D, 4 * D), D),
        "hgrn_lb_logits": jax.random.normal(ks[4], (N_A, D), f32),
        "hgrn_out_norm": gain(ks[5], (N_A, D)),
        "hgrn_w_o": dense(ks[6], (N_A, D, D), D),
        "kv_norm": gain(ks[7], (D,)),
        "w_kv": dense(ks[8], (D, 2 * D), D),
        "moba_w_q": dense(ks[9], (N_B, D, D), D),
        "moba_w_o": dense(ks[10], (N_B, D, D), D),
        "ffn_w_gu": dense(ks[11], (N_DENSE, D, 2 * D_FF), D),
        "ffn_w_down": dense(ks[12], (N_DENSE, D_FF, D), D_FF),
        "moe_router": dense(ks[13], (N_MOE, D, N_EXPERTS), D),
        "moe_w_gu": dense(ks[14], (N_MOE, N_EXPERTS, D, 2 * D_FF_EXPERT), D),
        "moe_w_down": dense(ks[15], (N_MOE, N_EXPERTS, D_FF_EXPERT, D), D_FF_EXPERT),
        "final_norm": gain(ks[16], (D,)),
    }


def reference(x, attn_norm, ffn_norm, hgrn_w_in, hgrn_lb_logits, hgrn_out_norm, hgrn_w_o,
              kv_norm, w_kv, moba_w_q, moba_w_o, ffn_w_gu, ffn_w_down,
              moe_router, moe_w_gu, moe_w_down, final_norm):
    lb_p = jax.nn.softmax(hgrn_lb_logits.astype(jnp.float32), axis=0)
    lb_cum = jnp.cumsum(lb_p, axis=0)
    lower_bounds = lb_cum - lb_cum[:1]
    k_pad = v_pad = k_mean = None
    for layer in range(DEPTH):
        h = rmsnorm(x, attn_norm[layer])
        if layer < N_A:
            x = x + hgrn2_mixer(h, hgrn_w_in[layer], lower_bounds[layer],
                                hgrn_out_norm[layer], hgrn_w_o[layer])
        else:
            if layer == N_A:
                k_pad, v_pad, k_mean = shared_kv(x, kv_norm, w_kv)
            j = layer - N_A
            x = x + moba_mixer(h, moba_w_q[j], moba_w_o[j], k_pad, v_pad, k_mean)
        h = rmsnorm(x, ffn_norm[layer])
        if layer % 2 == 0:
            x = x + swiglu(h, ffn_w_gu[layer // 2], ffn_w_down[layer // 2])
        else:
            x = x + moe_ffn(h, moe_router[layer // 2], moe_w_gu[layer // 2], moe_w_down[layer // 2])
    return rmsnorm(x, final_norm)
```

```python
import functools

import jax
import jax.numpy as jnp
from jax import lax
from jax.experimental import pallas as pl
from jax.experimental.pallas import tpu as pltpu

F32 = jnp.float32
BF16 = jnp.bfloat16

EPS = 1e-6
HEAD_DIM = 128
HGRN_CHUNK = 64
MOBA_BLOCK = 256
MOBA_TOPK = 3
MOE_TOPK = 2
LANES = 128
VMEM_LIMIT = 56 * 1024 * 1024

HGRN_TILE = 512
ROW_TILE = 512


def _params(*semantics):
    return pltpu.CompilerParams(dimension_semantics=semantics, vmem_limit_bytes=VMEM_LIMIT)


def _rms(x, gain):
    return x * lax.rsqrt(jnp.mean(x * x, axis=-1, keepdims=True) + EPS) * gain


def _dot(a, b):
    return jnp.dot(a, b, preferred_element_type=F32)


def _dot_nt(a, b):
    return lax.dot_general(a, b, (((1,), (1,)), ((), ())), preferred_element_type=F32)


def _dot_tn(a, b):
    return lax.dot_general(a, b, (((0,), (0,)), ((), ())), preferred_element_type=F32)


def _silu(x):
    return x * jax.nn.sigmoid(x)


def _norm_matmul_kernel(x_ref, g_ref, w_ref, o_ref):
    h = _rms(x_ref[...], g_ref[...]).astype(BF16)
    o_ref[...] = _dot(h, w_ref[...]).astype(o_ref.dtype)


def norm_matmul(x, gain, w, out_dtype):
    t, d = x.shape
    n = w.shape[1]
    return pl.pallas_call(
        _norm_matmul_kernel,
        grid=(t // ROW_TILE,),
        in_specs=[
            pl.BlockSpec((ROW_TILE, d), lambda i: (i, 0)),
            pl.BlockSpec((1, d), lambda i: (0, 0)),
            pl.BlockSpec((d, n), lambda i: (0, 0)),
        ],
        out_specs=pl.BlockSpec((ROW_TILE, n), lambda i: (i, 0)),
        out_shape=jax.ShapeDtypeStruct((t, n), out_dtype),
        compiler_params=_params("parallel"),
    )(x, gain.reshape(1, d), w)


def _kv_kernel(x_ref, g_ref, w_ref, kv_ref, km_ref):
    d = x_ref.shape[1]
    h = _rms(x_ref[...], g_ref[...]).astype(BF16)
    kv = _dot(h, w_ref[...])
    kv_ref[...] = kv.astype(kv_ref.dtype)
    km_ref[0] = jnp.mean(kv[:, :d], axis=0, keepdims=True)


def shared_kv(x, gain, w_kv):
    t, d = x.shape
    nblk = t // MOBA_BLOCK
    return pl.pallas_call(
        _kv_kernel,
        grid=(nblk,),
        in_specs=[
            pl.BlockSpec((MOBA_BLOCK, d), lambda i: (i, 0)),
            pl.BlockSpec((1, d), lambda i: (0, 0)),
            pl.BlockSpec((d, 2 * d), lambda i: (0, 0)),
        ],
        out_specs=[
            pl.BlockSpec((MOBA_BLOCK, 2 * d), lambda i: (i, 0)),
            pl.BlockSpec((1, 1, d), lambda i: (i, 0, 0)),
        ],
        out_shape=[
            jax.ShapeDtypeStruct((t, 2 * d), BF16),
            jax.ShapeDtypeStruct((nblk, 1, d), F32),
        ],
        compiler_params=_params("parallel"),
    )(x, gain.reshape(1, d), w_kv)


def _matmul_residual_kernel(x_ref, a_ref, w_ref, o_ref):
    o_ref[...] = x_ref[...] + _dot(a_ref[...], w_ref[...])


def matmul_residual(x, a, w):
    t, d = x.shape
    k = a.shape[1]
    return pl.pallas_call(
        _matmul_residual_kernel,
        grid=(t // ROW_TILE,),
        in_specs=[
            pl.BlockSpec((ROW_TILE, d), lambda i: (i, 0)),
            pl.BlockSpec((ROW_TILE, k), lambda i: (i, 0)),
            pl.BlockSpec((k, d), lambda i: (0, 0)),
        ],
        out_specs=pl.BlockSpec((ROW_TILE, d), lambda i: (i, 0)),
        out_shape=jax.ShapeDtypeStruct((t, d), F32),
        compiler_params=_params("parallel"),
    )(x, a, w)


def _hgrn_kernel(q_ref, f_ref, i_ref, g_ref, lbl_ref, gain_ref, o_ref, st_ref, *, layer):
    c = HGRN_CHUNK
    n_chunks = q_ref.shape[0] // c

    @pl.when(pl.program_id(2) == 0)
    def _():
        st_ref[...] = jnp.zeros_like(st_ref)

    logits = lbl_ref[...]
    e = jnp.exp(logits - jnp.max(logits, axis=0, keepdims=True))
    p = e / jnp.sum(e, axis=0, keepdims=True)
    if layer > 0:
        lb = jnp.sum(p[1:layer + 1], axis=0, keepdims=True)
        log_lb = jnp.log(lb)
        log_1m_lb = jnp.log1p(-lb)

    row = lax.broadcasted_iota(jnp.int32, (c, c), 0)
    col = lax.broadcasted_iota(jnp.int32, (c, c), 1)
    causal = col <= row
    tril = causal.astype(F32)
    gain = gain_ref[...]

    for n in range(n_chunks):
        rows = pl.ds(n * c, c)
        q = _silu(q_ref[rows, :])
        fz = f_ref[rows, :]
        v = i_ref[rows, :]
        log_sig = jnp.minimum(fz, 0.0) - jnp.log1p(jnp.exp(-jnp.abs(fz)))
        if layer > 0:
            y = log_1m_lb + log_sig
            log_f = jnp.maximum(log_lb, y) + jnp.log1p(jnp.exp(-jnp.abs(log_lb - y)))
            k = (1.0 - lb) * jax.nn.sigmoid(-fz)
        else:
            log_f = log_sig
            k = jax.nn.sigmoid(-fz)
        b = jnp.dot(tril, log_f, precision=lax.Precision.HIGHEST, preferred_element_type=F32)
        b_ref = b[c // 2 - 1:c // 2, :]
        b_last = b[c - 1:c, :]
        qd = (q * jnp.exp(b - b_ref)).astype(BF16)
        kd = (k * jnp.exp(b_ref - b)).astype(BF16)
        a = jnp.where(causal, _dot_nt(qd, kd), 0.0)
        vb = v.astype(BF16)
        o = _dot(a.astype(BF16), vb)
        st = st_ref[...]
        o = o + _dot_nt((q * jnp.exp(b)).astype(BF16), st.astype(BF16))
        kl = (k * jnp.exp(b_last - b)).astype(BF16)
        st_ref[...] = jnp.exp(b_last) * st + _dot_tn(vb, kl)
        o = o * lax.rsqrt(jnp.mean(o * o, axis=-1, keepdims=True) + EPS)
        o = o * gain * _silu(g_ref[rows, :])
        o_ref[rows, :] = o.astype(o_ref.dtype)


def hgrn_recurrence(proj, lb_logits, out_gain, *, layer, batch, seq):
    t = proj.shape[0]
    d = proj.shape[1] // 4
    heads = d // HEAD_DIM
    tiles = seq // HGRN_TILE
    n_layers = lb_logits.shape[0]

    def act(part):
        return pl.BlockSpec((HGRN_TILE, HEAD_DIM),
                            lambda b, h, s: (b * tiles + s, part * heads + h))

    return pl.pallas_call(
        functools.partial(_hgrn_kernel, layer=layer),
        grid=(batch, heads, tiles),
        in_specs=[
            act(0), act(1), act(2), act(3),
            pl.BlockSpec((n_layers, HEAD_DIM), lambda b, h, s: (0, h)),
            pl.BlockSpec((1, HEAD_DIM), lambda b, h, s: (0, h)),
        ],
        out_specs=pl.BlockSpec((HGRN_TILE, HEAD_DIM), lambda b, h, s: (b * tiles + s, h)),
        out_shape=jax.ShapeDtypeStruct((t, d), BF16),
        scratch_shapes=[pltpu.VMEM((HEAD_DIM, HEAD_DIM), F32)],
        compiler_params=_params("parallel", "parallel", "arbitrary"),
    )(proj, proj, proj, proj, lb_logits, out_gain.reshape(1, d))


def _moba_kernel(q_ref, k_ref, v_ref, km_ref, slope_ref, o_ref):
    bs = MOBA_BLOCK
    qb = pl.program_id(2)
    scale = HEAD_DIM ** -0.5
    q = q_ref[...]
    q16 = q.astype(BF16)

    km = km_ref[0]
    n_blk = km.shape[0]
    km_rep = jnp.concatenate([km] * (LANES // n_blk), axis=0)
    gate = lax.dot_general(q, km_rep, (((1,), (1,)), ((), ())),
                           precision=lax.Precision.HIGHEST, preferred_element_type=F32)
    lane = lax.broadcasted_iota(jnp.int32, (bs, LANES), 1)
    blk = lane % n_blk
    past = blk < qb
    rank = jnp.zeros((bs, LANES), F32)
    for s in range(1, n_blk):
        g_o = pltpu.roll(gate, s, axis=1)
        b_o = pltpu.roll(blk, s, axis=1)
        beats = (b_o < qb) & ((g_o > gate) | ((g_o == gate) & (b_o < blk)))
        rank = rank + beats.astype(F32)
    sel = (past & (rank < MOBA_TOPK)).astype(F32)

    rel = (lax.broadcasted_iota(jnp.int32, (bs, bs), 0)
           - lax.broadcasted_iota(jnp.int32, (bs, bs), 1))
    slope = slope_ref[0][0:1, 0:1]
    bias = -slope * rel.astype(F32)

    own = pl.ds(pl.multiple_of(qb * bs, bs), bs)
    s_own = _dot_nt(q16, k_ref[own, :]) * scale + bias
    s_own = jnp.where(rel >= 0, s_own, -jnp.inf)
    m0 = jnp.max(s_own, axis=1, keepdims=True)
    p0 = jnp.exp(s_own - m0)
    l0 = jnp.sum(p0, axis=1, keepdims=True)
    acc0 = _dot(p0.astype(BF16), v_ref[own, :])

    def body(kb, carry):
        m, l, acc = carry
        rows = pl.ds(pl.multiple_of(kb * bs, bs), bs)
        dist0 = ((qb - kb) * bs).astype(F32)
        s = _dot_nt(q16, k_ref[rows, :]) * scale + (bias - slope * dist0)
        chosen = jnp.max(jnp.where(lane == kb, sel, 0.0), axis=1, keepdims=True) > 0.0
        s = jnp.where(chosen, s, -jnp.inf)
        m_new = jnp.maximum(m, jnp.max(s, axis=1, keepdims=True))
        alpha = jnp.exp(m - m_new)
        p = jnp.exp(s - m_new)
        l = alpha * l + jnp.sum(p, axis=1, keepdims=True)
        acc = alpha * acc + _dot(p.astype(BF16), v_ref[rows, :])
        return m_new, l, acc

    _, l, acc = lax.fori_loop(0, qb, body, (m0, l0, acc0))
    o_ref[...] = (acc / l).astype(o_ref.dtype)


def moba_attention(q, kv, k_mean, *, batch, seq):
    t, d = q.shape
    heads = d // HEAD_DIM
    n_blk = seq // MOBA_BLOCK
    slopes = (2.0 ** (-8.0 * jnp.arange(1, heads + 1, dtype=F32) / heads))
    slopes = jnp.broadcast_to(slopes[:, None, None], (heads, 1, LANES))
    return pl.pallas_call(
        _moba_kernel,
        grid=(batch, heads, n_blk),
        in_specs=[
            pl.BlockSpec((MOBA_BLOCK, HEAD_DIM), lambda b, h, i: (b * n_blk + i, h)),
            pl.BlockSpec((seq, HEAD_DIM), lambda b, h, i: (b, h)),
            pl.BlockSpec((seq, HEAD_DIM), lambda b, h, i: (b, heads + h)),
            pl.BlockSpec((1, n_blk, HEAD_DIM), lambda b, h, i: (b, 0, h)),
            pl.BlockSpec((1, 1, LANES), lambda b, h, i: (h, 0, 0)),
        ],
        out_specs=pl.BlockSpec((MOBA_BLOCK, HEAD_DIM), lambda b, h, i: (b * n_blk + i, h)),
        out_shape=jax.ShapeDtypeStruct((t, d), BF16),
        compiler_params=_params("parallel", "parallel", "arbitrary"),
    )(q, kv, kv, k_mean.reshape(batch, n_blk, d), slopes)


def _router_kernel(x_ref, g_ref, w_ref, o_ref, *, n_experts):
    h = _rms(x_ref[...], g_ref[...])
    logits = jnp.dot(h, w_ref[...], precision=lax.Precision.HIGHEST, preferred_element_type=F32)
    lane = lax.broadcasted_iota(jnp.int32, logits.shape, 1)
    neg = -jnp.inf
    lg = jnp.where(lane < n_experts, logits, neg)
    m1 = jnp.max(lg, axis=1, keepdims=True)
    i1 = jnp.min(jnp.where(lg == m1, lane, LANES), axis=1, keepdims=True)
    lg2 = jnp.where(lane == i1, neg, lg)
    m2 = jnp.max(lg2, axis=1, keepdims=True)
    i2 = jnp.min(jnp.where(lg2 == m2, lane, LANES), axis=1, keepdims=True)
    e2 = jnp.exp(m2 - m1)
    den = 1.0 + e2
    o_ref[...] = jnp.where(lane == i1, 1.0 / den, 0.0) + jnp.where(lane == i2, e2 / den, 0.0)


def router_gates(x, gain, w_router):
    t, d = x.shape
    n_experts = w_router.shape[1]
    w = jnp.pad(w_router, ((0, 0), (0, LANES - n_experts)))
    return pl.pallas_call(
        functools.partial(_router_kernel, n_experts=n_experts),
        grid=(t // ROW_TILE,),
        in_specs=[
            pl.BlockSpec((ROW_TILE, d), lambda i: (i, 0)),
            pl.BlockSpec((1, d), lambda i: (0, 0)),
            pl.BlockSpec((d, LANES), lambda i: (0, 0)),
        ],
        out_specs=pl.BlockSpec((ROW_TILE, LANES), lambda i: (i, 0)),
        out_shape=jax.ShapeDtypeStruct((t, LANES), F32),
        compiler_params=_params("parallel"),
    )(x, gain.reshape(1, d), w)


def _ffn_kernel(*refs, gated):
    if gated:
        x_ref, g_ref, gates_ref, wg_ref, wu_ref, wd_ref, o_ref, h_ref, acc_ref = refs
    else:
        x_ref, g_ref, wg_ref, wu_ref, wd_ref, o_ref, h_ref, acc_ref = refs
    e = pl.program_id(1)
    f = pl.program_id(2)

    @pl.when((e == 0) & (f == 0))
    def _():
        x = x_ref[...]
        h_ref[...] = _rms(x, g_ref[...]).astype(BF16)
        acc_ref[...] = x

    h = h_ref[...]
    act = _silu(_dot(h, wg_ref[...])) * _dot(h, wu_ref[...])
    if gated:
        gates = gates_ref[...]
        lane = lax.broadcasted_iota(jnp.int32, gates.shape, 1)
        act = act * jnp.sum(jnp.where(lane == e, gates, 0.0), axis=1, keepdims=True)
    acc_ref[...] += _dot(act.astype(BF16), wd_ref[...])

    @pl.when((e == pl.num_programs(1) - 1) & (f == pl.num_programs(2) - 1))
    def _():
        o_ref[...] = acc_ref[...]


def ffn(x, gain, w_gu, w_down, gates, *, f_tile):
    t, d = x.shape
    n_e, ff, _ = w_down.shape
    nf = ff // f_tile
    gated = gates is not None
    in_specs = [
        pl.BlockSpec((ROW_TILE, d), lambda i, e, f: (i, 0)),
        pl.BlockSpec((1, d), lambda i, e, f: (0, 0)),
    ]
    args = [x, gain.reshape(1, d)]
    if gated:
        in_specs.append(pl.BlockSpec((ROW_TILE, LANES), lambda i, e, f: (i, 0)))
        args.append(gates)
    in_specs += [
        pl.BlockSpec((None, d, f_tile), lambda i, e, f: (e, 0, f)),
        pl.BlockSpec((None, d, f_tile), lambda i, e, f: (e, 0, nf + f)),
        pl.BlockSpec((None, f_tile, d), lambda i, e, f: (e, f, 0)),
    ]
    args += [w_gu, w_gu, w_down]
    return pl.pallas_call(
        functools.partial(_ffn_kernel, gated=gated),
        grid=(t // ROW_TILE, n_e, nf),
        in_specs=in_specs,
        out_specs=pl.BlockSpec((ROW_TILE, d), lambda i, e, f: (i, 0)),
        out_shape=jax.ShapeDtypeStruct((t, d), F32),
        scratch_shapes=[pltpu.VMEM((ROW_TILE, d), BF16), pltpu.VMEM((ROW_TILE, d), F32)],
        compiler_params=_params("parallel", "arbitrary", "arbitrary"),
    )(*args)


def _rmsnorm_kernel(x_ref, g_ref, o_ref):
    o_ref[...] = _rms(x_ref[...], g_ref[...])


def rmsnorm(x, gain):
    t, d = x.shape
    return pl.pallas_call(
        _rmsnorm_kernel,
        grid=(t // ROW_TILE,),
        in_specs=[pl.BlockSpec((ROW_TILE, d), lambda i: (i, 0)),
                  pl.BlockSpec((1, d), lambda i: (0, 0))],
        out_specs=pl.BlockSpec((ROW_TILE, d), lambda i: (i, 0)),
        out_shape=jax.ShapeDtypeStruct((t, d), F32),
        compiler_params=_params("parallel"),
    )(x, gain.reshape(1, d))


def _lane_tile(n, target):
    best = LANES
    for cand in range(LANES, target + 1, LANES):
        if n % cand == 0:
            best = cand
    return best


def moe_layer(x, gain, w_router, w_gu, w_down):
    gates = router_gates(x, gain, w_router)
    return ffn(x, gain, w_gu.astype(BF16), w_down.astype(BF16), gates,
               f_tile=_lane_tile(w_down.shape[1], 1024))


def kernel(x, attn_norm, ffn_norm, hgrn_w_in, hgrn_lb_logits, hgrn_out_norm, hgrn_w_o, kv_norm, w_kv, moba_w_q, moba_w_o, ffn_w_gu, ffn_w_down, moe_router, moe_w_gu, moe_w_down, final_norm):
    batch, seq, d = x.shape
    depth = attn_norm.shape[0]
    n_a = hgrn_w_in.shape[0]
    x = x.reshape(batch * seq, d)
    kv = k_mean = None
    for layer in range(depth):
        if layer < n_a:
            proj = norm_matmul(x, attn_norm[layer], hgrn_w_in[layer].astype(BF16), F32)
            o = hgrn_recurrence(proj, hgrn_lb_logits, hgrn_out_norm[layer],
                                layer=layer, batch=batch, seq=seq)
            x = matmul_residual(x, o, hgrn_w_o[layer].astype(BF16))
        else:
            if layer == n_a:
                kv, k_mean = shared_kv(x, kv_norm, w_kv.astype(BF16))
            j = layer - n_a
            q = norm_matmul(x, attn_norm[layer], moba_w_q[j].astype(BF16), F32)
            o = moba_attention(q, kv, k_mean, batch=batch, seq=seq)
            x = matmul_residual(x, o, moba_w_o[j].astype(BF16))
        if layer % 2 == 0:
            j = layer // 2
            x = ffn(x, ffn_norm[layer], ffn_w_gu[j][None].astype(BF16),
                    ffn_w_down[j][None].astype(BF16), None,
                    f_tile=_lane_tile(ffn_w_down.shape[1], 1536))
        else:
            j = layer // 2
            x = moe_layer(x, ffn_norm[layer], moe_router[j], moe_w_gu[j], moe_w_down[j])
    return rmsnorm(x, final_norm).reshape(batch, seq, d)
```

```python
import functools

import jax
import jax.numpy as jnp
from jax import lax
from jax.experimental import pallas as pl
from jax.experimental.pallas import tpu as pltpu

F32 = jnp.float32
BF16 = jnp.bfloat16

EPS = 1e-6
HEAD_DIM = 128
HGRN_CHUNK = 64
MOBA_BLOCK = 256
MOBA_TOPK = 3
MOE_TOPK = 2
LANES = 128
VMEM_LIMIT = 56 * 1024 * 1024

HGRN_TILE = 512
ROW_TILE = 512
TOK_TILE = 512
FFN_ROWS = 512
SUB_ROWS = 128
ROW_ALIGN = 16


def _params(*semantics):
    return pltpu.CompilerParams(dimension_semantics=semantics, vmem_limit_bytes=VMEM_LIMIT)


def _rms(x, gain):
    return x * lax.rsqrt(jnp.mean(x * x, axis=-1, keepdims=True) + EPS) * gain


def _dot(a, b):
    return jnp.dot(a, b, preferred_element_type=F32)


def _dot_nt(a, b):
    return lax.dot_general(a, b, (((1,), (1,)), ((), ())), preferred_element_type=F32)


def _dot_tn(a, b):
    return lax.dot_general(a, b, (((0,), (0,)), ((), ())), preferred_element_type=F32)


def _silu(x):
    return x * jax.nn.sigmoid(x)


def _norm_matmul_kernel(x_ref, g_ref, w_ref, o_ref):
    h = _rms(x_ref[...], g_ref[...]).astype(BF16)
    o_ref[...] = _dot(h, w_ref[...]).astype(o_ref.dtype)


def norm_matmul(x, gain, w, out_dtype):
    t, d = x.shape
    n = w.shape[1]
    return pl.pallas_call(
        _norm_matmul_kernel,
        grid=(t // ROW_TILE,),
        in_specs=[
            pl.BlockSpec((ROW_TILE, d), lambda i: (i, 0)),
            pl.BlockSpec((1, d), lambda i: (0, 0)),
            pl.BlockSpec((d, n), lambda i: (0, 0)),
        ],
        out_specs=pl.BlockSpec((ROW_TILE, n), lambda i: (i, 0)),
        out_shape=jax.ShapeDtypeStruct((t, n), out_dtype),
        compiler_params=_params("parallel"),
    )(x, gain.reshape(1, d), w)


def _kv_kernel(x_ref, g_ref, w_ref, kv_ref, km_ref):
    d = x_ref.shape[1]
    h = _rms(x_ref[...], g_ref[...]).astype(BF16)
    kv = _dot(h, w_ref[...])
    kv_ref[...] = kv.astype(kv_ref.dtype)
    km_ref[0] = jnp.mean(kv[:, :d], axis=0, keepdims=True)


def shared_kv(x, gain, w_kv):
    t, d = x.shape
    nblk = t // MOBA_BLOCK
    return pl.pallas_call(
        _kv_kernel,
        grid=(nblk,),
        in_specs=[
            pl.BlockSpec((MOBA_BLOCK, d), lambda i: (i, 0)),
            pl.BlockSpec((1, d), lambda i: (0, 0)),
            pl.BlockSpec((d, 2 * d), lambda i: (0, 0)),
        ],
        out_specs=[
            pl.BlockSpec((MOBA_BLOCK, 2 * d), lambda i: (i, 0)),
            pl.BlockSpec((1, 1, d), lambda i: (i, 0, 0)),
        ],
        out_shape=[
            jax.ShapeDtypeStruct((t, 2 * d), BF16),
            jax.ShapeDtypeStruct((nblk, 1, d), F32),
        ],
        compiler_params=_params("parallel"),
    )(x, gain.reshape(1, d), w_kv)


def _matmul_residual_kernel(x_ref, a_ref, w_ref, o_ref):
    o_ref[...] = x_ref[...] + _dot(a_ref[...], w_ref[...])


def matmul_residual(x, a, w):
    t, d = x.shape
    k = a.shape[1]
    return pl.pallas_call(
        _matmul_residual_kernel,
        grid=(t // ROW_TILE,),
        in_specs=[
            pl.BlockSpec((ROW_TILE, d), lambda i: (i, 0)),
            pl.BlockSpec((ROW_TILE, k), lambda i: (i, 0)),
            pl.BlockSpec((k, d), lambda i: (0, 0)),
        ],
        out_specs=pl.BlockSpec((ROW_TILE, d), lambda i: (i, 0)),
        out_shape=jax.ShapeDtypeStruct((t, d), F32),
        compiler_params=_params("parallel"),
    )(x, a, w)


def _hgrn_kernel(q_ref, f_ref, i_ref, g_ref, lbl_ref, gain_ref, o_ref, st_ref, *, layer):
    c = HGRN_CHUNK
    n_chunks = q_ref.shape[0] // c

    @pl.when(pl.program_id(2) == 0)
    def _():
        st_ref[...] = jnp.zeros_like(st_ref)

    logits = lbl_ref[...]
    e = jnp.exp(logits - jnp.max(logits, axis=0, keepdims=True))
    p = e / jnp.sum(e, axis=0, keepdims=True)
    if layer > 0:
        lb = jnp.sum(p[1:layer + 1], axis=0, keepdims=True)
        log_lb = jnp.log(lb)
        log_1m_lb = jnp.log1p(-lb)

    row = lax.broadcasted_iota(jnp.int32, (c, c), 0)
    col = lax.broadcasted_iota(jnp.int32, (c, c), 1)
    causal = col <= row
    tril = causal.astype(F32)
    gain = gain_ref[...]

    for n in range(n_chunks):
        rows = pl.ds(n * c, c)
        q = _silu(q_ref[rows, :])
        fz = f_ref[rows, :]
        v = i_ref[rows, :]
        log_sig = jnp.minimum(fz, 0.0) - jnp.log1p(jnp.exp(-jnp.abs(fz)))
        if layer > 0:
            y = log_1m_lb + log_sig
            log_f = jnp.maximum(log_lb, y) + jnp.log1p(jnp.exp(-jnp.abs(log_lb - y)))
            k = (1.0 - lb) * jax.nn.sigmoid(-fz)
        else:
            log_f = log_sig
            k = jax.nn.sigmoid(-fz)
        b = jnp.dot(tril, log_f, precision=lax.Precision.HIGHEST, preferred_element_type=F32)
        b_ref = b[c // 2 - 1:c // 2, :]
        b_last = b[c - 1:c, :]
        qd = (q * jnp.exp(b - b_ref)).astype(BF16)
        kd = (k * jnp.exp(b_ref - b)).astype(BF16)
        a = jnp.where(causal, _dot_nt(qd, kd), 0.0)
        vb = v.astype(BF16)
        o = _dot(a.astype(BF16), vb)
        st = st_ref[...]
        o = o + _dot_nt((q * jnp.exp(b)).astype(BF16), st.astype(BF16))
        kl = (k * jnp.exp(b_last - b)).astype(BF16)
        st_ref[...] = jnp.exp(b_last) * st + _dot_tn(vb, kl)
        o = o * lax.rsqrt(jnp.mean(o * o, axis=-1, keepdims=True) + EPS)
        o = o * gain * _silu(g_ref[rows, :])
        o_ref[rows, :] = o.astype(o_ref.dtype)


def hgrn_recurrence(proj, lb_logits, out_gain, *, layer, batch, seq):
    t = proj.shape[0]
    d = proj.shape[1] // 4
    heads = d // HEAD_DIM
    tiles = seq // HGRN_TILE
    n_layers = lb_logits.shape[0]

    def act(part):
        return pl.BlockSpec((HGRN_TILE, HEAD_DIM),
                            lambda b, h, s: (b * tiles + s, part * heads + h))

    return pl.pallas_call(
        functools.partial(_hgrn_kernel, layer=layer),
        grid=(batch, heads, tiles),
        in_specs=[
            act(0), act(1), act(2), act(3),
            pl.BlockSpec((n_layers, HEAD_DIM), lambda b, h, s: (0, h)),
            pl.BlockSpec((1, HEAD_DIM), lambda b, h, s: (0, h)),
        ],
        out_specs=pl.BlockSpec((HGRN_TILE, HEAD_DIM), lambda b, h, s: (b * tiles + s, h)),
        out_shape=jax.ShapeDtypeStruct((t, d), BF16),
        scratch_shapes=[pltpu.VMEM((HEAD_DIM, HEAD_DIM), F32)],
        compiler_params=_params("parallel", "parallel", "arbitrary"),
    )(proj, proj, proj, proj, lb_logits, out_gain.reshape(1, d))


def _moba_kernel(q_ref, k_ref, v_ref, km_ref, slope_ref, o_ref):
    bs = MOBA_BLOCK
    qb = pl.program_id(2)
    scale = HEAD_DIM ** -0.5
    q = q_ref[...]
    q16 = q.astype(BF16)

    km = km_ref[0]
    n_blk = km.shape[0]
    km_rep = jnp.concatenate([km] * (LANES // n_blk), axis=0)
    gate = lax.dot_general(q, km_rep, (((1,), (1,)), ((), ())),
                           precision=lax.Precision.HIGHEST, preferred_element_type=F32)
    lane = lax.broadcasted_iota(jnp.int32, (bs, LANES), 1)
    blk = lane % n_blk
    past = blk < qb
    rank = jnp.zeros((bs, LANES), F32)
    for s in range(1, n_blk):
        g_o = pltpu.roll(gate, s, axis=1)
        b_o = pltpu.roll(blk, s, axis=1)
        beats = (b_o < qb) & ((g_o > gate) | ((g_o == gate) & (b_o < blk)))
        rank = rank + beats.astype(F32)
    sel = (past & (rank < MOBA_TOPK)).astype(F32)

    rel = (lax.broadcasted_iota(jnp.int32, (bs, bs), 0)
           - lax.broadcasted_iota(jnp.int32, (bs, bs), 1))
    slope = slope_ref[0][0:1, 0:1]
    bias = -slope * rel.astype(F32)

    own = pl.ds(pl.multiple_of(qb * bs, bs), bs)
    s_own = _dot_nt(q16, k_ref[own, :]) * scale + bias
    s_own = jnp.where(rel >= 0, s_own, -jnp.inf)
    m0 = jnp.max(s_own, axis=1, keepdims=True)
    p0 = jnp.exp(s_own - m0)
    l0 = jnp.sum(p0, axis=1, keepdims=True)
    acc0 = _dot(p0.astype(BF16), v_ref[own, :])

    def body(kb, carry):
        m, l, acc = carry
        rows = pl.ds(pl.multiple_of(kb * bs, bs), bs)
        dist0 = ((qb - kb) * bs).astype(F32)
        s = _dot_nt(q16, k_ref[rows, :]) * scale + (bias - slope * dist0)
        chosen = jnp.max(jnp.where(lane == kb, sel, 0.0), axis=1, keepdims=True) > 0.0
        s = jnp.where(chosen, s, -jnp.inf)
        m_new = jnp.maximum(m, jnp.max(s, axis=1, keepdims=True))
        alpha = jnp.exp(m - m_new)
        p = jnp.exp(s - m_new)
        l = alpha * l + jnp.sum(p, axis=1, keepdims=True)
        acc = alpha * acc + _dot(p.astype(BF16), v_ref[rows, :])
        return m_new, l, acc

    _, l, acc = lax.fori_loop(0, qb, body, (m0, l0, acc0))
    o_ref[...] = (acc / l).astype(o_ref.dtype)


def moba_attention(q, kv, k_mean, *, batch, seq):
    t, d = q.shape
    heads = d // HEAD_DIM
    n_blk = seq // MOBA_BLOCK
    slopes = (2.0 ** (-8.0 * jnp.arange(1, heads + 1, dtype=F32) / heads))
    slopes = jnp.broadcast_to(slopes[:, None, None], (heads, 1, LANES))
    return pl.pallas_call(
        _moba_kernel,
        grid=(batch, heads, n_blk),
        in_specs=[
            pl.BlockSpec((MOBA_BLOCK, HEAD_DIM), lambda b, h, i: (b * n_blk + i, h)),
            pl.BlockSpec((seq, HEAD_DIM), lambda b, h, i: (b, h)),
            pl.BlockSpec((seq, HEAD_DIM), lambda b, h, i: (b, heads + h)),
            pl.BlockSpec((1, n_blk, HEAD_DIM), lambda b, h, i: (b, 0, h)),
            pl.BlockSpec((1, 1, LANES), lambda b, h, i: (h, 0, 0)),
        ],
        out_specs=pl.BlockSpec((MOBA_BLOCK, HEAD_DIM), lambda b, h, i: (b * n_blk + i, h)),
        out_shape=jax.ShapeDtypeStruct((t, d), BF16),
        compiler_params=_params("parallel", "parallel", "arbitrary"),
    )(q, kv, kv, k_mean.reshape(batch, n_blk, d), slopes)


def _router_kernel(x_ref, g_ref, w_ref, h_ref, gates_ref, routed_ref, cnt_ref, *, n_experts):
    h = _rms(x_ref[...], g_ref[...])
    h_ref[...] = h.astype(BF16)
    logits = jnp.dot(h, w_ref[...], precision=lax.Precision.HIGHEST, preferred_element_type=F32)
    lane = lax.broadcasted_iota(jnp.int32, logits.shape, 1)
    neg = -jnp.inf
    lg = jnp.where(lane < n_experts, logits, neg)
    m1 = jnp.max(lg, axis=1, keepdims=True)
    i1 = jnp.min(jnp.where(lg == m1, lane, LANES), axis=1, keepdims=True)
    lg2 = jnp.where(lane == i1, neg, lg)
    m2 = jnp.max(lg2, axis=1, keepdims=True)
    i2 = jnp.min(jnp.where(lg2 == m2, lane, LANES), axis=1, keepdims=True)
    e2 = jnp.exp(m2 - m1)
    den = 1.0 + e2
    gates_ref[...] = jnp.where(lane == i1, 1.0 / den, 0.0) + jnp.where(lane == i2, e2 / den, 0.0)
    routed = ((lane == i1) | (lane == i2)).astype(F32)
    routed_ref[...] = routed.astype(BF16)
    cnt_ref[0] = jnp.sum(routed, axis=0, keepdims=True).astype(jnp.int32)


def router(x, gain, w_router):
    t, d = x.shape
    n_experts = w_router.shape[1]
    n_tiles = t // TOK_TILE
    w = jnp.pad(w_router, ((0, 0), (0, LANES - n_experts)))
    return pl.pallas_call(
        functools.partial(_router_kernel, n_experts=n_experts),
        grid=(n_tiles,),
        in_specs=[
            pl.BlockSpec((TOK_TILE, d), lambda i: (i, 0)),
            pl.BlockSpec((1, d), lambda i: (0, 0)),
            pl.BlockSpec((d, LANES), lambda i: (0, 0)),
        ],
        out_specs=[
            pl.BlockSpec((TOK_TILE, d), lambda i: (i, 0)),
            pl.BlockSpec((TOK_TILE, LANES), lambda i: (i, 0)),
            pl.BlockSpec((TOK_TILE, LANES), lambda i: (i, 0)),
            pl.BlockSpec((1, 1, LANES), lambda i: (i, 0, 0)),
        ],
        out_shape=[
            jax.ShapeDtypeStruct((t, d), BF16),
            jax.ShapeDtypeStruct((t, LANES), F32),
            jax.ShapeDtypeStruct((t, LANES), BF16),
            jax.ShapeDtypeStruct((n_tiles, 1, LANES), jnp.int32),
        ],
        compiler_params=_params("parallel"),
    )(x, gain.reshape(1, d), w)


def _rank_in_tile(routed):
    tm = routed.shape[0]
    earlier = (lax.broadcasted_iota(jnp.int32, (tm, tm), 1)
               < lax.broadcasted_iota(jnp.int32, (tm, tm), 0)).astype(BF16)
    return _dot(earlier, routed)


def _slot_onehot(rank, routed, e, k):
    tm = rank.shape[0]
    slot = (lax.broadcasted_iota(jnp.int32, (tm, SUB_ROWS), 1) + k * SUB_ROWS).astype(F32)
    hit = (slot == rank[:, e:e + 1]) & (routed[:, e:e + 1] > 0)
    return hit.astype(BF16)


def _dispatch_kernel(rs_ref, n_ref, h_ref, routed_ref, _, xs_ref, slab_ref, sem, *, n_experts):
    i = pl.program_id(0)
    tm = h_ref.shape[0]
    routed = routed_ref[...]
    rank = _rank_in_tile(routed)
    h = h_ref[...]

    def copies(e, wait):
        n = n_ref[i * n_experts + e]
        base = rs_ref[i * n_experts + e]
        off = jnp.int32(0)
        size = tm
        while size >= ROW_ALIGN:
            @pl.when((n & size) != 0)
            def _(off=off, size=size):
                cp = pltpu.make_async_copy(
                    slab_ref.at[e, pl.ds(pl.multiple_of(off, ROW_ALIGN), size)],
                    xs_ref.at[pl.ds(pl.multiple_of(base + off, ROW_ALIGN), size)],
                    sem.at[e])
                if wait:
                    cp.wait()
                else:
                    cp.start()
            off = off + (n & size)
            size //= 2

    for e in range(n_experts):
        n = n_ref[i * n_experts + e]
        for k in range(tm // SUB_ROWS):
            @pl.when(k * SUB_ROWS < n)
            def _(e=e, k=k):
                onehot = _slot_onehot(rank, routed, e, k)
                slab_ref[e, k * SUB_ROWS:(k + 1) * SUB_ROWS, :] = _dot_tn(onehot, h).astype(BF16)
        copies(e, wait=False)
    for e in range(n_experts):
        copies(e, wait=True)


def dispatch(h, routed, row_start, n_rows, *, total_rows, n_experts):
    t, d = h.shape
    n_tiles = t // TOK_TILE
    return pl.pallas_call(
        functools.partial(_dispatch_kernel, n_experts=n_experts),
        grid_spec=pltpu.PrefetchScalarGridSpec(
            num_scalar_prefetch=2,
            grid=(n_tiles,),
            in_specs=[
                pl.BlockSpec((TOK_TILE, d), lambda i, rs, n: (i, 0)),
                pl.BlockSpec((TOK_TILE, LANES), lambda i, rs, n: (i, 0)),
                pl.BlockSpec(memory_space=pl.ANY),
            ],
            out_specs=pl.BlockSpec(memory_space=pl.ANY),
            scratch_shapes=[pltpu.VMEM((n_experts, TOK_TILE, d), BF16),
                            pltpu.SemaphoreType.DMA((n_experts,))],
        ),
        out_shape=jax.ShapeDtypeStruct((total_rows, d), BF16),
        input_output_aliases={4: 0},
        compiler_params=_params("arbitrary"),
    )(row_start, n_rows, h, routed, jnp.zeros((total_rows, d), BF16))


def _grouped_ffn_kernel(te_ref, na_ref, x_ref, wg_ref, wu_ref, wd_ref, o_ref, acc_ref):
    j = pl.program_id(0)
    f = pl.program_id(1)
    last_f = pl.num_programs(1) - 1
    active = j < na_ref[0]

    @pl.when(active)
    def _():
        @pl.when(f == 0)
        def _():
            acc_ref[...] = jnp.zeros_like(acc_ref)

        x = x_ref[...]
        act = _silu(_dot(x, wg_ref[...])) * _dot(x, wu_ref[...])
        acc_ref[...] += _dot(act.astype(BF16), wd_ref[...])

        @pl.when(f == last_f)
        def _():
            o_ref[...] = acc_ref[...].astype(o_ref.dtype)

    @pl.when(jnp.logical_not(active) & (f == last_f))
    def _():
        o_ref[...] = jnp.zeros_like(o_ref)


def grouped_ffn(xs, w_gu, w_down, tile_expert, n_active, *, f_tile):
    rows, d = xs.shape
    ff = w_down.shape[1]
    nf = ff // f_tile

    def f_idx(j, f, na):
        return jnp.where(j < na[0], f, nf - 1)

    return pl.pallas_call(
        _grouped_ffn_kernel,
        grid_spec=pltpu.PrefetchScalarGridSpec(
            num_scalar_prefetch=2,
            grid=(rows // FFN_ROWS, nf),
            in_specs=[
                pl.BlockSpec((FFN_ROWS, d), lambda j, f, te, na: (jnp.minimum(j, na[0] - 1), 0)),
                pl.BlockSpec((None, d, f_tile), lambda j, f, te, na: (te[j], 0, f_idx(j, f, na))),
                pl.BlockSpec((None, d, f_tile), lambda j, f, te, na: (te[j], 0, nf + f_idx(j, f, na))),
                pl.BlockSpec((None, f_tile, d), lambda j, f, te, na: (te[j], f_idx(j, f, na), 0)),
            ],
            out_specs=pl.BlockSpec((FFN_ROWS, d), lambda j, f, te, na: (j, 0)),
            scratch_shapes=[pltpu.VMEM((FFN_ROWS, d), F32)],
        ),
        out_shape=jax.ShapeDtypeStruct((rows, d), BF16),
        compiler_params=_params("arbitrary", "arbitrary"),
    )(tile_expert, n_active, xs, w_gu, w_gu, w_down)


def _combine_kernel(rs_ref, n_ref, x_ref, gates_ref, routed_ref, y_ref, o_ref, ybuf_ref, sem,
                    *, n_experts):
    i = pl.program_id(0)
    tm = x_ref.shape[0]
    n_sub = tm // SUB_ROWS

    def fetch(e, k):
        base = rs_ref[i * n_experts + e]
        return pltpu.make_async_copy(
            y_ref.at[pl.ds(pl.multiple_of(base + k * SUB_ROWS, ROW_ALIGN), SUB_ROWS)],
            ybuf_ref.at[e, k], sem.at[e, k])

    for e in range(n_experts):
        for k in range(n_sub):
            @pl.when(k * SUB_ROWS < n_ref[i * n_experts + e])
            def _(e=e, k=k):
                fetch(e, k).start()

    routed = routed_ref[...]
    rank = _rank_in_tile(routed)
    gates = gates_ref[...]
    o_ref[...] = x_ref[...]
    for e in range(n_experts):
        for k in range(n_sub):
            @pl.when(k * SUB_ROWS < n_ref[i * n_experts + e])
            def _(e=e, k=k):
                fetch(e, k).wait()
                onehot = _slot_onehot(rank, routed, e, k)
                o_ref[...] += gates[:, e:e + 1] * _dot(onehot, ybuf_ref[e, k])


def combine(x, gates, routed, y, row_start, n_rows, *, n_experts):
    t, d = x.shape
    n_tiles = t // TOK_TILE
    n_sub = TOK_TILE // SUB_ROWS
    return pl.pallas_call(
        functools.partial(_combine_kernel, n_experts=n_experts),
        grid_spec=pltpu.PrefetchScalarGridSpec(
            num_scalar_prefetch=2,
            grid=(n_tiles,),
            in_specs=[
                pl.BlockSpec((TOK_TILE, d), lambda i, rs, n: (i, 0)),
                pl.BlockSpec((TOK_TILE, LANES), lambda i, rs, n: (i, 0)),
                pl.BlockSpec((TOK_TILE, LANES), lambda i, rs, n: (i, 0)),
                pl.BlockSpec(memory_space=pl.ANY),
            ],
            out_specs=pl.BlockSpec((TOK_TILE, d), lambda i, rs, n: (i, 0)),
            scratch_shapes=[pltpu.VMEM((n_experts, n_sub, SUB_ROWS, d), BF16),
                            pltpu.SemaphoreType.DMA((n_experts, n_sub))],
        ),
        out_shape=jax.ShapeDtypeStruct((t, d), F32),
        compiler_params=_params("arbitrary"),
    )(row_start, n_rows, x, gates, routed, y)


def _ffn_kernel(x_ref, g_ref, wg_ref, wu_ref, wd_ref, o_ref, h_ref, acc_ref):
    f = pl.program_id(1)

    @pl.when(f == 0)
    def _():
        x = x_ref[...]
        h_ref[...] = _rms(x, g_ref[...]).astype(BF16)
        acc_ref[...] = x

    h = h_ref[...]
    act = _silu(_dot(h, wg_ref[...])) * _dot(h, wu_ref[...])
    acc_ref[...] += _dot(act.astype(BF16), wd_ref[...])

    @pl.when(f == pl.num_programs(1) - 1)
    def _():
        o_ref[...] = acc_ref[...]


def ffn(x, gain, w_gu, w_down, *, f_tile):
    t, d = x.shape
    ff = w_down.shape[0]
    nf = ff // f_tile
    return pl.pallas_call(
        _ffn_kernel,
        grid=(t // ROW_TILE, nf),
        in_specs=[
            pl.BlockSpec((ROW_TILE, d), lambda i, f: (i, 0)),
            pl.BlockSpec((1, d), lambda i, f: (0, 0)),
            pl.BlockSpec((d, f_tile), lambda i, f: (0, f)),
            pl.BlockSpec((d, f_tile), lambda i, f: (0, nf + f)),
            pl.BlockSpec((f_tile, d), lambda i, f: (f, 0)),
        ],
        out_specs=pl.BlockSpec((ROW_TILE, d), lambda i, f: (i, 0)),
        out_shape=jax.ShapeDtypeStruct((t, d), F32),
        scratch_shapes=[pltpu.VMEM((ROW_TILE, d), BF16), pltpu.VMEM((ROW_TILE, d), F32)],
        compiler_params=_params("parallel", "arbitrary"),
    )(x, gain.reshape(1, d), w_gu, w_gu, w_down)


def _rmsnorm_kernel(x_ref, g_ref, o_ref):
    o_ref[...] = _rms(x_ref[...], g_ref[...])


def rmsnorm(x, gain):
    t, d = x.shape
    return pl.pallas_call(
        _rmsnorm_kernel,
        grid=(t // ROW_TILE,),
        in_specs=[pl.BlockSpec((ROW_TILE, d), lambda i: (i, 0)),
                  pl.BlockSpec((1, d), lambda i: (0, 0))],
        out_specs=pl.BlockSpec((ROW_TILE, d), lambda i: (i, 0)),
        out_shape=jax.ShapeDtypeStruct((t, d), F32),
        compiler_params=_params("parallel"),
    )(x, gain.reshape(1, d))


def _lane_tile(n, target):
    best = LANES
    for cand in range(LANES, target + 1, LANES):
        if n % cand == 0:
            best = cand
    return best


def moe_layer(x, gain, w_router, w_gu, w_down):
    t, d = x.shape
    n_experts = w_router.shape[1]
    n_tiles = t // TOK_TILE
    h, gates, routed, cnt = router(x, gain, w_router)
    cnt = cnt.reshape(n_tiles, LANES)[:, :n_experts]
    n_rows = (cnt + ROW_ALIGN - 1) // ROW_ALIGN * ROW_ALIGN
    seg_rows = jnp.sum(n_rows, axis=0)
    seg_cap = (seg_rows + FFN_ROWS - 1) // FFN_ROWS * FFN_ROWS
    seg_end = jnp.cumsum(seg_cap)
    row_start = (seg_end - seg_cap)[None, :] + jnp.cumsum(n_rows, axis=0) - n_rows
    max_rows = (MOE_TOPK * t + n_tiles * n_experts * (ROW_ALIGN - 1)
                + n_experts * (FFN_ROWS - ROW_ALIGN))
    total_tiles = -(-(max_rows + SUB_ROWS) // FFN_ROWS)
    n_active = (seg_end[-1:] // FFN_ROWS).astype(jnp.int32)
    tile_expert = jnp.minimum(
        jnp.searchsorted(seg_end, jnp.arange(total_tiles, dtype=jnp.int32) * FFN_ROWS, side="right"),
        n_experts - 1).astype(jnp.int32)
    row_start = row_start.reshape(-1).astype(jnp.int32)
    n_rows = n_rows.reshape(-1).astype(jnp.int32)
    xs = dispatch(h, routed, row_start, n_rows,
                  total_rows=total_tiles * FFN_ROWS, n_experts=n_experts)
    y = grouped_ffn(xs, w_gu.astype(BF16), w_down.astype(BF16), tile_expert, n_active,
                    f_tile=_lane_tile(w_down.shape[1], 1024))
    return combine(x, gates, routed, y, row_start, n_rows, n_experts=n_experts)


def kernel(x, attn_norm, ffn_norm, hgrn_w_in, hgrn_lb_logits, hgrn_out_norm, hgrn_w_o, kv_norm, w_kv, moba_w_q, moba_w_o, ffn_w_gu, ffn_w_down, moe_router, moe_w_gu, moe_w_down, final_norm):
    batch, seq, d = x.shape
    depth = attn_norm.shape[0]
    n_a = hgrn_w_in.shape[0]
    x = x.reshape(batch * seq, d)
    kv = k_mean = None
    for layer in range(depth):
        if layer < n_a:
            proj = norm_matmul(x, attn_norm[layer], hgrn_w_in[layer].astype(BF16), F32)
            o = hgrn_recurrence(proj, hgrn_lb_logits, hgrn_out_norm[layer],
                                layer=layer, batch=batch, seq=seq)
            x = matmul_residual(x, o, hgrn_w_o[layer].astype(BF16))
        else:
            if layer == n_a:
                kv, k_mean = shared_kv(x, kv_norm, w_kv.astype(BF16))
            j = layer - n_a
            q = norm_matmul(x, attn_norm[layer], moba_w_q[j].astype(BF16), F32)
            o = moba_attention(q, kv, k_mean, batch=batch, seq=seq)
            x = matmul_residual(x, o, moba_w_o[j].astype(BF16))
        if layer % 2 == 0:
            j = layer // 2
            x = ffn(x, ffn_norm[layer], ffn_w_gu[j].astype(BF16), ffn_w_down[j].astype(BF16),
                    f_tile=_lane_tile(ffn_w_down.shape[1], 1536))
        else:
            j = layer // 2
            x = moe_layer(x, ffn_norm[layer], moe_router[j], moe_w_gu[j], moe_w_down[j])
    return rmsnorm(x, final_norm).reshape(batch, seq, d)
```

```python
import functools

import jax
import jax.numpy as jnp
from jax import lax
from jax.experimental import pallas as pl
from jax.experimental.pallas import tpu as pltpu

F32 = jnp.float32
BF16 = jnp.bfloat16

EPS = 1e-6
HEAD_DIM = 128
HGRN_CHUNK = 64
MOBA_BLOCK = 256
MOBA_TOPK = 3
MOE_TOPK = 2
MOBA_HEADS_PER_STEP = 2
LOG2E = 1.4426950408889634
LANES = 128
VMEM_LIMIT = 56 * 1024 * 1024

HGRN_TILE = 512
ROW_TILE = 512
TOK_TILE = 512
FFN_ROWS = 512
SUB_ROWS = 128
ROW_ALIGN = 16


def _params(*semantics):
    return pltpu.CompilerParams(dimension_semantics=semantics, vmem_limit_bytes=VMEM_LIMIT)


def _rms(x, gain):
    return x * lax.rsqrt(jnp.mean(x * x, axis=-1, keepdims=True) + EPS) * gain


def _dot(a, b):
    return jnp.dot(a, b, preferred_element_type=F32)


def _dot_nt(a, b):
    return lax.dot_general(a, b, (((1,), (1,)), ((), ())), preferred_element_type=F32)


def _dot_tn(a, b):
    return lax.dot_general(a, b, (((0,), (0,)), ((), ())), preferred_element_type=F32)


def _silu(x):
    return x * jax.nn.sigmoid(x)


def _norm_matmul_kernel(x_ref, g_ref, w_ref, o_ref):
    h = _rms(x_ref[...], g_ref[...]).astype(BF16)
    o_ref[...] = _dot(h, w_ref[...]).astype(o_ref.dtype)


def norm_matmul(x, gain, w, out_dtype):
    t, d = x.shape
    n = w.shape[1]
    return pl.pallas_call(
        _norm_matmul_kernel,
        grid=(t // ROW_TILE,),
        in_specs=[
            pl.BlockSpec((ROW_TILE, d), lambda i: (i, 0)),
            pl.BlockSpec((1, d), lambda i: (0, 0)),
            pl.BlockSpec((d, n), lambda i: (0, 0)),
        ],
        out_specs=pl.BlockSpec((ROW_TILE, n), lambda i: (i, 0)),
        out_shape=jax.ShapeDtypeStruct((t, n), out_dtype),
        compiler_params=_params("parallel"),
    )(x, gain.reshape(1, d), w)


def _kv_kernel(x_ref, g_ref, wk_ref, wvt_ref, k_ref, vt_ref, km_ref):
    h = _rms(x_ref[...], g_ref[...]).astype(BF16)
    k = _dot(h, wk_ref[...])
    k_ref[...] = k.astype(k_ref.dtype)
    vt_ref[0] = _dot_nt(wvt_ref[...], h).astype(vt_ref.dtype)
    km_ref[0] = jnp.mean(k, axis=0, keepdims=True)


def shared_kv(x, gain, w_k, w_v_t):
    t, d = x.shape
    nblk = t // MOBA_BLOCK
    return pl.pallas_call(
        _kv_kernel,
        grid=(nblk,),
        in_specs=[
            pl.BlockSpec((MOBA_BLOCK, d), lambda i: (i, 0)),
            pl.BlockSpec((1, d), lambda i: (0, 0)),
            pl.BlockSpec((d, d), lambda i: (0, 0)),
            pl.BlockSpec((d, d), lambda i: (0, 0)),
        ],
        out_specs=[
            pl.BlockSpec((MOBA_BLOCK, d), lambda i: (i, 0)),
            pl.BlockSpec((1, d, MOBA_BLOCK), lambda i: (i, 0, 0)),
            pl.BlockSpec((1, 1, d), lambda i: (i, 0, 0)),
        ],
        out_shape=[
            jax.ShapeDtypeStruct((t, d), BF16),
            jax.ShapeDtypeStruct((nblk, d, MOBA_BLOCK), BF16),
            jax.ShapeDtypeStruct((nblk, 1, d), F32),
        ],
        compiler_params=_params("parallel"),
    )(x, gain.reshape(1, d), w_k, w_v_t)


def _matmul_residual_kernel(x_ref, a_ref, w_ref, o_ref):
    o_ref[...] = x_ref[...] + _dot(a_ref[...], w_ref[...])


def matmul_residual(x, a, w):
    t, d = x.shape
    k = a.shape[1]
    return pl.pallas_call(
        _matmul_residual_kernel,
        grid=(t // ROW_TILE,),
        in_specs=[
            pl.BlockSpec((ROW_TILE, d), lambda i: (i, 0)),
            pl.BlockSpec((ROW_TILE, k), lambda i: (i, 0)),
            pl.BlockSpec((k, d), lambda i: (0, 0)),
        ],
        out_specs=pl.BlockSpec((ROW_TILE, d), lambda i: (i, 0)),
        out_shape=jax.ShapeDtypeStruct((t, d), F32),
        compiler_params=_params("parallel"),
    )(x, a, w)


def _hgrn_kernel(q_ref, f_ref, i_ref, g_ref, lbl_ref, gain_ref, o_ref, st_ref, *, layer):
    c = HGRN_CHUNK
    dk = HEAD_DIM
    n_chunks = q_ref.shape[0] // c
    group = MOBA_BLOCK // c

    def wide(ref):
        return jnp.concatenate([ref[n * c:(n + 1) * c, :] for n in range(n_chunks)], axis=1)

    def chunk(x, n):
        return x[:, n * dk:(n + 1) * dk]

    def rows_of(x, n0, n1):
        return jnp.concatenate([chunk(x, n) for n in range(n0, n1)], axis=0)

    @pl.when(pl.program_id(2) == 0)
    def _():
        st_ref[...] = jnp.zeros_like(st_ref)

    fz = wide(f_ref)
    t = jnp.exp(-jnp.abs(fz))
    log_sig = jnp.minimum(fz, 0.0) - jnp.log(1.0 + t)
    sig_neg = jnp.where(fz >= 0.0, t, 1.0) / (1.0 + t)
    if layer > 0:
        logits = lbl_ref[...]
        e = jnp.exp(logits - jnp.max(logits, axis=0, keepdims=True))
        p = e / jnp.sum(e, axis=0, keepdims=True)
        lb = jnp.sum(p[1:layer + 1], axis=0, keepdims=True)
        lb = jnp.concatenate([lb] * n_chunks, axis=1)
        log_lb = jnp.log(lb)
        y = jnp.log(1.0 - lb) + log_sig
        log_f = jnp.maximum(log_lb, y) + jnp.log(1.0 + jnp.exp(-jnp.abs(log_lb - y)))
        k = (1.0 - lb) * sig_neg
    else:
        log_f = log_sig
        k = sig_neg

    tril = (lax.broadcasted_iota(jnp.int32, (c, c), 1)
            <= lax.broadcasted_iota(jnp.int32, (c, c), 0)).astype(BF16)
    hi = log_f.astype(BF16)
    rest = log_f - hi.astype(F32)
    mid = rest.astype(BF16)
    lo = (rest - mid.astype(F32)).astype(BF16)
    b = _dot(tril, hi) + _dot(tril, mid) + _dot(tril, lo)
    b_mid = b[c // 2 - 1:c // 2, :]
    b_last = b[c - 1:c, :]

    q = _silu(wide(q_ref))
    qd = q * jnp.exp(b - b_mid)
    kd = k * jnp.exp(b_mid - b)
    qe = (qd * jnp.exp(b_mid)).astype(BF16)
    kl = (kd * jnp.exp(b_last - b_mid)).astype(BF16)
    qd = qd.astype(BF16)
    kd = kd.astype(BF16)
    v = wide(i_ref).astype(BF16)
    decay = jnp.exp(b_last)

    gr = group * c
    r_i = lax.broadcasted_iota(jnp.int32, (gr, gr), 0)
    c_i = lax.broadcasted_iota(jnp.int32, (gr, gr), 1)
    keep = (c_i <= r_i) & ((r_i // c) == (c_i // c))
    intra = []
    for n0 in range(0, n_chunks, group):
        a = jnp.where(keep, _dot_nt(rows_of(qd, n0, n0 + group), rows_of(kd, n0, n0 + group)), 0.0)
        intra.append(_dot(a.astype(BF16), rows_of(v, n0, n0 + group)))
    o = jnp.concatenate(intra, axis=0)

    st = st_ref[...]
    inter = []
    for n in range(n_chunks):
        inter.append(_dot_nt(chunk(qe, n), st.astype(BF16)))
        st = chunk(decay, n) * st + _dot_tn(chunk(v, n), chunk(kl, n))
    st_ref[...] = st
    o = o + jnp.concatenate(inter, axis=0)

    o = o * lax.rsqrt(jnp.mean(o * o, axis=-1, keepdims=True) + EPS)
    o = o * gain_ref[...] * _silu(g_ref[...])
    o_ref[...] = o.astype(o_ref.dtype)


def hgrn_recurrence(proj, lb_logits, out_gain, *, layer, batch, seq):
    t = proj.shape[0]
    d = proj.shape[1] // 4
    heads = d // HEAD_DIM
    tiles = seq // HGRN_TILE
    n_layers = lb_logits.shape[0]

    def act(part):
        return pl.BlockSpec((HGRN_TILE, HEAD_DIM),
                            lambda b, h, s: (b * tiles + s, part * heads + h))

    return pl.pallas_call(
        functools.partial(_hgrn_kernel, layer=layer),
        grid=(batch, heads, tiles),
        in_specs=[
            act(0), act(1), act(2), act(3),
            pl.BlockSpec((n_layers, HEAD_DIM), lambda b, h, s: (0, h)),
            pl.BlockSpec((1, HEAD_DIM), lambda b, h, s: (0, h)),
        ],
        out_specs=pl.BlockSpec((HGRN_TILE, HEAD_DIM), lambda b, h, s: (b * tiles + s, h)),
        out_shape=jax.ShapeDtypeStruct((t, d), BF16),
        scratch_shapes=[pltpu.VMEM((HEAD_DIM, HEAD_DIM), F32)],
        compiler_params=_params("parallel", "parallel", "arbitrary"),
    )(proj, proj, proj, proj, lb_logits, out_gain.reshape(1, d))


def _moba_kernel(q_ref, k_ref, vt_ref, km_ref, slope_ref, o_ref):
    bs = MOBA_BLOCK
    dh = HEAD_DIM
    qb = pl.program_id(2)
    n_blk = km_ref.shape[1]
    heads = q_ref.shape[1] // dh
    scale2 = HEAD_DIM ** -0.5 * LOG2E
    rel = (lax.broadcasted_iota(jnp.int32, (bs, bs), 1)
           - lax.broadcasted_iota(jnp.int32, (bs, bs), 0))
    rel_f = rel.astype(F32)

    def gate_rank(q, km, n):
        gate = lax.dot_general(km, q, (((1,), (1,)), ((), ())),
                               precision=lax.Precision.HIGHEST, preferred_element_type=F32)
        blk = lax.broadcasted_iota(jnp.int32, gate.shape, 0)
        rank = jnp.zeros(gate.shape, F32)
        for m in range(n):
            g_m = gate[m:m + 1, :]
            rank = rank + ((g_m > gate) | ((g_m == gate) & (m < blk))).astype(F32)
        return rank

    def attend(n, hh):
        cols = slice(hh * dh, (hh + 1) * dh)
        q = q_ref[:, cols]
        q16 = q.astype(BF16)
        slope2 = slope_ref[hh][0:1, 0:1] * LOG2E
        bias = -slope2 * rel_f
        rank = gate_rank(q, km_ref[0][:, cols], n) if n > MOBA_TOPK else None
        s_all = _dot_nt(k_ref[0:(n + 1) * bs, cols], q16) * scale2
        us, shifts, maxes = [], [], []
        for kb in range(n + 1):
            u = s_all[kb * bs:(kb + 1) * bs] + bias
            if kb == n:
                u = jnp.where(rel >= 0, u, -jnp.inf)
            elif rank is not None:
                u = jnp.where(rank[kb:kb + 1, :] < MOBA_TOPK, u, -jnp.inf)
            shift = slope2 * float((n - kb) * bs)
            us.append(u)
            shifts.append(shift)
            maxes.append(jnp.max(u, axis=0, keepdims=True) - shift)
        m = functools.reduce(jnp.maximum, maxes)
        ps = [jnp.exp2(u - (m + shift)) for u, shift in zip(us, shifts)]
        l = functools.reduce(jnp.add, [jnp.sum(p, axis=0, keepdims=True) for p in ps])
        p_all = jnp.concatenate(ps, axis=0).astype(BF16)
        vt = jnp.concatenate([vt_ref[kb, cols, :] for kb in range(n + 1)], axis=1)
        o_ref[:, cols] = (_dot(vt, p_all) / l).T.astype(o_ref.dtype)

    for n in range(n_blk):
        @pl.when(qb == n)
        def _(n=n):
            for hh in range(heads):
                attend(n, hh)


def moba_attention(q, k, v_t, k_mean, *, batch, seq):
    t, d = q.shape
    heads = d // HEAD_DIM
    n_blk = seq // MOBA_BLOCK
    hp = MOBA_HEADS_PER_STEP
    width = hp * HEAD_DIM
    slopes = (2.0 ** (-8.0 * jnp.arange(1, heads + 1, dtype=F32) / heads))
    slopes = jnp.broadcast_to(slopes[:, None, None], (heads, 1, LANES))
    return pl.pallas_call(
        _moba_kernel,
        grid=(batch, heads // hp, n_blk),
        in_specs=[
            pl.BlockSpec((MOBA_BLOCK, width), lambda b, h, i: (b * n_blk + i, h)),
            pl.BlockSpec((seq, width), lambda b, h, i: (b, h)),
            pl.BlockSpec((n_blk, width, MOBA_BLOCK), lambda b, h, i: (b, h, 0)),
            pl.BlockSpec((1, n_blk, width), lambda b, h, i: (b, 0, h)),
            pl.BlockSpec((hp, 1, LANES), lambda b, h, i: (h, 0, 0)),
        ],
        out_specs=pl.BlockSpec((MOBA_BLOCK, width), lambda b, h, i: (b * n_blk + i, h)),
        out_shape=jax.ShapeDtypeStruct((t, d), BF16),
        compiler_params=_params("parallel", "parallel", "arbitrary"),
    )(q, k, v_t, k_mean.reshape(batch, n_blk, d), slopes)


def _router_kernel(x_ref, g_ref, w_ref, h_ref, gates_ref, routed_ref, cnt_ref, *, n_experts):
    h = _rms(x_ref[...], g_ref[...])
    h_ref[...] = h.astype(BF16)
    logits = jnp.dot(h, w_ref[...], precision=lax.Precision.HIGHEST, preferred_element_type=F32)
    lane = lax.broadcasted_iota(jnp.int32, logits.shape, 1)
    neg = -jnp.inf
    lg = jnp.where(lane < n_experts, logits, neg)
    m1 = jnp.max(lg, axis=1, keepdims=True)
    i1 = jnp.min(jnp.where(lg == m1, lane, LANES), axis=1, keepdims=True)
    lg2 = jnp.where(lane == i1, neg, lg)
    m2 = jnp.max(lg2, axis=1, keepdims=True)
    i2 = jnp.min(jnp.where(lg2 == m2, lane, LANES), axis=1, keepdims=True)
    e2 = jnp.exp(m2 - m1)
    den = 1.0 + e2
    gates_ref[...] = jnp.where(lane == i1, 1.0 / den, 0.0) + jnp.where(lane == i2, e2 / den, 0.0)
    routed = ((lane == i1) | (lane == i2)).astype(F32)
    routed_ref[...] = routed.astype(BF16)
    cnt_ref[0] = jnp.sum(routed, axis=0, keepdims=True).astype(jnp.int32)


def router(x, gain, w_router):
    t, d = x.shape
    n_experts = w_router.shape[1]
    n_tiles = t // TOK_TILE
    w = jnp.pad(w_router, ((0, 0), (0, LANES - n_experts)))
    return pl.pallas_call(
        functools.partial(_router_kernel, n_experts=n_experts),
        grid=(n_tiles,),
        in_specs=[
            pl.BlockSpec((TOK_TILE, d), lambda i: (i, 0)),
            pl.BlockSpec((1, d), lambda i: (0, 0)),
            pl.BlockSpec((d, LANES), lambda i: (0, 0)),
        ],
        out_specs=[
            pl.BlockSpec((TOK_TILE, d), lambda i: (i, 0)),
            pl.BlockSpec((TOK_TILE, LANES), lambda i: (i, 0)),
            pl.BlockSpec((TOK_TILE, LANES), lambda i: (i, 0)),
            pl.BlockSpec((1, 1, LANES), lambda i: (i, 0, 0)),
        ],
        out_shape=[
            jax.ShapeDtypeStruct((t, d), BF16),
            jax.ShapeDtypeStruct((t, LANES), F32),
            jax.ShapeDtypeStruct((t, LANES), BF16),
            jax.ShapeDtypeStruct((n_tiles, 1, LANES), jnp.int32),
        ],
        compiler_params=_params("parallel"),
    )(x, gain.reshape(1, d), w)


def _rank_in_tile(routed):
    tm = routed.shape[0]
    earlier = (lax.broadcasted_iota(jnp.int32, (tm, tm), 1)
               < lax.broadcasted_iota(jnp.int32, (tm, tm), 0)).astype(BF16)
    return _dot(earlier, routed)


def _slot_onehot(rank, routed, e, k):
    tm = rank.shape[0]
    slot = (lax.broadcasted_iota(jnp.int32, (tm, SUB_ROWS), 1) + k * SUB_ROWS).astype(F32)
    hit = (slot == rank[:, e:e + 1]) & (routed[:, e:e + 1] > 0)
    return hit.astype(BF16)


def _dispatch_kernel(rs_ref, n_ref, h_ref, routed_ref, _, xs_ref, slab_ref, sem, *, n_experts):
    i = pl.program_id(0)
    tm = h_ref.shape[0]
    routed = routed_ref[...]
    rank = _rank_in_tile(routed)
    h = h_ref[...]

    def copies(e, wait):
        n = n_ref[i * n_experts + e]
        base = rs_ref[i * n_experts + e]
        off = jnp.int32(0)
        size = tm
        while size >= ROW_ALIGN:
            @pl.when((n & size) != 0)
            def _(off=off, size=size):
                cp = pltpu.make_async_copy(
                    slab_ref.at[e, pl.ds(pl.multiple_of(off, ROW_ALIGN), size)],
                    xs_ref.at[pl.ds(pl.multiple_of(base + off, ROW_ALIGN), size)],
                    sem.at[e])
                if wait:
                    cp.wait()
                else:
                    cp.start()
            off = off + (n & size)
            size //= 2

    for e in range(n_experts):
        n = n_ref[i * n_experts + e]
        for k in range(tm // SUB_ROWS):
            @pl.when(k * SUB_ROWS < n)
            def _(e=e, k=k):
                onehot = _slot_onehot(rank, routed, e, k)
                slab_ref[e, k * SUB_ROWS:(k + 1) * SUB_ROWS, :] = _dot_tn(onehot, h).astype(BF16)
        copies(e, wait=False)
    for e in range(n_experts):
        copies(e, wait=True)


def dispatch(h, routed, row_start, n_rows, *, total_rows, n_experts):
    t, d = h.shape
    n_tiles = t // TOK_TILE
    return pl.pallas_call(
        functools.partial(_dispatch_kernel, n_experts=n_experts),
        grid_spec=pltpu.PrefetchScalarGridSpec(
            num_scalar_prefetch=2,
            grid=(n_tiles,),
            in_specs=[
                pl.BlockSpec((TOK_TILE, d), lambda i, rs, n: (i, 0)),
                pl.BlockSpec((TOK_TILE, LANES), lambda i, rs, n: (i, 0)),
                pl.BlockSpec(memory_space=pl.ANY),
            ],
            out_specs=pl.BlockSpec(memory_space=pl.ANY),
            scratch_shapes=[pltpu.VMEM((n_experts, TOK_TILE, d), BF16),
                            pltpu.SemaphoreType.DMA((n_experts,))],
        ),
        out_shape=jax.ShapeDtypeStruct((total_rows, d), BF16),
        input_output_aliases={4: 0},
        compiler_params=_params("arbitrary"),
    )(row_start, n_rows, h, routed, jnp.zeros((total_rows, d), BF16))


def _grouped_ffn_kernel(te_ref, na_ref, x_ref, wg_ref, wu_ref, wd_ref, o_ref, acc_ref):
    j = pl.program_id(0)
    f = pl.program_id(1)
    last_f = pl.num_programs(1) - 1
    active = j < na_ref[0]

    @pl.when(active)
    def _():
        @pl.when(f == 0)
        def _():
            acc_ref[...] = jnp.zeros_like(acc_ref)

        x = x_ref[...]
        act = _silu(_dot(x, wg_ref[...])) * _dot(x, wu_ref[...])
        acc_ref[...] += _dot(act.astype(BF16), wd_ref[...])

        @pl.when(f == last_f)
        def _():
            o_ref[...] = acc_ref[...].astype(o_ref.dtype)

    @pl.when(jnp.logical_not(active) & (f == last_f))
    def _():
        o_ref[...] = jnp.zeros_like(o_ref)


def grouped_ffn(xs, w_gu, w_down, tile_expert, n_active, *, f_tile):
    rows, d = xs.shape
    ff = w_down.shape[1]
    nf = ff // f_tile

    def f_idx(j, f, na):
        return jnp.where(j < na[0], f, nf - 1)

    return pl.pallas_call(
        _grouped_ffn_kernel,
        grid_spec=pltpu.PrefetchScalarGridSpec(
            num_scalar_prefetch=2,
            grid=(rows // FFN_ROWS, nf),
            in_specs=[
                pl.BlockSpec((FFN_ROWS, d), lambda j, f, te, na: (jnp.minimum(j, na[0] - 1), 0)),
                pl.BlockSpec((None, d, f_tile), lambda j, f, te, na: (te[j], 0, f_idx(j, f, na))),
                pl.BlockSpec((None, d, f_tile), lambda j, f, te, na: (te[j], 0, nf + f_idx(j, f, na))),
                pl.BlockSpec((None, f_tile, d), lambda j, f, te, na: (te[j], f_idx(j, f, na), 0)),
            ],
            out_specs=pl.BlockSpec((FFN_ROWS, d), lambda j, f, te, na: (j, 0)),
            scratch_shapes=[pltpu.VMEM((FFN_ROWS, d), F32)],
        ),
        out_shape=jax.ShapeDtypeStruct((rows, d), BF16),
        compiler_params=_params("arbitrary", "arbitrary"),
    )(tile_expert, n_active, xs, w_gu, w_gu, w_down)


def _combine_kernel(rs_ref, n_ref, x_ref, gates_ref, routed_ref, y_ref, o_ref, ybuf_ref, sem,
                    *, n_experts):
    i = pl.program_id(0)
    tm = x_ref.shape[0]
    n_sub = tm // SUB_ROWS

    def fetch(e, k):
        base = rs_ref[i * n_experts + e]
        return pltpu.make_async_copy(
            y_ref.at[pl.ds(pl.multiple_of(base + k * SUB_ROWS, ROW_ALIGN), SUB_ROWS)],
            ybuf_ref.at[e, k], sem.at[e, k])

    for e in range(n_experts):
        for k in range(n_sub):
            @pl.when(k * SUB_ROWS < n_ref[i * n_experts + e])
            def _(e=e, k=k):
                fetch(e, k).start()

    routed = routed_ref[...]
    rank = _rank_in_tile(routed)
    gates = gates_ref[...]
    o_ref[...] = x_ref[...]
    for e in range(n_experts):
        for k in range(n_sub):
            @pl.when(k * SUB_ROWS < n_ref[i * n_experts + e])
            def _(e=e, k=k):
                fetch(e, k).wait()
                onehot = _slot_onehot(rank, routed, e, k)
                o_ref[...] += gates[:, e:e + 1] * _dot(onehot, ybuf_ref[e, k])


def combine(x, gates, routed, y, row_start, n_rows, *, n_experts):
    t, d = x.shape
    n_tiles = t // TOK_TILE
    n_sub = TOK_TILE // SUB_ROWS
    return pl.pallas_call(
        functools.partial(_combine_kernel, n_experts=n_experts),
        grid_spec=pltpu.PrefetchScalarGridSpec(
            num_scalar_prefetch=2,
            grid=(n_tiles,),
            in_specs=[
                pl.BlockSpec((TOK_TILE, d), lambda i, rs, n: (i, 0)),
                pl.BlockSpec((TOK_TILE, LANES), lambda i, rs, n: (i, 0)),
                pl.BlockSpec((TOK_TILE, LANES), lambda i, rs, n: (i, 0)),
                pl.BlockSpec(memory_space=pl.ANY),
            ],
            out_specs=pl.BlockSpec((TOK_TILE, d), lambda i, rs, n: (i, 0)),
            scratch_shapes=[pltpu.VMEM((n_experts, n_sub, SUB_ROWS, d), BF16),
                            pltpu.SemaphoreType.DMA((n_experts, n_sub))],
        ),
        out_shape=jax.ShapeDtypeStruct((t, d), F32),
        compiler_params=_params("arbitrary"),
    )(row_start, n_rows, x, gates, routed, y)


def _ffn_kernel(x_ref, g_ref, wg_ref, wu_ref, wd_ref, o_ref, h_ref, acc_ref):
    f = pl.program_id(1)

    @pl.when(f == 0)
    def _():
        x = x_ref[...]
        h_ref[...] = _rms(x, g_ref[...]).astype(BF16)
        acc_ref[...] = x

    h = h_ref[...]
    act = _silu(_dot(h, wg_ref[...])) * _dot(h, wu_ref[...])
    acc_ref[...] += _dot(act.astype(BF16), wd_ref[...])

    @pl.when(f == pl.num_programs(1) - 1)
    def _():
        o_ref[...] = acc_ref[...]


def ffn(x, gain, w_gu, w_down, *, f_tile):
    t, d = x.shape
    ff = w_down.shape[0]
    nf = ff // f_tile
    return pl.pallas_call(
        _ffn_kernel,
        grid=(t // ROW_TILE, nf),
        in_specs=[
            pl.BlockSpec((ROW_TILE, d), lambda i, f: (i, 0)),
            pl.BlockSpec((1, d), lambda i, f: (0, 0)),
            pl.BlockSpec((d, f_tile), lambda i, f: (0, f)),
            pl.BlockSpec((d, f_tile), lambda i, f: (0, nf + f)),
            pl.BlockSpec((f_tile, d), lambda i, f: (f, 0)),
        ],
        out_specs=pl.BlockSpec((ROW_TILE, d), lambda i, f: (i, 0)),
        out_shape=jax.ShapeDtypeStruct((t, d), F32),
        scratch_shapes=[pltpu.VMEM((ROW_TILE, d), BF16), pltpu.VMEM((ROW_TILE, d), F32)],
        compiler_params=_params("parallel", "arbitrary"),
    )(x, gain.reshape(1, d), w_gu, w_gu, w_down)


def _rmsnorm_kernel(x_ref, g_ref, o_ref):
    o_ref[...] = _rms(x_ref[...], g_ref[...])


def rmsnorm(x, gain):
    t, d = x.shape
    return pl.pallas_call(
        _rmsnorm_kernel,
        grid=(t // ROW_TILE,),
        in_specs=[pl.BlockSpec((ROW_TILE, d), lambda i: (i, 0)),
                  pl.BlockSpec((1, d), lambda i: (0, 0))],
        out_specs=pl.BlockSpec((ROW_TILE, d), lambda i: (i, 0)),
        out_shape=jax.ShapeDtypeStruct((t, d), F32),
        compiler_params=_params("parallel"),
    )(x, gain.reshape(1, d))


def _lane_tile(n, target):
    best = LANES
    for cand in range(LANES, target + 1, LANES):
        if n % cand == 0:
            best = cand
    return best


def moe_layer(x, gain, w_router, w_gu, w_down):
    t, d = x.shape
    n_experts = w_router.shape[1]
    n_tiles = t // TOK_TILE
    h, gates, routed, cnt = router(x, gain, w_router)
    cnt = cnt.reshape(n_tiles, LANES)[:, :n_experts]
    n_rows = (cnt + ROW_ALIGN - 1) // ROW_ALIGN * ROW_ALIGN
    seg_rows = jnp.sum(n_rows, axis=0)
    seg_cap = (seg_rows + FFN_ROWS - 1) // FFN_ROWS * FFN_ROWS
    seg_end = jnp.cumsum(seg_cap)
    row_start = (seg_end - seg_cap)[None, :] + jnp.cumsum(n_rows, axis=0) - n_rows
    max_rows = (MOE_TOPK * t + n_tiles * n_experts * (ROW_ALIGN - 1)
                + n_experts * (FFN_ROWS - ROW_ALIGN))
    total_tiles = -(-(max_rows + SUB_ROWS) // FFN_ROWS)
    n_active = (seg_end[-1:] // FFN_ROWS).astype(jnp.int32)
    tile_expert = jnp.minimum(
        jnp.searchsorted(seg_end, jnp.arange(total_tiles, dtype=jnp.int32) * FFN_ROWS, side="right"),
        n_experts - 1).astype(jnp.int32)
    row_start = row_start.reshape(-1).astype(jnp.int32)
    n_rows = n_rows.reshape(-1).astype(jnp.int32)
    xs = dispatch(h, routed, row_start, n_rows,
                  total_rows=total_tiles * FFN_ROWS, n_experts=n_experts)
    y = grouped_ffn(xs, w_gu.astype(BF16), w_down.astype(BF16), tile_expert, n_active,
                    f_tile=_lane_tile(w_down.shape[1], 1024))
    return combine(x, gates, routed, y, row_start, n_rows, n_experts=n_experts)


def kernel(x, attn_norm, ffn_norm, hgrn_w_in, hgrn_lb_logits, hgrn_out_norm, hgrn_w_o, kv_norm, w_kv, moba_w_q, moba_w_o, ffn_w_gu, ffn_w_down, moe_router, moe_w_gu, moe_w_down, final_norm):
    batch, seq, d = x.shape
    depth = attn_norm.shape[0]
    n_a = hgrn_w_in.shape[0]
    x = x.reshape(batch * seq, d)
    k = v_t = k_mean = None
    for layer in range(depth):
        if layer < n_a:
            proj = norm_matmul(x, attn_norm[layer], hgrn_w_in[layer].astype(BF16), F32)
            o = hgrn_recurrence(proj, hgrn_lb_logits, hgrn_out_norm[layer],
                                layer=layer, batch=batch, seq=seq)
            x = matmul_residual(x, o, hgrn_w_o[layer].astype(BF16))
        else:
            if layer == n_a:
                k, v_t, k_mean = shared_kv(x, kv_norm, w_kv[:, :d].astype(BF16),
                                           w_kv[:, d:].T.astype(BF16))
            j = layer - n_a
            q = norm_matmul(x, attn_norm[layer], moba_w_q[j].astype(BF16), F32)
            o = moba_attention(q, k, v_t, k_mean, batch=batch, seq=seq)
            x = matmul_residual(x, o, moba_w_o[j].astype(BF16))
        if layer % 2 == 0:
            j = layer // 2
            x = ffn(x, ffn_norm[layer], ffn_w_gu[j].astype(BF16), ffn_w_down[j].astype(BF16),
                    f_tile=_lane_tile(ffn_w_down.shape[1], 1536))
        else:
            j = layer // 2
            x = moe_layer(x, ffn_norm[layer], moe_router[j], moe_w_gu[j], moe_w_down[j])
    return rmsnorm(x, final_norm).reshape(batch, seq, d)
```

```python
import functools

import jax
import jax.numpy as jnp
from jax import lax
from jax.experimental import pallas as pl
from jax.experimental.pallas import tpu as pltpu

F32 = jnp.float32
BF16 = jnp.bfloat16

EPS = 1e-6
HEAD_DIM = 128
HGRN_CHUNK = 64
MOBA_BLOCK = 256
MOBA_TOPK = 3
MOE_TOPK = 2
MOBA_HEADS_PER_STEP = 2
LOG2E = 1.4426950408889634
LANES = 128
VMEM_LIMIT = 56 * 1024 * 1024

HGRN_TILE = 512
ROW_TILE = 512
TOK_TILE = 512
FFN_ROWS = 512
ROW_ALIGN = 16


def _params(*semantics):
    return pltpu.CompilerParams(dimension_semantics=semantics, vmem_limit_bytes=VMEM_LIMIT)


def _rms(x, gain):
    return x * lax.rsqrt(jnp.mean(x * x, axis=-1, keepdims=True) + EPS) * gain


def _dot(a, b):
    return jnp.dot(a, b, preferred_element_type=F32)


def _dot_nt(a, b):
    return lax.dot_general(a, b, (((1,), (1,)), ((), ())), preferred_element_type=F32)


def _dot_tn(a, b):
    return lax.dot_general(a, b, (((0,), (0,)), ((), ())), preferred_element_type=F32)


def _silu(x):
    return x * jax.nn.sigmoid(x)


def _norm_matmul_kernel(x_ref, g_ref, w_ref, o_ref):
    h = _rms(x_ref[...], g_ref[...]).astype(BF16)
    o_ref[...] = _dot(h, w_ref[...]).astype(o_ref.dtype)


def norm_matmul(x, gain, w, out_dtype):
    t, d = x.shape
    n = w.shape[1]
    return pl.pallas_call(
        _norm_matmul_kernel,
        grid=(t // ROW_TILE,),
        in_specs=[
            pl.BlockSpec((ROW_TILE, d), lambda i: (i, 0)),
            pl.BlockSpec((1, d), lambda i: (0, 0)),
            pl.BlockSpec((d, n), lambda i: (0, 0)),
        ],
        out_specs=pl.BlockSpec((ROW_TILE, n), lambda i: (i, 0)),
        out_shape=jax.ShapeDtypeStruct((t, n), out_dtype),
        compiler_params=_params("parallel"),
    )(x, gain.reshape(1, d), w)


def _kv_kernel(x_ref, g_ref, wk_ref, wvt_ref, k_ref, vt_ref, km_ref):
    h = _rms(x_ref[...], g_ref[...]).astype(BF16)
    k = _dot(h, wk_ref[...])
    k_ref[...] = k.astype(k_ref.dtype)
    vt_ref[0] = _dot_nt(wvt_ref[...], h).astype(vt_ref.dtype)
    km_ref[0] = jnp.mean(k, axis=0, keepdims=True)


def shared_kv(x, gain, w_k, w_v_t):
    t, d = x.shape
    nblk = t // MOBA_BLOCK
    return pl.pallas_call(
        _kv_kernel,
        grid=(nblk,),
        in_specs=[
            pl.BlockSpec((MOBA_BLOCK, d), lambda i: (i, 0)),
            pl.BlockSpec((1, d), lambda i: (0, 0)),
            pl.BlockSpec((d, d), lambda i: (0, 0)),
            pl.BlockSpec((d, d), lambda i: (0, 0)),
        ],
        out_specs=[
            pl.BlockSpec((MOBA_BLOCK, d), lambda i: (i, 0)),
            pl.BlockSpec((1, d, MOBA_BLOCK), lambda i: (i, 0, 0)),
            pl.BlockSpec((1, 1, d), lambda i: (i, 0, 0)),
        ],
        out_shape=[
            jax.ShapeDtypeStruct((t, d), BF16),
            jax.ShapeDtypeStruct((nblk, d, MOBA_BLOCK), BF16),
            jax.ShapeDtypeStruct((nblk, 1, d), F32),
        ],
        compiler_params=_params("parallel"),
    )(x, gain.reshape(1, d), w_k, w_v_t)


def _matmul_residual_kernel(x_ref, a_ref, w_ref, o_ref):
    o_ref[...] = x_ref[...] + _dot(a_ref[...], w_ref[...])


def matmul_residual(x, a, w):
    t, d = x.shape
    k = a.shape[1]
    return pl.pallas_call(
        _matmul_residual_kernel,
        grid=(t // ROW_TILE,),
        in_specs=[
            pl.BlockSpec((ROW_TILE, d), lambda i: (i, 0)),
            pl.BlockSpec((ROW_TILE, k), lambda i: (i, 0)),
            pl.BlockSpec((k, d), lambda i: (0, 0)),
        ],
        out_specs=pl.BlockSpec((ROW_TILE, d), lambda i: (i, 0)),
        out_shape=jax.ShapeDtypeStruct((t, d), F32),
        compiler_params=_params("parallel"),
    )(x, a, w)


def _hgrn_kernel(q_ref, f_ref, i_ref, g_ref, lbl_ref, gain_ref, o_ref, st_ref, *, layer):
    c = HGRN_CHUNK
    dk = HEAD_DIM
    n_chunks = q_ref.shape[0] // c
    group = MOBA_BLOCK // c

    def wide(ref):
        return jnp.concatenate([ref[n * c:(n + 1) * c, :] for n in range(n_chunks)], axis=1)

    def chunk(x, n):
        return x[:, n * dk:(n + 1) * dk]

    def rows_of(x, n0, n1):
        return jnp.concatenate([chunk(x, n) for n in range(n0, n1)], axis=0)

    @pl.when(pl.program_id(2) == 0)
    def _():
        st_ref[...] = jnp.zeros_like(st_ref)

    fz = wide(f_ref)
    t = jnp.exp(-jnp.abs(fz))
    log_sig = jnp.minimum(fz, 0.0) - jnp.log(1.0 + t)
    sig_neg = jnp.where(fz >= 0.0, t, 1.0) / (1.0 + t)
    if layer > 0:
        logits = lbl_ref[...]
        e = jnp.exp(logits - jnp.max(logits, axis=0, keepdims=True))
        p = e / jnp.sum(e, axis=0, keepdims=True)
        lb = jnp.sum(p[1:layer + 1], axis=0, keepdims=True)
        lb = jnp.concatenate([lb] * n_chunks, axis=1)
        log_lb = jnp.log(lb)
        y = jnp.log(1.0 - lb) + log_sig
        log_f = jnp.maximum(log_lb, y) + jnp.log(1.0 + jnp.exp(-jnp.abs(log_lb - y)))
        k = (1.0 - lb) * sig_neg
    else:
        log_f = log_sig
        k = sig_neg

    tril = (lax.broadcasted_iota(jnp.int32, (c, c), 1)
            <= lax.broadcasted_iota(jnp.int32, (c, c), 0)).astype(BF16)
    hi = log_f.astype(BF16)
    rest = log_f - hi.astype(F32)
    mid = rest.astype(BF16)
    lo = (rest - mid.astype(F32)).astype(BF16)
    b = _dot(tril, hi) + _dot(tril, mid) + _dot(tril, lo)
    b_mid = b[c // 2 - 1:c // 2, :]
    b_last = b[c - 1:c, :]

    q = _silu(wide(q_ref))
    qd = q * jnp.exp(b - b_mid)
    kd = k * jnp.exp(b_mid - b)
    qe = (qd * jnp.exp(b_mid)).astype(BF16)
    kl = (kd * jnp.exp(b_last - b_mid)).astype(BF16)
    qd = qd.astype(BF16)
    kd = kd.astype(BF16)
    v = wide(i_ref).astype(BF16)
    decay = jnp.exp(b_last)

    gr = group * c
    r_i = lax.broadcasted_iota(jnp.int32, (gr, gr), 0)
    c_i = lax.broadcasted_iota(jnp.int32, (gr, gr), 1)
    keep = (c_i <= r_i) & ((r_i // c) == (c_i // c))
    intra = []
    for n0 in range(0, n_chunks, group):
        a = jnp.where(keep, _dot_nt(rows_of(qd, n0, n0 + group), rows_of(kd, n0, n0 + group)), 0.0)
        intra.append(_dot(a.astype(BF16), rows_of(v, n0, n0 + group)))
    o = jnp.concatenate(intra, axis=0)

    st = st_ref[...]
    inter = []
    for n in range(n_chunks):
        inter.append(_dot_nt(chunk(qe, n), st.astype(BF16)))
        st = chunk(decay, n) * st + _dot_tn(chunk(v, n), chunk(kl, n))
    st_ref[...] = st
    o = o + jnp.concatenate(inter, axis=0)

    o = o * lax.rsqrt(jnp.mean(o * o, axis=-1, keepdims=True) + EPS)
    o = o * gain_ref[...] * _silu(g_ref[...])
    o_ref[...] = o.astype(o_ref.dtype)


def hgrn_recurrence(proj, lb_logits, out_gain, *, layer, batch, seq):
    t = proj.shape[0]
    d = proj.shape[1] // 4
    heads = d // HEAD_DIM
    tiles = seq // HGRN_TILE
    n_layers = lb_logits.shape[0]

    def act(part):
        return pl.BlockSpec((HGRN_TILE, HEAD_DIM),
                            lambda b, h, s: (b * tiles + s, part * heads + h))

    return pl.pallas_call(
        functools.partial(_hgrn_kernel, layer=layer),
        grid=(batch, heads, tiles),
        in_specs=[
            act(0), act(1), act(2), act(3),
            pl.BlockSpec((n_layers, HEAD_DIM), lambda b, h, s: (0, h)),
            pl.BlockSpec((1, HEAD_DIM), lambda b, h, s: (0, h)),
        ],
        out_specs=pl.BlockSpec((HGRN_TILE, HEAD_DIM), lambda b, h, s: (b * tiles + s, h)),
        out_shape=jax.ShapeDtypeStruct((t, d), BF16),
        scratch_shapes=[pltpu.VMEM((HEAD_DIM, HEAD_DIM), F32)],
        compiler_params=_params("parallel", "parallel", "arbitrary"),
    )(proj, proj, proj, proj, lb_logits, out_gain.reshape(1, d))


def _moba_kernel(q_ref, k_ref, vt_ref, km_ref, slope_ref, o_ref):
    bs = MOBA_BLOCK
    dh = HEAD_DIM
    qb = pl.program_id(2)
    n_blk = km_ref.shape[1]
    heads = q_ref.shape[1] // dh
    scale2 = HEAD_DIM ** -0.5 * LOG2E
    rel = (lax.broadcasted_iota(jnp.int32, (bs, bs), 1)
           - lax.broadcasted_iota(jnp.int32, (bs, bs), 0))
    rel_f = rel.astype(F32)

    def gate_rank(q, km, n):
        gate = lax.dot_general(km, q, (((1,), (1,)), ((), ())),
                               precision=lax.Precision.HIGHEST, preferred_element_type=F32)
        blk = lax.broadcasted_iota(jnp.int32, gate.shape, 0)
        rank = jnp.zeros(gate.shape, F32)
        for m in range(n):
            g_m = gate[m:m + 1, :]
            rank = rank + ((g_m > gate) | ((g_m == gate) & (m < blk))).astype(F32)
        return rank

    def attend(n, hh):
        cols = slice(hh * dh, (hh + 1) * dh)
        q = q_ref[:, cols]
        q16 = q.astype(BF16)
        slope2 = slope_ref[hh][0:1, 0:1] * LOG2E
        bias = -slope2 * rel_f
        rank = gate_rank(q, km_ref[0][:, cols], n) if n > MOBA_TOPK else None
        s_all = _dot_nt(k_ref[0:(n + 1) * bs, cols], q16) * scale2
        us, shifts, maxes = [], [], []
        for kb in range(n + 1):
            u = s_all[kb * bs:(kb + 1) * bs] + bias
            if kb == n:
                u = jnp.where(rel >= 0, u, -jnp.inf)
            elif rank is not None:
                u = jnp.where(rank[kb:kb + 1, :] < MOBA_TOPK, u, -jnp.inf)
            shift = slope2 * float((n - kb) * bs)
            us.append(u)
            shifts.append(shift)
            maxes.append(jnp.max(u, axis=0, keepdims=True) - shift)
        m = functools.reduce(jnp.maximum, maxes)
        ps = [jnp.exp2(u - (m + shift)) for u, shift in zip(us, shifts)]
        l = functools.reduce(jnp.add, [jnp.sum(p, axis=0, keepdims=True) for p in ps])
        p_all = jnp.concatenate(ps, axis=0).astype(BF16)
        vt = jnp.concatenate([vt_ref[kb, cols, :] for kb in range(n + 1)], axis=1)
        o_ref[:, cols] = (_dot(vt, p_all) / l).T.astype(o_ref.dtype)

    for n in range(n_blk):
        @pl.when(qb == n)
        def _(n=n):
            for hh in range(heads):
                attend(n, hh)


def moba_attention(q, k, v_t, k_mean, *, batch, seq):
    t, d = q.shape
    heads = d // HEAD_DIM
    n_blk = seq // MOBA_BLOCK
    hp = MOBA_HEADS_PER_STEP
    width = hp * HEAD_DIM
    slopes = (2.0 ** (-8.0 * jnp.arange(1, heads + 1, dtype=F32) / heads))
    slopes = jnp.broadcast_to(slopes[:, None, None], (heads, 1, LANES))
    return pl.pallas_call(
        _moba_kernel,
        grid=(batch, heads // hp, n_blk),
        in_specs=[
            pl.BlockSpec((MOBA_BLOCK, width), lambda b, h, i: (b * n_blk + i, h)),
            pl.BlockSpec((seq, width), lambda b, h, i: (b, h)),
            pl.BlockSpec((n_blk, width, MOBA_BLOCK), lambda b, h, i: (b, h, 0)),
            pl.BlockSpec((1, n_blk, width), lambda b, h, i: (b, 0, h)),
            pl.BlockSpec((hp, 1, LANES), lambda b, h, i: (h, 0, 0)),
        ],
        out_specs=pl.BlockSpec((MOBA_BLOCK, width), lambda b, h, i: (b * n_blk + i, h)),
        out_shape=jax.ShapeDtypeStruct((t, d), BF16),
        compiler_params=_params("parallel", "parallel", "arbitrary"),
    )(q, k, v_t, k_mean.reshape(batch, n_blk, d), slopes)


def _router_kernel(x_ref, g_ref, w_ref, h_ref, info_ref, info_t_ref, cnt_ref, *, n_experts):
    h = _rms(x_ref[...], g_ref[...])
    h_ref[...] = h.astype(BF16)
    logits = jnp.dot(h, w_ref[...], precision=lax.Precision.HIGHEST, preferred_element_type=F32)
    tm = logits.shape[0]
    lane = lax.broadcasted_iota(jnp.int32, logits.shape, 1)
    neg = -jnp.inf
    lg = jnp.where(lane < n_experts, logits, neg)
    m1 = jnp.max(lg, axis=1, keepdims=True)
    i1 = jnp.min(jnp.where(lg == m1, lane, LANES), axis=1, keepdims=True)
    lg2 = jnp.where(lane == i1, neg, lg)
    m2 = jnp.max(lg2, axis=1, keepdims=True)
    i2 = jnp.min(jnp.where(lg2 == m2, lane, LANES), axis=1, keepdims=True)
    e2 = jnp.exp(m2 - m1)
    w1 = 1.0 / (1.0 + e2)
    w2 = e2 / (1.0 + e2)

    routed = ((lane == i1) | (lane == i2)).astype(F32)
    cnt = jnp.sum(routed, axis=0, keepdims=True)
    cnt_ref[0] = cnt.astype(jnp.int32)
    earlier = (lax.broadcasted_iota(jnp.int32, (tm, tm), 1)
               < lax.broadcasted_iota(jnp.int32, (tm, tm), 0)).astype(BF16)
    rank = _dot(earlier, routed.astype(BF16))
    group_rows = jnp.floor((cnt + (ROW_ALIGN - 1)) * (1.0 / ROW_ALIGN)) * ROW_ALIGN
    lower = (lax.broadcasted_iota(jnp.int32, (LANES, LANES), 0)
             < lax.broadcasted_iota(jnp.int32, (LANES, LANES), 1)).astype(BF16)
    group_start = _dot(jnp.broadcast_to(group_rows, (8, LANES)).astype(BF16), lower)[0:1, :]
    slab_row = group_start + rank
    first = i1 < i2
    i_lo = jnp.where(first, i1, i2)
    i_hi = jnp.where(first, i2, i1)
    row_lo = jnp.sum(jnp.where(lane == i_lo, slab_row, 0.0), axis=1, keepdims=True)
    row_hi = jnp.sum(jnp.where(lane == i_hi, slab_row, 0.0), axis=1, keepdims=True)
    info = jnp.where(lane == 0, row_lo, 0.0) + jnp.where(lane == 1, row_hi, 0.0)
    info_t_ref[0] = info.T[0:8, :]
    info = info + jnp.where(lane == 2, jnp.where(first, w1, w2), 0.0)
    info_ref[...] = info + jnp.where(lane == 3, jnp.where(first, w2, w1), 0.0)


def router(x, gain, w_router):
    t, d = x.shape
    n_experts = w_router.shape[1]
    n_tiles = t // TOK_TILE
    w = jnp.pad(w_router, ((0, 0), (0, LANES - n_experts)))
    return pl.pallas_call(
        functools.partial(_router_kernel, n_experts=n_experts),
        grid=(n_tiles,),
        in_specs=[
            pl.BlockSpec((TOK_TILE, d), lambda i: (i, 0)),
            pl.BlockSpec((1, d), lambda i: (0, 0)),
            pl.BlockSpec((d, LANES), lambda i: (0, 0)),
        ],
        out_specs=[
            pl.BlockSpec((TOK_TILE, d), lambda i: (i, 0)),
            pl.BlockSpec((TOK_TILE, LANES), lambda i: (i, 0)),
            pl.BlockSpec((1, 8, TOK_TILE), lambda i: (i, 0, 0)),
            pl.BlockSpec((1, 1, LANES), lambda i: (i, 0, 0)),
        ],
        out_shape=[
            jax.ShapeDtypeStruct((t, d), BF16),
            jax.ShapeDtypeStruct((t, LANES), F32),
            jax.ShapeDtypeStruct((n_tiles, 8, TOK_TILE), F32),
            jax.ShapeDtypeStruct((n_tiles, 1, LANES), jnp.int32),
        ],
        compiler_params=_params("parallel"),
    )(x, gain.reshape(1, d), w)


def _slab_rows(n_experts):
    bound = MOE_TOPK * TOK_TILE + n_experts * (ROW_ALIGN - 1)
    return -(-bound // LANES) * LANES


def _group_copies(n_ref, rs_ref, tile, n_experts, make_copy, wait):
    slab_row = jnp.int32(0)
    for e in range(n_experts):
        n = n_ref[tile * n_experts + e]
        base = rs_ref[tile * n_experts + e]
        off = jnp.int32(0)
        size = TOK_TILE
        while size >= ROW_ALIGN:
            @pl.when((n & size) != 0)
            def _(off=off, size=size, slab_row=slab_row, base=base):
                cp = make_copy(pl.multiple_of(slab_row + off, ROW_ALIGN),
                               pl.multiple_of(base + off, ROW_ALIGN), size)
                if wait:
                    cp.wait()
                else:
                    cp.start()
            off = off + (n & size)
            size //= 2
        slab_row = slab_row + n


def _dispatch_kernel(rs_ref, n_ref, h_ref, info_t_ref, _, xs_ref, slab_ref, sem, *, n_experts):
    i = pl.program_id(0)
    rows, tm = slab_ref.shape[0], h_ref.shape[0]
    info_t = info_t_ref[0]
    r = lax.broadcasted_iota(jnp.int32, (rows, tm), 0).astype(F32)
    onehot = ((r == info_t[0:1, :]) | (r == info_t[1:2, :])).astype(BF16)
    slab_ref[...] = _dot(onehot, h_ref[...]).astype(BF16)

    def make_copy(slab_row, buffer_row, size):
        return pltpu.make_async_copy(slab_ref.at[pl.ds(slab_row, size)],
                                     xs_ref.at[pl.ds(buffer_row, size)], sem)

    _group_copies(n_ref, rs_ref, i, n_experts, make_copy, wait=False)
    _group_copies(n_ref, rs_ref, i, n_experts, make_copy, wait=True)


def dispatch(h, info_t, row_start, n_rows, *, total_rows, n_experts):
    t, d = h.shape
    n_tiles = t // TOK_TILE
    return pl.pallas_call(
        functools.partial(_dispatch_kernel, n_experts=n_experts),
        grid_spec=pltpu.PrefetchScalarGridSpec(
            num_scalar_prefetch=2,
            grid=(n_tiles,),
            in_specs=[
                pl.BlockSpec((TOK_TILE, d), lambda i, rs, n: (i, 0)),
                pl.BlockSpec((1, 8, TOK_TILE), lambda i, rs, n: (i, 0, 0)),
                pl.BlockSpec(memory_space=pl.ANY),
            ],
            out_specs=pl.BlockSpec(memory_space=pl.ANY),
            scratch_shapes=[pltpu.VMEM((_slab_rows(n_experts), d), BF16),
                            pltpu.SemaphoreType.DMA],
        ),
        out_shape=jax.ShapeDtypeStruct((total_rows, d), BF16),
        input_output_aliases={4: 0},
        compiler_params=_params("arbitrary"),
    )(row_start, n_rows, h, info_t, jnp.zeros((total_rows, d), BF16))


def _grouped_ffn_kernel(te_ref, na_ref, x_ref, wg_ref, wu_ref, wd_ref, o_ref, acc_ref):
    j = pl.program_id(0)
    f = pl.program_id(1)
    last_f = pl.num_programs(1) - 1
    active = j < na_ref[0]

    @pl.when(active)
    def _():
        @pl.when(f == 0)
        def _():
            acc_ref[...] = jnp.zeros_like(acc_ref)

        x = x_ref[...]
        act = _silu(_dot(x, wg_ref[...])) * _dot(x, wu_ref[...])
        acc_ref[...] += _dot(act.astype(BF16), wd_ref[...])

        @pl.when(f == last_f)
        def _():
            o_ref[...] = acc_ref[...].astype(o_ref.dtype)

    @pl.when(jnp.logical_not(active) & (f == last_f))
    def _():
        o_ref[...] = jnp.zeros_like(o_ref)


def grouped_ffn(xs, w_gu, w_down, tile_expert, n_active, *, f_tile):
    rows, d = xs.shape
    ff = w_down.shape[1]
    nf = ff // f_tile

    def f_idx(j, f, na):
        return jnp.where(j < na[0], f, nf - 1)

    return pl.pallas_call(
        _grouped_ffn_kernel,
        grid_spec=pltpu.PrefetchScalarGridSpec(
            num_scalar_prefetch=2,
            grid=(rows // FFN_ROWS, nf),
            in_specs=[
                pl.BlockSpec((FFN_ROWS, d), lambda j, f, te, na: (jnp.minimum(j, na[0] - 1), 0)),
                pl.BlockSpec((None, d, f_tile), lambda j, f, te, na: (te[j], 0, f_idx(j, f, na))),
                pl.BlockSpec((None, d, f_tile), lambda j, f, te, na: (te[j], 0, nf + f_idx(j, f, na))),
                pl.BlockSpec((None, f_tile, d), lambda j, f, te, na: (te[j], f_idx(j, f, na), 0)),
            ],
            out_specs=pl.BlockSpec((FFN_ROWS, d), lambda j, f, te, na: (j, 0)),
            scratch_shapes=[pltpu.VMEM((FFN_ROWS, d), F32)],
        ),
        out_shape=jax.ShapeDtypeStruct((rows, d), BF16),
        compiler_params=_params("arbitrary", "arbitrary"),
    )(tile_expert, n_active, xs, w_gu, w_gu, w_down)


def _combine_kernel(rs_ref, n_ref, x_ref, info_ref, y_ref, o_ref, ybuf_ref, sem, *, n_experts):
    i = pl.program_id(0)
    slot = i % 2
    rows, tm = ybuf_ref.shape[1], x_ref.shape[0]

    def copies(tile, slot, wait):
        def make_copy(slab_row, buffer_row, size):
            return pltpu.make_async_copy(y_ref.at[pl.ds(buffer_row, size)],
                                         ybuf_ref.at[slot, pl.ds(slab_row, size)], sem.at[slot])
        _group_copies(n_ref, rs_ref, tile, n_experts, make_copy, wait)

    @pl.when(i == 0)
    def _():
        ybuf_ref[...] = jnp.zeros_like(ybuf_ref)
        copies(i, slot, wait=False)

    @pl.when(i + 1 < pl.num_programs(0))
    def _():
        copies(i + 1, 1 - slot, wait=False)

    copies(i, slot, wait=True)
    y = ybuf_ref[slot]
    info = info_ref[...]
    r = lax.broadcasted_iota(jnp.int32, (tm, rows), 1).astype(F32)
    pick_lo = (r == info[:, 0:1]).astype(BF16)
    pick_hi = (r == info[:, 1:2]).astype(BF16)
    o_ref[...] = (x_ref[...] + info[:, 2:3] * _dot(pick_lo, y) + info[:, 3:4] * _dot(pick_hi, y))


def combine(x, info, y, row_start, n_rows, *, n_experts):
    t, d = x.shape
    n_tiles = t // TOK_TILE
    return pl.pallas_call(
        functools.partial(_combine_kernel, n_experts=n_experts),
        grid_spec=pltpu.PrefetchScalarGridSpec(
            num_scalar_prefetch=2,
            grid=(n_tiles,),
            in_specs=[
                pl.BlockSpec((TOK_TILE, d), lambda i, rs, n: (i, 0)),
                pl.BlockSpec((TOK_TILE, LANES), lambda i, rs, n: (i, 0)),
                pl.BlockSpec(memory_space=pl.ANY),
            ],
            out_specs=pl.BlockSpec((TOK_TILE, d), lambda i, rs, n: (i, 0)),
            scratch_shapes=[pltpu.VMEM((2, _slab_rows(n_experts), d), BF16),
                            pltpu.SemaphoreType.DMA((2,))],
        ),
        out_shape=jax.ShapeDtypeStruct((t, d), F32),
        compiler_params=_params("arbitrary"),
    )(row_start, n_rows, x, info, y)


def _ffn_kernel(x_ref, g_ref, wg_ref, wu_ref, wd_ref, o_ref, h_ref, acc_ref):
    f = pl.program_id(1)

    @pl.when(f == 0)
    def _():
        x = x_ref[...]
        h_ref[...] = _rms(x, g_ref[...]).astype(BF16)
        acc_ref[...] = x

    h = h_ref[...]
    act = _silu(_dot(h, wg_ref[...])) * _dot(h, wu_ref[...])
    acc_ref[...] += _dot(act.astype(BF16), wd_ref[...])

    @pl.when(f == pl.num_programs(1) - 1)
    def _():
        o_ref[...] = acc_ref[...]


def ffn(x, gain, w_gu, w_down, *, f_tile):
    t, d = x.shape
    ff = w_down.shape[0]
    nf = ff // f_tile
    return pl.pallas_call(
        _ffn_kernel,
        grid=(t // ROW_TILE, nf),
        in_specs=[
            pl.BlockSpec((ROW_TILE, d), lambda i, f: (i, 0)),
            pl.BlockSpec((1, d), lambda i, f: (0, 0)),
            pl.BlockSpec((d, f_tile), lambda i, f: (0, f)),
            pl.BlockSpec((d, f_tile), lambda i, f: (0, nf + f)),
            pl.BlockSpec((f_tile, d), lambda i, f: (f, 0)),
        ],
        out_specs=pl.BlockSpec((ROW_TILE, d), lambda i, f: (i, 0)),
        out_shape=jax.ShapeDtypeStruct((t, d), F32),
        scratch_shapes=[pltpu.VMEM((ROW_TILE, d), BF16), pltpu.VMEM((ROW_TILE, d), F32)],
        compiler_params=_params("parallel", "arbitrary"),
    )(x, gain.reshape(1, d), w_gu, w_gu, w_down)


def _rmsnorm_kernel(x_ref, g_ref, o_ref):
    o_ref[...] = _rms(x_ref[...], g_ref[...])


def rmsnorm(x, gain):
    t, d = x.shape
    return pl.pallas_call(
        _rmsnorm_kernel,
        grid=(t // ROW_TILE,),
        in_specs=[pl.BlockSpec((ROW_TILE, d), lambda i: (i, 0)),
                  pl.BlockSpec((1, d), lambda i: (0, 0))],
        out_specs=pl.BlockSpec((ROW_TILE, d), lambda i: (i, 0)),
        out_shape=jax.ShapeDtypeStruct((t, d), F32),
        compiler_params=_params("parallel"),
    )(x, gain.reshape(1, d))


def _lane_tile(n, target):
    best = LANES
    for cand in range(LANES, target + 1, LANES):
        if n % cand == 0:
            best = cand
    return best


def moe_layer(x, gain, w_router, w_gu, w_down):
    t, d = x.shape
    n_experts = w_router.shape[1]
    n_tiles = t // TOK_TILE
    h, info, info_t, cnt = router(x, gain, w_router)
    cnt = cnt.reshape(n_tiles, LANES)[:, :n_experts]
    n_rows = (cnt + ROW_ALIGN - 1) // ROW_ALIGN * ROW_ALIGN
    seg_rows = jnp.sum(n_rows, axis=0)
    seg_cap = (seg_rows + FFN_ROWS - 1) // FFN_ROWS * FFN_ROWS
    seg_end = jnp.cumsum(seg_cap)
    row_start = (seg_end - seg_cap)[None, :] + jnp.cumsum(n_rows, axis=0) - n_rows
    max_rows = (MOE_TOPK * t + n_tiles * n_experts * (ROW_ALIGN - 1)
                + n_experts * (FFN_ROWS - ROW_ALIGN))
    total_tiles = -(-max_rows // FFN_ROWS)
    n_active = (seg_end[-1:] // FFN_ROWS).astype(jnp.int32)
    tile_expert = jnp.minimum(
        jnp.searchsorted(seg_end, jnp.arange(total_tiles, dtype=jnp.int32) * FFN_ROWS, side="right"),
        n_experts - 1).astype(jnp.int32)
    row_start = row_start.reshape(-1).astype(jnp.int32)
    n_rows = n_rows.reshape(-1).astype(jnp.int32)
    xs = dispatch(h, info_t, row_start, n_rows,
                  total_rows=total_tiles * FFN_ROWS, n_experts=n_experts)
    y = grouped_ffn(xs, w_gu.astype(BF16), w_down.astype(BF16), tile_expert, n_active,
                    f_tile=_lane_tile(w_down.shape[1], 1024))
    return combine(x, info, y, row_start, n_rows, n_experts=n_experts)


def kernel(x, attn_norm, ffn_norm, hgrn_w_in, hgrn_lb_logits, hgrn_out_norm, hgrn_w_o, kv_norm, w_kv, moba_w_q, moba_w_o, ffn_w_gu, ffn_w_down, moe_router, moe_w_gu, moe_w_down, final_norm):
    batch, seq, d = x.shape
    depth = attn_norm.shape[0]
    n_a = hgrn_w_in.shape[0]
    x = x.reshape(batch * seq, d)
    k = v_t = k_mean = None
    for layer in range(depth):
        if layer < n_a:
            proj = norm_matmul(x, attn_norm[layer], hgrn_w_in[layer].astype(BF16), F32)
            o = hgrn_recurrence(proj, hgrn_lb_logits, hgrn_out_norm[layer],
                                layer=layer, batch=batch, seq=seq)
            x = matmul_residual(x, o, hgrn_w_o[layer].astype(BF16))
        else:
            if layer == n_a:
                k, v_t, k_mean = shared_kv(x, kv_norm, w_kv[:, :d].astype(BF16),
                                           w_kv[:, d:].T.astype(BF16))
            j = layer - n_a
            q = norm_matmul(x, attn_norm[layer], moba_w_q[j].astype(BF16), F32)
            o = moba_attention(q, k, v_t, k_mean, batch=batch, seq=seq)
            x = matmul_residual(x, o, moba_w_o[j].astype(BF16))
        if layer % 2 == 0:
            j = layer // 2
            x = ffn(x, ffn_norm[layer], ffn_w_gu[j].astype(BF16), ffn_w_down[j].astype(BF16),
                    f_tile=_lane_tile(ffn_w_down.shape[1], 1536))
        else:
            j = layer // 2
            x = moe_layer(x, ffn_norm[layer], moe_router[j], moe_w_gu[j], moe_w_down[j])
    return rmsnorm(x, final_norm).reshape(batch, seq, d)
```

```python
import functools

import jax
import jax.numpy as jnp
from jax import lax
from jax.experimental import pallas as pl
from jax.experimental.pallas import tpu as pltpu

F32 = jnp.float32
BF16 = jnp.bfloat16

EPS = 1e-6
HEAD_DIM = 128
HGRN_CHUNK = 64
MOBA_BLOCK = 256
MOBA_TOPK = 3
MOE_TOPK = 2
MOBA_HEADS_PER_STEP = 2
LOG2E = 1.4426950408889634
LANES = 128
VMEM_LIMIT = 56 * 1024 * 1024

HGRN_TILE = 512
HGRN_HEADS_PER_STEP = 2
ROW_TILE = 512
TOK_TILE = 512
FFN_ROWS = 512
ROW_ALIGN = 16


def _params(*semantics):
    return pltpu.CompilerParams(dimension_semantics=semantics, vmem_limit_bytes=VMEM_LIMIT)


def _rms(x, gain):
    return x * lax.rsqrt(jnp.mean(x * x, axis=-1, keepdims=True) + EPS) * gain


def _dot(a, b):
    return jnp.dot(a, b, preferred_element_type=F32)


def _dot_nt(a, b):
    return lax.dot_general(a, b, (((1,), (1,)), ((), ())), preferred_element_type=F32)


def _dot_tn(a, b):
    return lax.dot_general(a, b, (((0,), (0,)), ((), ())), preferred_element_type=F32)


def _silu(x):
    half = 0.5 * x
    return half + half * jnp.tanh(half)


def _norm_matmul_kernel(x_ref, g_ref, w_ref, o_ref):
    h = _rms(x_ref[...], g_ref[...]).astype(BF16)
    o_ref[...] = _dot(h, w_ref[...]).astype(o_ref.dtype)


def norm_matmul(x, gain, w, out_dtype):
    t, d = x.shape
    n = w.shape[1]
    return pl.pallas_call(
        _norm_matmul_kernel,
        grid=(t // ROW_TILE,),
        in_specs=[
            pl.BlockSpec((ROW_TILE, d), lambda i: (i, 0)),
            pl.BlockSpec((1, d), lambda i: (0, 0)),
            pl.BlockSpec((d, n), lambda i: (0, 0)),
        ],
        out_specs=pl.BlockSpec((ROW_TILE, n), lambda i: (i, 0)),
        out_shape=jax.ShapeDtypeStruct((t, n), out_dtype),
        compiler_params=_params("parallel"),
    )(x, gain.reshape(1, d), w)


def _kv_kernel(x_ref, g_ref, wk_ref, wvt_ref, k_ref, vt_ref, km_ref):
    h = _rms(x_ref[...], g_ref[...]).astype(BF16)
    k = _dot(h, wk_ref[...])
    k_ref[...] = k.astype(k_ref.dtype)
    vt_ref[0] = _dot_nt(wvt_ref[...], h).astype(vt_ref.dtype)
    km_ref[0] = jnp.mean(k, axis=0, keepdims=True)


def shared_kv(x, gain, w_k, w_v_t):
    t, d = x.shape
    nblk = t // MOBA_BLOCK
    return pl.pallas_call(
        _kv_kernel,
        grid=(nblk,),
        in_specs=[
            pl.BlockSpec((MOBA_BLOCK, d), lambda i: (i, 0)),
            pl.BlockSpec((1, d), lambda i: (0, 0)),
            pl.BlockSpec((d, d), lambda i: (0, 0)),
            pl.BlockSpec((d, d), lambda i: (0, 0)),
        ],
        out_specs=[
            pl.BlockSpec((MOBA_BLOCK, d), lambda i: (i, 0)),
            pl.BlockSpec((1, d, MOBA_BLOCK), lambda i: (i, 0, 0)),
            pl.BlockSpec((1, 1, d), lambda i: (i, 0, 0)),
        ],
        out_shape=[
            jax.ShapeDtypeStruct((t, d), BF16),
            jax.ShapeDtypeStruct((nblk, d, MOBA_BLOCK), BF16),
            jax.ShapeDtypeStruct((nblk, 1, d), F32),
        ],
        compiler_params=_params("parallel"),
    )(x, gain.reshape(1, d), w_k, w_v_t)


def _matmul_residual_kernel(x_ref, a_ref, w_ref, o_ref):
    o_ref[...] = x_ref[...] + _dot(a_ref[...], w_ref[...])


def matmul_residual(x, a, w):
    t, d = x.shape
    k = a.shape[1]
    return pl.pallas_call(
        _matmul_residual_kernel,
        grid=(t // ROW_TILE,),
        in_specs=[
            pl.BlockSpec((ROW_TILE, d), lambda i: (i, 0)),
            pl.BlockSpec((ROW_TILE, k), lambda i: (i, 0)),
            pl.BlockSpec((k, d), lambda i: (0, 0)),
        ],
        out_specs=pl.BlockSpec((ROW_TILE, d), lambda i: (i, 0)),
        out_shape=jax.ShapeDtypeStruct((t, d), F32),
        compiler_params=_params("parallel"),
    )(x, a, w)


def _hgrn_kernel(q_ref, f_ref, i_ref, g_ref, lbl_ref, gain_ref, o_ref, st_ref, *, layer):
    c = HGRN_CHUNK
    dk = HEAD_DIM
    heads = q_ref.shape[1] // dk
    per_head = q_ref.shape[0] // c
    n_chunks = heads * per_head
    group = MOBA_BLOCK // c

    def wide(ref):
        return jnp.concatenate([ref[n * c:(n + 1) * c, hh * dk:(hh + 1) * dk]
                                for hh in range(heads) for n in range(per_head)], axis=1)

    def tall(ref):
        return jnp.concatenate([ref[:, hh * dk:(hh + 1) * dk] for hh in range(heads)], axis=0)

    def chunk(x, n):
        return x[:, n * dk:(n + 1) * dk]

    def rows_of(x, n0, n1):
        return jnp.concatenate([chunk(x, n) for n in range(n0, n1)], axis=0)

    @pl.when(pl.program_id(2) == 0)
    def _():
        st_ref[...] = jnp.zeros_like(st_ref)

    fz = wide(f_ref)
    t = jnp.exp(-jnp.abs(fz))
    log_sig = jnp.minimum(fz, 0.0) - jnp.log(1.0 + t)
    sig_neg = jnp.exp(log_sig - fz)
    if layer > 0:
        logits = lbl_ref[...]
        e = jnp.exp(logits - jnp.max(logits, axis=0, keepdims=True))
        p = e / jnp.sum(e, axis=0, keepdims=True)
        lb = jnp.sum(p[1:layer + 1], axis=0, keepdims=True)
        lb = jnp.concatenate([chunk(lb, hh) for hh in range(heads) for _ in range(per_head)],
                             axis=1)
        log_lb = jnp.log(lb)
        y = jnp.log(1.0 - lb) + log_sig
        log_f = jnp.maximum(log_lb, y) + jnp.log(1.0 + jnp.exp(-jnp.abs(log_lb - y)))
        k = (1.0 - lb) * sig_neg
    else:
        log_f = log_sig
        k = sig_neg

    tril = (lax.broadcasted_iota(jnp.int32, (c, c), 1)
            <= lax.broadcasted_iota(jnp.int32, (c, c), 0)).astype(BF16)
    hi = log_f.astype(BF16)
    rest = log_f - hi.astype(F32)
    mid = rest.astype(BF16)
    lo = (rest - mid.astype(F32)).astype(BF16)
    b = _dot(tril, hi) + _dot(tril, mid) + _dot(tril, lo)
    b_mid = b[c // 2 - 1:c // 2, :]
    b_last = b[c - 1:c, :]

    q = _silu(wide(q_ref))
    qd = q * jnp.exp(b - b_mid)
    kd = k * jnp.exp(b_mid - b)
    qe = (qd * jnp.exp(b_mid)).astype(BF16)
    kl = (kd * jnp.exp(b_last - b_mid)).astype(BF16)
    qd = qd.astype(BF16)
    kd = kd.astype(BF16)
    v = wide(i_ref).astype(BF16)
    decay = jnp.exp(b_last)

    gr = group * c
    r_i = lax.broadcasted_iota(jnp.int32, (gr, gr), 0)
    c_i = lax.broadcasted_iota(jnp.int32, (gr, gr), 1)
    keep = (c_i <= r_i) & ((r_i // c) == (c_i // c))
    groups = range(0, n_chunks, group)
    scores = [_dot_nt(rows_of(qd, n0, n0 + group), rows_of(kd, n0, n0 + group)) for n0 in groups]

    grow = [_dot_tn(chunk(v, n), chunk(kl, n)) for n in range(n_chunks)]
    intra = [_dot(jnp.where(keep, a, 0.0).astype(BF16), rows_of(v, n0, n0 + group))
             for a, n0 in zip(scores, groups)]
    before = []
    for hh in range(heads):
        st = st_ref[hh]
        for n in range(hh * per_head, (hh + 1) * per_head):
            before.append(st.astype(BF16))
            st = chunk(decay, n) * st + grow[n]
        st_ref[hh] = st
    inter = [_dot_nt(chunk(qe, n), before[n]) for n in range(n_chunks)]
    o = jnp.concatenate(intra, axis=0) + jnp.concatenate(inter, axis=0)

    o = o * lax.rsqrt(jnp.mean(o * o, axis=-1, keepdims=True) + EPS)
    o = o * _silu(tall(g_ref))
    rows = q_ref.shape[0]
    for hh in range(heads):
        o_ref[:, hh * dk:(hh + 1) * dk] = (o[hh * rows:(hh + 1) * rows]
                                           * gain_ref[:, hh * dk:(hh + 1) * dk]).astype(o_ref.dtype)


def hgrn_recurrence(proj, lb_logits, out_gain, *, layer, batch, seq):
    t = proj.shape[0]
    d = proj.shape[1] // 4
    heads = d // HEAD_DIM
    tiles = seq // HGRN_TILE
    n_layers = lb_logits.shape[0]
    hp = HGRN_HEADS_PER_STEP
    width = hp * HEAD_DIM
    steps = heads // hp

    def act(part):
        return pl.BlockSpec((HGRN_TILE, width),
                            lambda b, h, s: (b * tiles + s, part * steps + h))

    return pl.pallas_call(
        functools.partial(_hgrn_kernel, layer=layer),
        grid=(batch, steps, tiles),
        in_specs=[
            act(0), act(1), act(2), act(3),
            pl.BlockSpec((n_layers, width), lambda b, h, s: (0, h)),
            pl.BlockSpec((1, width), lambda b, h, s: (0, h)),
        ],
        out_specs=pl.BlockSpec((HGRN_TILE, width), lambda b, h, s: (b * tiles + s, h)),
        out_shape=jax.ShapeDtypeStruct((t, d), BF16),
        scratch_shapes=[pltpu.VMEM((hp, HEAD_DIM, HEAD_DIM), F32)],
        compiler_params=_params("parallel", "parallel", "arbitrary"),
    )(proj, proj, proj, proj, lb_logits, out_gain.reshape(1, d))


def _moba_kernel(q_ref, k_ref, vt_ref, km_ref, slope_ref, o_ref):
    bs = MOBA_BLOCK
    dh = HEAD_DIM
    qb = pl.program_id(2)
    n_blk = km_ref.shape[1]
    heads = q_ref.shape[1] // dh
    scale2 = HEAD_DIM ** -0.5 * LOG2E
    rel = (lax.broadcasted_iota(jnp.int32, (bs, bs), 1)
           - lax.broadcasted_iota(jnp.int32, (bs, bs), 0))
    rel_f = rel.astype(F32)

    def gate_rank(q, km, n):
        gate = lax.dot_general(km, q, (((1,), (1,)), ((), ())),
                               precision=lax.Precision.HIGHEST, preferred_element_type=F32)
        blk = lax.broadcasted_iota(jnp.int32, gate.shape, 0)
        rank = jnp.zeros(gate.shape, F32)
        for m in range(n):
            g_m = gate[m:m + 1, :]
            rank = rank + ((g_m > gate) | ((g_m == gate) & (m < blk))).astype(F32)
        return rank

    def scores(n, hh):
        cols = slice(hh * dh, (hh + 1) * dh)
        q = q_ref[:, cols]
        q16 = (q * scale2).astype(BF16)
        rank = gate_rank(q, km_ref[0][:, cols], n) if n > MOBA_TOPK else None
        return _dot_nt(k_ref[0:(n + 1) * bs, cols], q16), rank

    def probabilities(n, hh, s_all, rank):
        slope2 = slope_ref[hh][0:1, 0:1] * LOG2E
        bias = -slope2 * rel_f
        us, shifts, maxes = [], [], []
        for kb in range(n + 1):
            u = s_all[kb * bs:(kb + 1) * bs] + bias
            if kb == n:
                u = jnp.where(rel >= 0, u, -jnp.inf)
            elif rank is not None:
                u = jnp.where(rank[kb:kb + 1, :] < MOBA_TOPK, u, -jnp.inf)
            shift = slope2 * float((n - kb) * bs)
            us.append(u)
            shifts.append(shift)
            maxes.append(jnp.max(u, axis=0, keepdims=True) - shift)
        m = functools.reduce(jnp.maximum, maxes)
        ps = [jnp.exp2(u - (m + shift)) for u, shift in zip(us, shifts)]
        return jnp.concatenate(ps, axis=0).astype(BF16)

    def output(n, hh, p_all):
        cols = slice(hh * dh, (hh + 1) * dh)
        vt = jnp.concatenate([vt_ref[kb, cols, :] for kb in range(n + 1)], axis=1)
        ones = jnp.ones((ROW_ALIGN, vt.shape[1]), BF16)
        pv = _dot(jnp.concatenate([vt, ones], axis=0), p_all)
        o_ref[:, cols] = (pv[0:dh] / pv[dh:dh + 1]).T.astype(o_ref.dtype)

    for n in range(n_blk):
        @pl.when(qb == n)
        def _(n=n):
            scored = [scores(n, hh) for hh in range(heads)]
            probs = [probabilities(n, hh, *scored[hh]) for hh in range(heads)]
            for hh in range(heads):
                output(n, hh, probs[hh])


def moba_attention(q, k, v_t, k_mean, *, batch, seq):
    t, d = q.shape
    heads = d // HEAD_DIM
    n_blk = seq // MOBA_BLOCK
    hp = MOBA_HEADS_PER_STEP
    width = hp * HEAD_DIM
    slopes = (2.0 ** (-8.0 * jnp.arange(1, heads + 1, dtype=F32) / heads))
    slopes = jnp.broadcast_to(slopes[:, None, None], (heads, 1, LANES))
    return pl.pallas_call(
        _moba_kernel,
        grid=(batch, heads // hp, n_blk),
        in_specs=[
            pl.BlockSpec((MOBA_BLOCK, width), lambda b, h, i: (b * n_blk + i, h)),
            pl.BlockSpec((seq, width), lambda b, h, i: (b, h)),
            pl.BlockSpec((n_blk, width, MOBA_BLOCK), lambda b, h, i: (b, h, 0)),
            pl.BlockSpec((1, n_blk, width), lambda b, h, i: (b, 0, h)),
            pl.BlockSpec((hp, 1, LANES), lambda b, h, i: (h, 0, 0)),
        ],
        out_specs=pl.BlockSpec((MOBA_BLOCK, width), lambda b, h, i: (b * n_blk + i, h)),
        out_shape=jax.ShapeDtypeStruct((t, d), BF16),
        compiler_params=_params("parallel", "parallel", "arbitrary"),
    )(q, k, v_t, k_mean.reshape(batch, n_blk, d), slopes)


def _router_kernel(x_ref, g_ref, w_ref, h_ref, info_ref, info_t_ref, cnt_ref, *, n_experts):
    h = _rms(x_ref[...], g_ref[...])
    h_ref[...] = h.astype(BF16)
    logits = jnp.dot(h, w_ref[...], precision=lax.Precision.HIGHEST, preferred_element_type=F32)
    tm = logits.shape[0]
    lane = lax.broadcasted_iota(jnp.int32, logits.shape, 1)
    neg = -jnp.inf
    lg = jnp.where(lane < n_experts, logits, neg)
    m1 = jnp.max(lg, axis=1, keepdims=True)
    i1 = jnp.min(jnp.where(lg == m1, lane, LANES), axis=1, keepdims=True)
    lg2 = jnp.where(lane == i1, neg, lg)
    m2 = jnp.max(lg2, axis=1, keepdims=True)
    i2 = jnp.min(jnp.where(lg2 == m2, lane, LANES), axis=1, keepdims=True)
    e2 = jnp.exp(m2 - m1)
    w1 = 1.0 / (1.0 + e2)
    w2 = e2 / (1.0 + e2)

    routed = ((lane == i1) | (lane == i2)).astype(F32)
    cnt = jnp.sum(routed, axis=0, keepdims=True)
    cnt_ref[0] = cnt.astype(jnp.int32)
    earlier = (lax.broadcasted_iota(jnp.int32, (tm, tm), 1)
               < lax.broadcasted_iota(jnp.int32, (tm, tm), 0)).astype(BF16)
    rank = _dot(earlier, routed.astype(BF16))
    group_rows = jnp.floor((cnt + (ROW_ALIGN - 1)) * (1.0 / ROW_ALIGN)) * ROW_ALIGN
    lower = (lax.broadcasted_iota(jnp.int32, (LANES, LANES), 0)
             < lax.broadcasted_iota(jnp.int32, (LANES, LANES), 1)).astype(BF16)
    group_start = _dot(jnp.broadcast_to(group_rows, (8, LANES)).astype(BF16), lower)[0:1, :]
    slab_row = group_start + rank
    first = i1 < i2
    i_lo = jnp.where(first, i1, i2)
    i_hi = jnp.where(first, i2, i1)
    row_lo = jnp.sum(jnp.where(lane == i_lo, slab_row, 0.0), axis=1, keepdims=True)
    row_hi = jnp.sum(jnp.where(lane == i_hi, slab_row, 0.0), axis=1, keepdims=True)
    info = jnp.where(lane == 0, row_lo, 0.0) + jnp.where(lane == 1, row_hi, 0.0)
    info_t_ref[0] = info.T[0:8, :]
    info = info + jnp.where(lane == 2, jnp.where(first, w1, w2), 0.0)
    info_ref[...] = info + jnp.where(lane == 3, jnp.where(first, w2, w1), 0.0)


def router(x, gain, w_router):
    t, d = x.shape
    n_experts = w_router.shape[1]
    n_tiles = t // TOK_TILE
    w = jnp.pad(w_router, ((0, 0), (0, LANES - n_experts)))
    return pl.pallas_call(
        functools.partial(_router_kernel, n_experts=n_experts),
        grid=(n_tiles,),
        in_specs=[
            pl.BlockSpec((TOK_TILE, d), lambda i: (i, 0)),
            pl.BlockSpec((1, d), lambda i: (0, 0)),
            pl.BlockSpec((d, LANES), lambda i: (0, 0)),
        ],
        out_specs=[
            pl.BlockSpec((TOK_TILE, d), lambda i: (i, 0)),
            pl.BlockSpec((TOK_TILE, LANES), lambda i: (i, 0)),
            pl.BlockSpec((1, 8, TOK_TILE), lambda i: (i, 0, 0)),
            pl.BlockSpec((1, 1, LANES), lambda i: (i, 0, 0)),
        ],
        out_shape=[
            jax.ShapeDtypeStruct((t, d), BF16),
            jax.ShapeDtypeStruct((t, LANES), F32),
            jax.ShapeDtypeStruct((n_tiles, 8, TOK_TILE), F32),
            jax.ShapeDtypeStruct((n_tiles, 1, LANES), jnp.int32),
        ],
        compiler_params=_params("parallel"),
    )(x, gain.reshape(1, d), w)


def _slab_rows(n_experts):
    bound = MOE_TOPK * TOK_TILE + n_experts * (ROW_ALIGN - 1)
    return -(-bound // LANES) * LANES


def _group_copies(n_ref, rs_ref, tile, n_experts, make_copy, wait):
    slab_row = jnp.int32(0)
    for e in range(n_experts):
        n = n_ref[tile * n_experts + e]
        base = rs_ref[tile * n_experts + e]
        off = jnp.int32(0)
        size = TOK_TILE
        while size >= ROW_ALIGN:
            @pl.when((n & size) != 0)
            def _(off=off, size=size, slab_row=slab_row, base=base):
                cp = make_copy(pl.multiple_of(slab_row + off, ROW_ALIGN),
                               pl.multiple_of(base + off, ROW_ALIGN), size)
                if wait:
                    cp.wait()
                else:
                    cp.start()
            off = off + (n & size)
            size //= 2
        slab_row = slab_row + n


def _dispatch_kernel(rs_ref, n_ref, h_ref, info_t_ref, _, xs_ref, slab_ref, sem, *, n_experts):
    i = pl.program_id(0)
    rows, tm = slab_ref.shape[0], h_ref.shape[0]
    info_t = info_t_ref[0]
    r = lax.broadcasted_iota(jnp.int32, (rows, tm), 0).astype(F32)
    onehot = ((r == info_t[0:1, :]) | (r == info_t[1:2, :])).astype(BF16)
    slab_ref[...] = _dot(onehot, h_ref[...]).astype(BF16)

    def make_copy(slab_row, buffer_row, size):
        return pltpu.make_async_copy(slab_ref.at[pl.ds(slab_row, size)],
                                     xs_ref.at[pl.ds(buffer_row, size)], sem)

    _group_copies(n_ref, rs_ref, i, n_experts, make_copy, wait=False)
    _group_copies(n_ref, rs_ref, i, n_experts, make_copy, wait=True)


def dispatch(h, info_t, row_start, n_rows, *, total_rows, n_experts):
    t, d = h.shape
    n_tiles = t // TOK_TILE
    return pl.pallas_call(
        functools.partial(_dispatch_kernel, n_experts=n_experts),
        grid_spec=pltpu.PrefetchScalarGridSpec(
            num_scalar_prefetch=2,
            grid=(n_tiles,),
            in_specs=[
                pl.BlockSpec((TOK_TILE, d), lambda i, rs, n: (i, 0)),
                pl.BlockSpec((1, 8, TOK_TILE), lambda i, rs, n: (i, 0, 0)),
                pl.BlockSpec(memory_space=pl.ANY),
            ],
            out_specs=pl.BlockSpec(memory_space=pl.ANY),
            scratch_shapes=[pltpu.VMEM((_slab_rows(n_experts), d), BF16),
                            pltpu.SemaphoreType.DMA],
        ),
        out_shape=jax.ShapeDtypeStruct((total_rows, d), BF16),
        input_output_aliases={4: 0},
        compiler_params=_params("arbitrary"),
    )(row_start, n_rows, h, info_t, jnp.zeros((total_rows, d), BF16))


def _grouped_ffn_kernel(te_ref, na_ref, x_ref, wg_ref, wu_ref, wd_ref, o_ref, acc_ref):
    j = pl.program_id(0)
    f = pl.program_id(1)
    last_f = pl.num_programs(1) - 1
    active = j < na_ref[0]

    @pl.when(active)
    def _():
        @pl.when(f == 0)
        def _():
            acc_ref[...] = jnp.zeros_like(acc_ref)

        x = x_ref[...]
        act = _silu(_dot(x, wg_ref[...])) * _dot(x, wu_ref[...])
        acc_ref[...] += _dot(act.astype(BF16), wd_ref[...])

        @pl.when(f == last_f)
        def _():
            o_ref[...] = acc_ref[...].astype(o_ref.dtype)

    @pl.when(jnp.logical_not(active) & (f == last_f))
    def _():
        o_ref[...] = jnp.zeros_like(o_ref)


def grouped_ffn(xs, w_gu, w_down, tile_expert, n_active, *, f_tile):
    rows, d = xs.shape
    ff = w_down.shape[1]
    nf = ff // f_tile

    def f_idx(j, f, na):
        return jnp.where(j < na[0], f, nf - 1)

    return pl.pallas_call(
        _grouped_ffn_kernel,
        grid_spec=pltpu.PrefetchScalarGridSpec(
            num_scalar_prefetch=2,
            grid=(rows // FFN_ROWS, nf),
            in_specs=[
                pl.BlockSpec((FFN_ROWS, d), lambda j, f, te, na: (jnp.minimum(j, na[0] - 1), 0)),
                pl.BlockSpec((None, d, f_tile), lambda j, f, te, na: (te[j], 0, f_idx(j, f, na))),
                pl.BlockSpec((None, d, f_tile), lambda j, f, te, na: (te[j], 0, nf + f_idx(j, f, na))),
                pl.BlockSpec((None, f_tile, d), lambda j, f, te, na: (te[j], f_idx(j, f, na), 0)),
            ],
            out_specs=pl.BlockSpec((FFN_ROWS, d), lambda j, f, te, na: (j, 0)),
            scratch_shapes=[pltpu.VMEM((FFN_ROWS, d), F32)],
        ),
        out_shape=jax.ShapeDtypeStruct((rows, d), BF16),
        compiler_params=_params("arbitrary", "arbitrary"),
    )(tile_expert, n_active, xs, w_gu, w_gu, w_down)


def _combine_kernel(rs_ref, n_ref, x_ref, info_ref, y_ref, o_ref, ybuf_ref, sem, *, n_experts):
    i = pl.program_id(0)
    slot = i % 2
    rows, tm = ybuf_ref.shape[1], x_ref.shape[0]

    def copies(tile, slot, wait):
        def make_copy(slab_row, buffer_row, size):
            return pltpu.make_async_copy(y_ref.at[pl.ds(buffer_row, size)],
                                         ybuf_ref.at[slot, pl.ds(slab_row, size)], sem.at[slot])
        _group_copies(n_ref, rs_ref, tile, n_experts, make_copy, wait)

    @pl.when(i == 0)
    def _():
        ybuf_ref[...] = jnp.zeros_like(ybuf_ref)
        copies(i, slot, wait=False)

    @pl.when(i + 1 < pl.num_programs(0))
    def _():
        copies(i + 1, 1 - slot, wait=False)

    copies(i, slot, wait=True)
    y = ybuf_ref[slot]
    info = info_ref[...]
    r = lax.broadcasted_iota(jnp.int32, (tm, rows), 1).astype(F32)
    pick_lo = (r == info[:, 0:1]).astype(BF16)
    pick_hi = (r == info[:, 1:2]).astype(BF16)
    o_ref[...] = (x_ref[...] + info[:, 2:3] * _dot(pick_lo, y) + info[:, 3:4] * _dot(pick_hi, y))


def combine(x, info, y, row_start, n_rows, *, n_experts):
    t, d = x.shape
    n_tiles = t // TOK_TILE
    return pl.pallas_call(
        functools.partial(_combine_kernel, n_experts=n_experts),
        grid_spec=pltpu.PrefetchScalarGridSpec(
            num_scalar_prefetch=2,
            grid=(n_tiles,),
            in_specs=[
                pl.BlockSpec((TOK_TILE, d), lambda i, rs, n: (i, 0)),
                pl.BlockSpec((TOK_TILE, LANES), lambda i, rs, n: (i, 0)),
                pl.BlockSpec(memory_space=pl.ANY),
            ],
            out_specs=pl.BlockSpec((TOK_TILE, d), lambda i, rs, n: (i, 0)),
            scratch_shapes=[pltpu.VMEM((2, _slab_rows(n_experts), d), BF16),
                            pltpu.SemaphoreType.DMA((2,))],
        ),
        out_shape=jax.ShapeDtypeStruct((t, d), F32),
        compiler_params=_params("arbitrary"),
    )(row_start, n_rows, x, info, y)


def _ffn_kernel(x_ref, g_ref, wg_ref, wu_ref, wd_ref, o_ref, h_ref, acc_ref):
    f = pl.program_id(1)

    @pl.when(f == 0)
    def _():
        x = x_ref[...]
        h_ref[...] = _rms(x, g_ref[...]).astype(BF16)
        acc_ref[...] = x

    h = h_ref[...]
    act = _silu(_dot(h, wg_ref[...])) * _dot(h, wu_ref[...])
    acc_ref[...] += _dot(act.astype(BF16), wd_ref[...])

    @pl.when(f == pl.num_programs(1) - 1)
    def _():
        o_ref[...] = acc_ref[...]


def ffn(x, gain, w_gu, w_down, *, f_tile):
    t, d = x.shape
    ff = w_down.shape[0]
    nf = ff // f_tile
    return pl.pallas_call(
        _ffn_kernel,
        grid=(t // ROW_TILE, nf),
        in_specs=[
            pl.BlockSpec((ROW_TILE, d), lambda i, f: (i, 0)),
            pl.BlockSpec((1, d), lambda i, f: (0, 0)),
            pl.BlockSpec((d, f_tile), lambda i, f: (0, f)),
            pl.BlockSpec((d, f_tile), lambda i, f: (0, nf + f)),
            pl.BlockSpec((f_tile, d), lambda i, f: (f, 0)),
        ],
        out_specs=pl.BlockSpec((ROW_TILE, d), lambda i, f: (i, 0)),
        out_shape=jax.ShapeDtypeStruct((t, d), F32),
        scratch_shapes=[pltpu.VMEM((ROW_TILE, d), BF16), pltpu.VMEM((ROW_TILE, d), F32)],
        compiler_params=_params("parallel", "arbitrary"),
    )(x, gain.reshape(1, d), w_gu, w_gu, w_down)


def _rmsnorm_kernel(x_ref, g_ref, o_ref):
    o_ref[...] = _rms(x_ref[...], g_ref[...])


def rmsnorm(x, gain):
    t, d = x.shape
    return pl.pallas_call(
        _rmsnorm_kernel,
        grid=(t // ROW_TILE,),
        in_specs=[pl.BlockSpec((ROW_TILE, d), lambda i: (i, 0)),
                  pl.BlockSpec((1, d), lambda i: (0, 0))],
        out_specs=pl.BlockSpec((ROW_TILE, d), lambda i: (i, 0)),
        out_shape=jax.ShapeDtypeStruct((t, d), F32),
        compiler_params=_params("parallel"),
    )(x, gain.reshape(1, d))


def _lane_tile(n, target):
    best = LANES
    for cand in range(LANES, target + 1, LANES):
        if n % cand == 0:
            best = cand
    return best


def moe_layer(x, gain, w_router, w_gu, w_down):
    t, d = x.shape
    n_experts = w_router.shape[1]
    n_tiles = t // TOK_TILE
    h, info, info_t, cnt = router(x, gain, w_router)
    cnt = cnt.reshape(n_tiles, LANES)[:, :n_experts]
    n_rows = (cnt + ROW_ALIGN - 1) // ROW_ALIGN * ROW_ALIGN
    seg_rows = jnp.sum(n_rows, axis=0)
    seg_cap = (seg_rows + FFN_ROWS - 1) // FFN_ROWS * FFN_ROWS
    seg_end = jnp.cumsum(seg_cap)
    row_start = (seg_end - seg_cap)[None, :] + jnp.cumsum(n_rows, axis=0) - n_rows
    max_rows = (MOE_TOPK * t + n_tiles * n_experts * (ROW_ALIGN - 1)
                + n_experts * (FFN_ROWS - ROW_ALIGN))
    total_tiles = -(-max_rows // FFN_ROWS)
    n_active = (seg_end[-1:] // FFN_ROWS).astype(jnp.int32)
    tile_expert = jnp.minimum(
        jnp.searchsorted(seg_end, jnp.arange(total_tiles, dtype=jnp.int32) * FFN_ROWS, side="right"),
        n_experts - 1).astype(jnp.int32)
    row_start = row_start.reshape(-1).astype(jnp.int32)
    n_rows = n_rows.reshape(-1).astype(jnp.int32)
    xs = dispatch(h, info_t, row_start, n_rows,
                  total_rows=total_tiles * FFN_ROWS, n_experts=n_experts)
    y = grouped_ffn(xs, w_gu.astype(BF16), w_down.astype(BF16), tile_expert, n_active,
                    f_tile=_lane_tile(w_down.shape[1], 1024))
    return combine(x, info, y, row_start, n_rows, n_experts=n_experts)


def kernel(x, attn_norm, ffn_norm, hgrn_w_in, hgrn_lb_logits, hgrn_out_norm, hgrn_w_o, kv_norm, w_kv, moba_w_q, moba_w_o, ffn_w_gu, ffn_w_down, moe_router, moe_w_gu, moe_w_down, final_norm):
    batch, seq, d = x.shape
    depth = attn_norm.shape[0]
    n_a = hgrn_w_in.shape[0]
    x = x.reshape(batch * seq, d)
    k = v_t = k_mean = None
    for layer in range(depth):
        if layer < n_a:
            proj = norm_matmul(x, attn_norm[layer], hgrn_w_in[layer].astype(BF16), F32)
            o = hgrn_recurrence(proj, hgrn_lb_logits, hgrn_out_norm[layer],
                                layer=layer, batch=batch, seq=seq)
            x = matmul_residual(x, o, hgrn_w_o[layer].astype(BF16))
        else:
            if layer == n_a:
                k, v_t, k_mean = shared_kv(x, kv_norm, w_kv[:, :d].astype(BF16),
                                           w_kv[:, d:].T.astype(BF16))
            j = layer - n_a
            q = norm_matmul(x, attn_norm[layer], moba_w_q[j].astype(BF16), F32)
            o = moba_attention(q, k, v_t, k_mean, batch=batch, seq=seq)
            x = matmul_residual(x, o, moba_w_o[j].astype(BF16))
        if layer % 2 == 0:
            j = layer // 2
            x = ffn(x, ffn_norm[layer], ffn_w_gu[j].astype(BF16), ffn_w_down[j].astype(BF16),
                    f_tile=_lane_tile(ffn_w_down.shape[1], 1536))
        else:
            j = layer // 2
            x = moe_layer(x, ffn_norm[layer], moe_router[j], moe_w_gu[j], moe_w_down[j])
    return rmsnorm(x, final_norm).reshape(batch, seq, d)
```

```python
import functools

import jax
import jax.numpy as jnp
from jax import lax
from jax.experimental import pallas as pl
from jax.experimental.pallas import tpu as pltpu

F32 = jnp.float32
BF16 = jnp.bfloat16

EPS = 1e-6
HEAD_DIM = 128
HGRN_CHUNK = 64
MOBA_BLOCK = 256
MOBA_TOPK = 3
MOE_TOPK = 2
MOBA_HEADS_PER_STEP = 2
LOG2E = 1.4426950408889634
LANES = 128
MXU_WIDTH = 256
FFN_DENSE_ROWS = 256
VMEM_LIMIT = 56 * 1024 * 1024

HGRN_TILE = 512
HGRN_HEADS_PER_STEP = 2
ROW_TILE = 512
TOK_TILE = 512
FFN_ROWS = 512
ROW_ALIGN = 16


def _params(*semantics):
    return pltpu.CompilerParams(dimension_semantics=semantics, vmem_limit_bytes=VMEM_LIMIT)


def _rms(x, gain):
    return x * lax.rsqrt(jnp.mean(x * x, axis=-1, keepdims=True) + EPS) * gain


def _dot(a, b):
    return jnp.dot(a, b, preferred_element_type=F32)


def _dot_nt(a, b):
    return lax.dot_general(a, b, (((1,), (1,)), ((), ())), preferred_element_type=F32)


def _dot_tn(a, b):
    return lax.dot_general(a, b, (((0,), (0,)), ((), ())), preferred_element_type=F32)


def _silu(x):
    half = 0.5 * x
    return half + half * jnp.tanh(half)


def _norm_matmul_kernel(x_ref, g_ref, w_ref, o_ref):
    h = _rms(x_ref[...], g_ref[...]).astype(BF16)
    o_ref[...] = _dot(h, w_ref[...]).astype(o_ref.dtype)


def norm_matmul(x, gain, w, out_dtype):
    t, d = x.shape
    n = w.shape[1]
    return pl.pallas_call(
        _norm_matmul_kernel,
        grid=(t // ROW_TILE,),
        in_specs=[
            pl.BlockSpec((ROW_TILE, d), lambda i: (i, 0)),
            pl.BlockSpec((1, d), lambda i: (0, 0)),
            pl.BlockSpec((d, n), lambda i: (0, 0)),
        ],
        out_specs=pl.BlockSpec((ROW_TILE, n), lambda i: (i, 0)),
        out_shape=jax.ShapeDtypeStruct((t, n), out_dtype),
        compiler_params=_params("parallel"),
    )(x, gain.reshape(1, d), w)


def _kv_kernel(x_ref, g_ref, wk_ref, wvt_ref, k_ref, vt_ref, km_ref):
    h = _rms(x_ref[...], g_ref[...]).astype(BF16)
    k = _dot(h, wk_ref[...])
    k_ref[...] = k.astype(k_ref.dtype)
    vt_ref[0] = _dot_nt(wvt_ref[...], h).astype(vt_ref.dtype)
    km_ref[0] = jnp.mean(k, axis=0, keepdims=True)


def shared_kv(x, gain, w_k, w_v_t):
    t, d = x.shape
    nblk = t // MOBA_BLOCK
    return pl.pallas_call(
        _kv_kernel,
        grid=(nblk,),
        in_specs=[
            pl.BlockSpec((MOBA_BLOCK, d), lambda i: (i, 0)),
            pl.BlockSpec((1, d), lambda i: (0, 0)),
            pl.BlockSpec((d, d), lambda i: (0, 0)),
            pl.BlockSpec((d, d), lambda i: (0, 0)),
        ],
        out_specs=[
            pl.BlockSpec((MOBA_BLOCK, d), lambda i: (i, 0)),
            pl.BlockSpec((1, d, MOBA_BLOCK), lambda i: (i, 0, 0)),
            pl.BlockSpec((1, 1, d), lambda i: (i, 0, 0)),
        ],
        out_shape=[
            jax.ShapeDtypeStruct((t, d), BF16),
            jax.ShapeDtypeStruct((nblk, d, MOBA_BLOCK), BF16),
            jax.ShapeDtypeStruct((nblk, 1, d), F32),
        ],
        compiler_params=_params("parallel"),
    )(x, gain.reshape(1, d), w_k, w_v_t)


def _hgrn_kernel(q_ref, f_ref, i_ref, g_ref, lbl_ref, gain_ref, o_ref, st_ref, *, layer):
    c = HGRN_CHUNK
    dk = HEAD_DIM
    heads = q_ref.shape[1] // dk
    per_head = q_ref.shape[0] // c
    n_chunks = heads * per_head
    group = MOBA_BLOCK // c

    def wide(ref):
        return jnp.concatenate([ref[n * c:(n + 1) * c, hh * dk:(hh + 1) * dk]
                                for hh in range(heads) for n in range(per_head)], axis=1)

    def tall(ref):
        return jnp.concatenate([ref[:, hh * dk:(hh + 1) * dk] for hh in range(heads)], axis=0)

    def chunk(x, n):
        return x[:, n * dk:(n + 1) * dk]

    def rows_of(x, n0, n1):
        return jnp.concatenate([chunk(x, n) for n in range(n0, n1)], axis=0)

    @pl.when(pl.program_id(2) == 0)
    def _():
        st_ref[...] = jnp.zeros_like(st_ref)

    fz = wide(f_ref)
    t = jnp.exp(-jnp.abs(fz))
    log_sig = jnp.minimum(fz, 0.0) - jnp.log(1.0 + t)
    sig_neg = jnp.exp(log_sig - fz)
    if layer > 0:
        logits = lbl_ref[...]
        e = jnp.exp(logits - jnp.max(logits, axis=0, keepdims=True))
        p = e / jnp.sum(e, axis=0, keepdims=True)
        lb = jnp.sum(p[1:layer + 1], axis=0, keepdims=True)
        lb = jnp.concatenate([chunk(lb, hh) for hh in range(heads) for _ in range(per_head)],
                             axis=1)
        log_lb = jnp.log(lb)
        y = jnp.log(1.0 - lb) + log_sig
        log_f = jnp.maximum(log_lb, y) + jnp.log(1.0 + jnp.exp(-jnp.abs(log_lb - y)))
        k = (1.0 - lb) * sig_neg
    else:
        log_f = log_sig
        k = sig_neg

    tril = (lax.broadcasted_iota(jnp.int32, (c, c), 1)
            <= lax.broadcasted_iota(jnp.int32, (c, c), 0)).astype(BF16)
    hi = log_f.astype(BF16)
    rest = log_f - hi.astype(F32)
    mid = rest.astype(BF16)
    lo = (rest - mid.astype(F32)).astype(BF16)
    b = _dot(tril, hi) + _dot(tril, mid) + _dot(tril, lo)
    b_mid = b[c // 2 - 1:c // 2, :]
    b_last = b[c - 1:c, :]

    q = _silu(wide(q_ref))
    qd = q * jnp.exp(b - b_mid)
    kd = k * jnp.exp(b_mid - b)
    qe = (qd * jnp.exp(b_mid)).astype(BF16)
    kl = (kd * jnp.exp(b_last - b_mid)).astype(BF16)
    qd = qd.astype(BF16)
    kd = kd.astype(BF16)
    v = wide(i_ref).astype(BF16)
    decay = jnp.exp(b_last)

    gr = group * c
    r_i = lax.broadcasted_iota(jnp.int32, (gr, gr), 0)
    c_i = lax.broadcasted_iota(jnp.int32, (gr, gr), 1)
    keep = (c_i <= r_i) & ((r_i // c) == (c_i // c))
    groups = range(0, n_chunks, group)
    scores = [_dot_nt(rows_of(qd, n0, n0 + group), rows_of(kd, n0, n0 + group)) for n0 in groups]

    grow = [_dot_tn(chunk(v, n), chunk(kl, n)) for n in range(n_chunks)]
    intra = [_dot(jnp.where(keep, a, 0.0).astype(BF16), rows_of(v, n0, n0 + group))
             for a, n0 in zip(scores, groups)]
    before = []
    for hh in range(heads):
        st = st_ref[hh]
        for n in range(hh * per_head, (hh + 1) * per_head):
            before.append(st.astype(BF16))
            st = chunk(decay, n) * st + grow[n]
        st_ref[hh] = st
    inter = [_dot_nt(chunk(qe, n), before[n]) for n in range(n_chunks)]
    o = jnp.concatenate(intra, axis=0) + jnp.concatenate(inter, axis=0)

    o = o * lax.rsqrt(jnp.mean(o * o, axis=-1, keepdims=True) + EPS)
    o = o * _silu(tall(g_ref))
    rows = q_ref.shape[0]
    for hh in range(heads):
        o_ref[:, hh * dk:(hh + 1) * dk] = (o[hh * rows:(hh + 1) * rows]
                                           * gain_ref[:, hh * dk:(hh + 1) * dk]).astype(o_ref.dtype)


def hgrn_recurrence(proj, lb_logits, out_gain, *, layer, batch, seq):
    t = proj.shape[0]
    d = proj.shape[1] // 4
    heads = d // HEAD_DIM
    tiles = seq // HGRN_TILE
    n_layers = lb_logits.shape[0]
    hp = HGRN_HEADS_PER_STEP
    width = hp * HEAD_DIM
    steps = heads // hp

    def act(part):
        return pl.BlockSpec((HGRN_TILE, width),
                            lambda b, h, s: (b * tiles + s, part * steps + h))

    return pl.pallas_call(
        functools.partial(_hgrn_kernel, layer=layer),
        grid=(batch, steps, tiles),
        in_specs=[
            act(0), act(1), act(2), act(3),
            pl.BlockSpec((n_layers, width), lambda b, h, s: (0, h)),
            pl.BlockSpec((1, width), lambda b, h, s: (0, h)),
        ],
        out_specs=pl.BlockSpec((HGRN_TILE, width), lambda b, h, s: (b * tiles + s, h)),
        out_shape=jax.ShapeDtypeStruct((t, d), BF16),
        scratch_shapes=[pltpu.VMEM((hp, HEAD_DIM, HEAD_DIM), F32)],
        compiler_params=_params("parallel", "parallel", "arbitrary"),
    )(proj, proj, proj, proj, lb_logits, out_gain.reshape(1, d))


def _moba_kernel(q_ref, k_ref, vt_ref, km_ref, slope_ref, o_ref):
    bs = MOBA_BLOCK
    dh = HEAD_DIM
    n_blk = km_ref.shape[1]
    heads = q_ref.shape[1] // dh
    scale2 = HEAD_DIM ** -0.5 * LOG2E
    rel = (lax.broadcasted_iota(jnp.int32, (bs, bs), 1)
           - lax.broadcasted_iota(jnp.int32, (bs, bs), 0))
    rel_f = rel.astype(F32)

    def gate_rank(q, km, n):
        gate = lax.dot_general(km, q, (((1,), (1,)), ((), ())),
                               precision=lax.Precision.HIGHEST, preferred_element_type=F32)
        blk = lax.broadcasted_iota(jnp.int32, gate.shape, 0)
        rank = jnp.zeros(gate.shape, F32)
        for m in range(n):
            g_m = gate[m:m + 1, :]
            rank = rank + ((g_m > gate) | ((g_m == gate) & (m < blk))).astype(F32)
        return rank

    def scores(n, hh):
        cols = slice(hh * dh, (hh + 1) * dh)
        q = q_ref[n * bs:(n + 1) * bs, cols]
        q16 = (q * scale2).astype(BF16)
        rank = gate_rank(q, km_ref[0][:, cols], n) if n > MOBA_TOPK else None
        return _dot_nt(k_ref[0:(n + 1) * bs, cols], q16), rank

    def probabilities(n, hh, s_all, rank):
        slope2 = slope_ref[hh][0:1, 0:1] * LOG2E
        bias = -slope2 * rel_f
        us, shifts, maxes = [], [], []
        for kb in range(n + 1):
            u = s_all[kb * bs:(kb + 1) * bs] + bias
            if kb == n:
                u = jnp.where(rel >= 0, u, -jnp.inf)
            elif rank is not None:
                u = jnp.where(rank[kb:kb + 1, :] < MOBA_TOPK, u, -jnp.inf)
            shift = slope2 * float((n - kb) * bs)
            us.append(u)
            shifts.append(shift)
            maxes.append(jnp.max(u, axis=0, keepdims=True) - shift)
        m = functools.reduce(jnp.maximum, maxes)
        ps = [jnp.exp2(u - (m + shift)) for u, shift in zip(us, shifts)]
        return jnp.concatenate(ps, axis=0).astype(BF16)

    def output(n, hh, p_all):
        cols = slice(hh * dh, (hh + 1) * dh)
        vt = jnp.concatenate([vt_ref[kb, cols, :] for kb in range(n + 1)], axis=1)
        ones = jnp.ones((ROW_ALIGN, vt.shape[1]), BF16)
        pv = _dot(jnp.concatenate([vt, ones], axis=0), p_all)
        o_ref[n * bs:(n + 1) * bs, cols] = (pv[0:dh] / pv[dh:dh + 1]).T.astype(o_ref.dtype)

    scored = [scores(0, hh) for hh in range(heads)]
    for n in range(n_blk):
        probs = [probabilities(n, hh, *scored[hh]) for hh in range(heads)]
        if n + 1 < n_blk:
            scored = [scores(n + 1, hh) for hh in range(heads)]
        for hh in range(heads):
            output(n, hh, probs[hh])


def moba_attention(q, k, v_t, k_mean, *, batch, seq):
    t, d = q.shape
    heads = d // HEAD_DIM
    n_blk = seq // MOBA_BLOCK
    hp = MOBA_HEADS_PER_STEP
    width = hp * HEAD_DIM
    slopes = (2.0 ** (-8.0 * jnp.arange(1, heads + 1, dtype=F32) / heads))
    slopes = jnp.broadcast_to(slopes[:, None, None], (heads, 1, LANES))
    return pl.pallas_call(
        _moba_kernel,
        grid=(batch, heads // hp),
        in_specs=[
            pl.BlockSpec((seq, width), lambda b, h: (b, h)),
            pl.BlockSpec((seq, width), lambda b, h: (b, h)),
            pl.BlockSpec((n_blk, width, MOBA_BLOCK), lambda b, h: (b, h, 0)),
            pl.BlockSpec((1, n_blk, width), lambda b, h: (b, 0, h)),
            pl.BlockSpec((hp, 1, LANES), lambda b, h: (h, 0, 0)),
        ],
        out_specs=pl.BlockSpec((seq, width), lambda b, h: (b, h)),
        out_shape=jax.ShapeDtypeStruct((t, d), BF16),
        compiler_params=_params("parallel", "parallel"),
    )(q, k, v_t, k_mean.reshape(batch, n_blk, d), slopes)


def _router_kernel(x_ref, a_ref, wo_ref, g_ref, w_ref, x1_ref, h_ref, info_ref, info_t_ref,
                   cnt_ref, *, n_experts):
    x1 = x_ref[...] + _dot(a_ref[...], wo_ref[...])
    x1_ref[...] = x1
    h = _rms(x1, g_ref[...])
    h_ref[...] = h.astype(BF16)
    logits = jnp.dot(h, w_ref[...], precision=lax.Precision.HIGHEST, preferred_element_type=F32)
    tm = logits.shape[0]
    lane = lax.broadcasted_iota(jnp.int32, logits.shape, 1)
    neg = -jnp.inf
    lg = jnp.where(lane < n_experts, logits, neg)
    m1 = jnp.max(lg, axis=1, keepdims=True)
    i1 = jnp.min(jnp.where(lg == m1, lane, LANES), axis=1, keepdims=True)
    lg2 = jnp.where(lane == i1, neg, lg)
    m2 = jnp.max(lg2, axis=1, keepdims=True)
    i2 = jnp.min(jnp.where(lg2 == m2, lane, LANES), axis=1, keepdims=True)
    e2 = jnp.exp(m2 - m1)
    w1 = 1.0 / (1.0 + e2)
    w2 = e2 / (1.0 + e2)

    routed = ((lane == i1) | (lane == i2)).astype(F32)
    cnt = jnp.sum(routed, axis=0, keepdims=True)
    cnt_ref[0] = cnt.astype(jnp.int32)
    earlier = (lax.broadcasted_iota(jnp.int32, (tm, tm), 1)
               < lax.broadcasted_iota(jnp.int32, (tm, tm), 0)).astype(BF16)
    rank = _dot(earlier, routed.astype(BF16))
    group_rows = jnp.floor((cnt + (ROW_ALIGN - 1)) * (1.0 / ROW_ALIGN)) * ROW_ALIGN
    lower = (lax.broadcasted_iota(jnp.int32, (LANES, LANES), 0)
             < lax.broadcasted_iota(jnp.int32, (LANES, LANES), 1)).astype(BF16)
    group_start = _dot(jnp.broadcast_to(group_rows, (8, LANES)).astype(BF16), lower)[0:1, :]
    slab_row = group_start + rank
    first = i1 < i2
    i_lo = jnp.where(first, i1, i2)
    i_hi = jnp.where(first, i2, i1)
    row_lo = jnp.sum(jnp.where(lane == i_lo, slab_row, 0.0), axis=1, keepdims=True)
    row_hi = jnp.sum(jnp.where(lane == i_hi, slab_row, 0.0), axis=1, keepdims=True)
    info = jnp.where(lane == 0, row_lo, 0.0) + jnp.where(lane == 1, row_hi, 0.0)
    info_t_ref[0] = info.T[0:8, :]
    info = info + jnp.where(lane == 2, jnp.where(first, w1, w2), 0.0)
    info_ref[...] = info + jnp.where(lane == 3, jnp.where(first, w2, w1), 0.0)


def router(x, a, w_o, gain, w_router):
    t, d = x.shape
    n_experts = w_router.shape[1]
    n_tiles = t // TOK_TILE
    w = jnp.pad(w_router, ((0, 0), (0, LANES - n_experts)))
    return pl.pallas_call(
        functools.partial(_router_kernel, n_experts=n_experts),
        grid=(n_tiles,),
        in_specs=[
            pl.BlockSpec((TOK_TILE, d), lambda i: (i, 0)),
            pl.BlockSpec((TOK_TILE, d), lambda i: (i, 0)),
            pl.BlockSpec((d, d), lambda i: (0, 0)),
            pl.BlockSpec((1, d), lambda i: (0, 0)),
            pl.BlockSpec((d, LANES), lambda i: (0, 0)),
        ],
        out_specs=[
            pl.BlockSpec((TOK_TILE, d), lambda i: (i, 0)),
            pl.BlockSpec((TOK_TILE, d), lambda i: (i, 0)),
            pl.BlockSpec((TOK_TILE, LANES), lambda i: (i, 0)),
            pl.BlockSpec((1, 8, TOK_TILE), lambda i: (i, 0, 0)),
            pl.BlockSpec((1, 1, LANES), lambda i: (i, 0, 0)),
        ],
        out_shape=[
            jax.ShapeDtypeStruct((t, d), F32),
            jax.ShapeDtypeStruct((t, d), BF16),
            jax.ShapeDtypeStruct((t, LANES), F32),
            jax.ShapeDtypeStruct((n_tiles, 8, TOK_TILE), F32),
            jax.ShapeDtypeStruct((n_tiles, 1, LANES), jnp.int32),
        ],
        compiler_params=_params("parallel"),
    )(x, a, w_o, gain.reshape(1, d), w)


def _slab_rows(n_experts):
    bound = MOE_TOPK * TOK_TILE + n_experts * (ROW_ALIGN - 1)
    return -(-bound // LANES) * LANES


def _group_copies(n_ref, rs_ref, tile, n_experts, make_copy, wait):
    slab_row = jnp.int32(0)
    for e in range(n_experts):
        n = n_ref[tile * n_experts + e]
        base = rs_ref[tile * n_experts + e]
        off = jnp.int32(0)
        size = TOK_TILE
        while size >= ROW_ALIGN:
            @pl.when((n & size) != 0)
            def _(off=off, size=size, slab_row=slab_row, base=base):
                cp = make_copy(pl.multiple_of(slab_row + off, ROW_ALIGN),
                               pl.multiple_of(base + off, ROW_ALIGN), size)
                if wait:
                    cp.wait()
                else:
                    cp.start()
            off = off + (n & size)
            size //= 2
        slab_row = slab_row + n


def _dispatch_kernel(rs_ref, n_ref, h_ref, info_t_ref, _, xs_ref, slab_ref, sem, *, n_experts):
    i = pl.program_id(0)
    rows, tm = slab_ref.shape[0], h_ref.shape[0]
    info_t = info_t_ref[0]
    r = lax.broadcasted_iota(jnp.int32, (rows, tm), 0).astype(F32)
    onehot = ((r == info_t[0:1, :]) | (r == info_t[1:2, :])).astype(BF16)
    slab_ref[...] = _dot(onehot, h_ref[...]).astype(BF16)

    def make_copy(slab_row, buffer_row, size):
        return pltpu.make_async_copy(slab_ref.at[pl.ds(slab_row, size)],
                                     xs_ref.at[pl.ds(buffer_row, size)], sem)

    _group_copies(n_ref, rs_ref, i, n_experts, make_copy, wait=False)
    _group_copies(n_ref, rs_ref, i, n_experts, make_copy, wait=True)


def dispatch(h, info_t, row_start, n_rows, *, total_rows, n_experts):
    t, d = h.shape
    n_tiles = t // TOK_TILE
    return pl.pallas_call(
        functools.partial(_dispatch_kernel, n_experts=n_experts),
        grid_spec=pltpu.PrefetchScalarGridSpec(
            num_scalar_prefetch=2,
            grid=(n_tiles,),
            in_specs=[
                pl.BlockSpec((TOK_TILE, d), lambda i, rs, n: (i, 0)),
                pl.BlockSpec((1, 8, TOK_TILE), lambda i, rs, n: (i, 0, 0)),
                pl.BlockSpec(memory_space=pl.ANY),
            ],
            out_specs=pl.BlockSpec(memory_space=pl.ANY),
            scratch_shapes=[pltpu.VMEM((_slab_rows(n_experts), d), BF16),
                            pltpu.SemaphoreType.DMA],
        ),
        out_shape=jax.ShapeDtypeStruct((total_rows, d), BF16),
        input_output_aliases={4: 0},
        compiler_params=_params("arbitrary"),
    )(row_start, n_rows, h, info_t, jnp.zeros((total_rows, d), BF16))


def _grouped_ffn_kernel(te_ref, na_ref, x_ref, wg_ref, wu_ref, wd_ref, o_ref, acc_ref):
    j = pl.program_id(0)
    f = pl.program_id(1)
    last_f = pl.num_programs(1) - 1
    active = j < na_ref[0]

    @pl.when(active)
    def _():
        @pl.when(f == 0)
        def _():
            acc_ref[...] = jnp.zeros_like(acc_ref)

        x = x_ref[...]
        act = _silu(_dot(x, wg_ref[...])) * _dot(x, wu_ref[...])
        acc_ref[...] += _dot(act.astype(BF16), wd_ref[...])

        @pl.when(f == last_f)
        def _():
            o_ref[...] = acc_ref[...].astype(o_ref.dtype)

    @pl.when(jnp.logical_not(active) & (f == last_f))
    def _():
        o_ref[...] = jnp.zeros_like(o_ref)


def grouped_ffn(xs, w_gu, w_down, tile_expert, n_active, *, f_tile):
    rows, d = xs.shape
    ff = w_down.shape[1]
    nf = ff // f_tile

    def f_idx(j, f, na):
        return jnp.where(j < na[0], f, nf - 1)

    return pl.pallas_call(
        _grouped_ffn_kernel,
        grid_spec=pltpu.PrefetchScalarGridSpec(
            num_scalar_prefetch=2,
            grid=(rows // FFN_ROWS, nf),
            in_specs=[
                pl.BlockSpec((FFN_ROWS, d), lambda j, f, te, na: (jnp.minimum(j, na[0] - 1), 0)),
                pl.BlockSpec((None, d, f_tile), lambda j, f, te, na: (te[j], 0, f_idx(j, f, na))),
                pl.BlockSpec((None, d, f_tile), lambda j, f, te, na: (te[j], 0, nf + f_idx(j, f, na))),
                pl.BlockSpec((None, f_tile, d), lambda j, f, te, na: (te[j], f_idx(j, f, na), 0)),
            ],
            out_specs=pl.BlockSpec((FFN_ROWS, d), lambda j, f, te, na: (j, 0)),
            scratch_shapes=[pltpu.VMEM((FFN_ROWS, d), F32)],
        ),
        out_shape=jax.ShapeDtypeStruct((rows, d), BF16),
        compiler_params=_params("arbitrary", "arbitrary"),
    )(tile_expert, n_active, xs, w_gu, w_gu, w_down)


def _combine_kernel(rs_ref, n_ref, x_ref, info_ref, fg_ref, y_ref, o_ref, ybuf_ref, sem,
                    *, n_experts, final_norm):
    i = pl.program_id(0)
    slot = i % 2
    rows, tm = ybuf_ref.shape[1], x_ref.shape[0]

    def copies(tile, slot, wait):
        def make_copy(slab_row, buffer_row, size):
            return pltpu.make_async_copy(y_ref.at[pl.ds(buffer_row, size)],
                                         ybuf_ref.at[slot, pl.ds(slab_row, size)], sem.at[slot])
        _group_copies(n_ref, rs_ref, tile, n_experts, make_copy, wait)

    @pl.when(i == 0)
    def _():
        ybuf_ref[...] = jnp.zeros_like(ybuf_ref)
        copies(i, slot, wait=False)

    @pl.when(i + 1 < pl.num_programs(0))
    def _():
        copies(i + 1, 1 - slot, wait=False)

    copies(i, slot, wait=True)
    y = ybuf_ref[slot]
    info = info_ref[...]
    r = lax.broadcasted_iota(jnp.int32, (tm, rows), 1).astype(F32)
    pick_lo = (r == info[:, 0:1]).astype(BF16)
    pick_hi = (r == info[:, 1:2]).astype(BF16)
    out = x_ref[...] + info[:, 2:3] * _dot(pick_lo, y) + info[:, 3:4] * _dot(pick_hi, y)
    o_ref[...] = _rms(out, fg_ref[...]) if final_norm else out


def combine(x, info, y, row_start, n_rows, final_gain, *, n_experts, final_norm):
    t, d = x.shape
    n_tiles = t // TOK_TILE
    return pl.pallas_call(
        functools.partial(_combine_kernel, n_experts=n_experts, final_norm=final_norm),
        grid_spec=pltpu.PrefetchScalarGridSpec(
            num_scalar_prefetch=2,
            grid=(n_tiles,),
            in_specs=[
                pl.BlockSpec((TOK_TILE, d), lambda i, rs, n: (i, 0)),
                pl.BlockSpec((TOK_TILE, LANES), lambda i, rs, n: (i, 0)),
                pl.BlockSpec((1, d), lambda i, rs, n: (0, 0)),
                pl.BlockSpec(memory_space=pl.ANY),
            ],
            out_specs=pl.BlockSpec((TOK_TILE, d), lambda i, rs, n: (i, 0)),
            scratch_shapes=[pltpu.VMEM((2, _slab_rows(n_experts), d), BF16),
                            pltpu.SemaphoreType.DMA((2,))],
        ),
        out_shape=jax.ShapeDtypeStruct((t, d), F32),
        compiler_params=_params("arbitrary"),
    )(row_start, n_rows, x, info, final_gain.reshape(1, d), y)


def _ffn_kernel(x_ref, a_ref, wo_ref, g_ref, wg_ref, wu_ref, wd_ref, o_ref):
    x1 = x_ref[...] + _dot(a_ref[...], wo_ref[...])
    h = _rms(x1, g_ref[...]).astype(BF16)
    act = _silu(_dot(h, wg_ref[...])) * _dot(h, wu_ref[...])
    o_ref[...] = x1 + _dot(act.astype(BF16), wd_ref[...])


def ffn(x, a, w_o, gain, w_gu, w_down):
    t, d = x.shape
    ff = w_down.shape[0]
    once = pl.Buffered(1)
    rows = pl.BlockSpec((FFN_DENSE_ROWS, d), lambda i: (i, 0))
    return pl.pallas_call(
        _ffn_kernel,
        grid=(t // FFN_DENSE_ROWS,),
        in_specs=[
            rows,
            rows,
            pl.BlockSpec((d, d), lambda i: (0, 0), pipeline_mode=once),
            pl.BlockSpec((1, d), lambda i: (0, 0)),
            pl.BlockSpec((d, ff), lambda i: (0, 0), pipeline_mode=once),
            pl.BlockSpec((d, ff), lambda i: (0, 1), pipeline_mode=once),
            pl.BlockSpec((ff, d), lambda i: (0, 0), pipeline_mode=once),
        ],
        out_specs=rows,
        out_shape=jax.ShapeDtypeStruct((t, d), F32),
        compiler_params=_params("parallel"),
    )(x, a, w_o, gain.reshape(1, d), w_gu, w_gu, w_down)


def _rmsnorm_kernel(x_ref, g_ref, o_ref):
    o_ref[...] = _rms(x_ref[...], g_ref[...])


def rmsnorm(x, gain):
    t, d = x.shape
    return pl.pallas_call(
        _rmsnorm_kernel,
        grid=(t // ROW_TILE,),
        in_specs=[pl.BlockSpec((ROW_TILE, d), lambda i: (i, 0)),
                  pl.BlockSpec((1, d), lambda i: (0, 0))],
        out_specs=pl.BlockSpec((ROW_TILE, d), lambda i: (i, 0)),
        out_shape=jax.ShapeDtypeStruct((t, d), F32),
        compiler_params=_params("parallel"),
    )(x, gain.reshape(1, d))


def _mxu_tile(n, target):
    best = None
    for cand in range(MXU_WIDTH, target + 1, MXU_WIDTH):
        if n % cand == 0:
            best = cand
    assert best is not None, (n, target)
    return best


def moe_layer(x, a, w_o, gain, w_router, w_gu, w_down, final_gain, *, final_norm):
    t, d = x.shape
    n_experts = w_router.shape[1]
    n_tiles = t // TOK_TILE
    x, h, info, info_t, cnt = router(x, a, w_o, gain, w_router)
    cnt = cnt.reshape(n_tiles, LANES)[:, :n_experts]
    n_rows = (cnt + ROW_ALIGN - 1) // ROW_ALIGN * ROW_ALIGN
    seg_rows = jnp.sum(n_rows, axis=0)
    seg_cap = (seg_rows + FFN_ROWS - 1) // FFN_ROWS * FFN_ROWS
    seg_end = jnp.cumsum(seg_cap)
    row_start = (seg_end - seg_cap)[None, :] + jnp.cumsum(n_rows, axis=0) - n_rows
    max_rows = (MOE_TOPK * t + n_tiles * n_experts * (ROW_ALIGN - 1)
                + n_experts * (FFN_ROWS - ROW_ALIGN))
    total_tiles = -(-max_rows // FFN_ROWS)
    n_active = (seg_end[-1:] // FFN_ROWS).astype(jnp.int32)
    tile_expert = jnp.minimum(
        jnp.searchsorted(seg_end, jnp.arange(total_tiles, dtype=jnp.int32) * FFN_ROWS, side="right"),
        n_experts - 1).astype(jnp.int32)
    row_start = row_start.reshape(-1).astype(jnp.int32)
    n_rows = n_rows.reshape(-1).astype(jnp.int32)
    xs = dispatch(h, info_t, row_start, n_rows,
                  total_rows=total_tiles * FFN_ROWS, n_experts=n_experts)
    y = grouped_ffn(xs, w_gu.astype(BF16), w_down.astype(BF16), tile_expert, n_active,
                    f_tile=_mxu_tile(w_down.shape[1], 2048))
    return combine(x, info, y, row_start, n_rows, final_gain,
                   n_experts=n_experts, final_norm=final_norm)


def kernel(x, attn_norm, ffn_norm, hgrn_w_in, hgrn_lb_logits, hgrn_out_norm, hgrn_w_o, kv_norm, w_kv, moba_w_q, moba_w_o, ffn_w_gu, ffn_w_down, moe_router, moe_w_gu, moe_w_down, final_norm):
    batch, seq, d = x.shape
    depth = attn_norm.shape[0]
    n_a = hgrn_w_in.shape[0]
    x = x.reshape(batch * seq, d)
    k = v_t = k_mean = None
    for layer in range(depth):
        if layer < n_a:
            proj = norm_matmul(x, attn_norm[layer], hgrn_w_in[layer].astype(BF16), F32)
            o = hgrn_recurrence(proj, hgrn_lb_logits, hgrn_out_norm[layer],
                                layer=layer, batch=batch, seq=seq)
            w_o = hgrn_w_o[layer].astype(BF16)
        else:
            if layer == n_a:
                k, v_t, k_mean = shared_kv(x, kv_norm, w_kv[:, :d].astype(BF16),
                                           w_kv[:, d:].T.astype(BF16))
            j = layer - n_a
            q = norm_matmul(x, attn_norm[layer], moba_w_q[j].astype(BF16), F32)
            o = moba_attention(q, k, v_t, k_mean, batch=batch, seq=seq)
            w_o = moba_w_o[j].astype(BF16)
        j = layer // 2
        if layer % 2 == 0:
            x = ffn(x, o, w_o, ffn_norm[layer], ffn_w_gu[j].astype(BF16),
                    ffn_w_down[j].astype(BF16))
        else:
            last = layer == depth - 1
            x = moe_layer(x, o, w_o, ffn_norm[layer], moe_router[j], moe_w_gu[j], moe_w_down[j],
                          final_norm, final_norm=last)
    if depth % 2 == 1:
        x = rmsnorm(x, final_norm)
    return x.reshape(batch, seq, d)
```

```python
import functools

import jax
import jax.numpy as jnp
from jax import lax
from jax.experimental import pallas as pl
from jax.experimental.pallas import tpu as pltpu

F32 = jnp.float32
BF16 = jnp.bfloat16

EPS = 1e-6
HEAD_DIM = 128
HGRN_CHUNK = 64
MOBA_BLOCK = 256
MOBA_TOPK = 3
MOE_TOPK = 2
MOBA_HEADS_PER_STEP = 2
LOG2E = 1.4426950408889634
LANES = 128
MXU_WIDTH = 256
FFN_DENSE_ROWS = 256
VMEM_LIMIT = 56 * 1024 * 1024

HGRN_TILE = 512
HGRN_HEADS_PER_STEP = 2
ROW_TILE = 512
TOK_TILE = 512
FFN_ROWS = 512
ROW_ALIGN = 16


def _params(*semantics):
    return pltpu.CompilerParams(dimension_semantics=semantics, vmem_limit_bytes=VMEM_LIMIT)


def _rms(x, gain):
    return x * lax.rsqrt(jnp.mean(x * x, axis=-1, keepdims=True) + EPS) * gain


def _dot(a, b):
    return jnp.dot(a, b, preferred_element_type=F32)


def _dot_nt(a, b):
    return lax.dot_general(a, b, (((1,), (1,)), ((), ())), preferred_element_type=F32)


def _dot_tn(a, b):
    return lax.dot_general(a, b, (((0,), (0,)), ((), ())), preferred_element_type=F32)


def _silu(x):
    half = 0.5 * x
    return half + half * jnp.tanh(half)


def _norm_matmul_kernel(x_ref, g_ref, w_ref, o_ref):
    h = _rms(x_ref[...], g_ref[...]).astype(BF16)
    o_ref[...] = _dot(h, w_ref[...]).astype(o_ref.dtype)


def norm_matmul(x, gain, w, out_dtype):
    t, d = x.shape
    n = w.shape[1]
    return pl.pallas_call(
        _norm_matmul_kernel,
        grid=(t // ROW_TILE,),
        in_specs=[
            pl.BlockSpec((ROW_TILE, d), lambda i: (i, 0)),
            pl.BlockSpec((1, d), lambda i: (0, 0)),
            pl.BlockSpec((d, n), lambda i: (0, 0)),
        ],
        out_specs=pl.BlockSpec((ROW_TILE, n), lambda i: (i, 0)),
        out_shape=jax.ShapeDtypeStruct((t, n), out_dtype),
        compiler_params=_params("parallel"),
    )(x, gain.reshape(1, d), w)


def _kv_kernel(x_ref, g_ref, wk_ref, wvt_ref, k_ref, vt_ref, km_ref):
    h = _rms(x_ref[...], g_ref[...]).astype(BF16)
    k = _dot(h, wk_ref[...])
    k_ref[...] = k.astype(k_ref.dtype)
    vt_ref[0] = _dot_nt(wvt_ref[...], h).astype(vt_ref.dtype)
    km_ref[0] = jnp.mean(k, axis=0, keepdims=True)


def shared_kv(x, gain, w_k, w_v_t):
    t, d = x.shape
    nblk = t // MOBA_BLOCK
    return pl.pallas_call(
        _kv_kernel,
        grid=(nblk,),
        in_specs=[
            pl.BlockSpec((MOBA_BLOCK, d), lambda i: (i, 0)),
            pl.BlockSpec((1, d), lambda i: (0, 0)),
            pl.BlockSpec((d, d), lambda i: (0, 0)),
            pl.BlockSpec((d, d), lambda i: (0, 0)),
        ],
        out_specs=[
            pl.BlockSpec((MOBA_BLOCK, d), lambda i: (i, 0)),
            pl.BlockSpec((1, d, MOBA_BLOCK), lambda i: (i, 0, 0)),
            pl.BlockSpec((1, 1, d), lambda i: (i, 0, 0)),
        ],
        out_shape=[
            jax.ShapeDtypeStruct((t, d), BF16),
            jax.ShapeDtypeStruct((nblk, d, MOBA_BLOCK), BF16),
            jax.ShapeDtypeStruct((nblk, 1, d), F32),
        ],
        compiler_params=_params("parallel"),
    )(x, gain.reshape(1, d), w_k, w_v_t)


def _hgrn_kernel(q_ref, f_ref, i_ref, g_ref, lbl_ref, gain_ref, o_ref, st_ref, *, layer):
    c = HGRN_CHUNK
    dk = HEAD_DIM
    heads = q_ref.shape[1] // dk
    per_head = q_ref.shape[0] // c
    n_chunks = heads * per_head
    group = MOBA_BLOCK // c

    def wide(ref):
        return jnp.concatenate([ref[n * c:(n + 1) * c, hh * dk:(hh + 1) * dk]
                                for hh in range(heads) for n in range(per_head)], axis=1)

    def tall(ref):
        return jnp.concatenate([ref[:, hh * dk:(hh + 1) * dk] for hh in range(heads)], axis=0)

    def chunk(x, n):
        return x[:, n * dk:(n + 1) * dk]

    def rows_of(x, n0, n1):
        return jnp.concatenate([chunk(x, n) for n in range(n0, n1)], axis=0)

    @pl.when(pl.program_id(2) == 0)
    def _():
        st_ref[...] = jnp.zeros_like(st_ref)

    fz = wide(f_ref)
    t = jnp.exp(-jnp.abs(fz))
    log_sig = jnp.minimum(fz, 0.0) - jnp.log(1.0 + t)
    sig_neg = jnp.exp(log_sig - fz)
    if layer > 0:
        logits = lbl_ref[...]
        e = jnp.exp(logits - jnp.max(logits, axis=0, keepdims=True))
        p = e / jnp.sum(e, axis=0, keepdims=True)
        lb = jnp.sum(p[1:layer + 1], axis=0, keepdims=True)
        lb = jnp.concatenate([chunk(lb, hh) for hh in range(heads) for _ in range(per_head)],
                             axis=1)
        log_lb = jnp.log(lb)
        y = jnp.log(1.0 - lb) + log_sig
        log_f = jnp.maximum(log_lb, y) + jnp.log(1.0 + jnp.exp(-jnp.abs(log_lb - y)))
        k = (1.0 - lb) * sig_neg
    else:
        log_f = log_sig
        k = sig_neg

    tril = (lax.broadcasted_iota(jnp.int32, (c, c), 1)
            <= lax.broadcasted_iota(jnp.int32, (c, c), 0)).astype(BF16)
    hi = log_f.astype(BF16)
    rest = log_f - hi.astype(F32)
    mid = rest.astype(BF16)
    lo = (rest - mid.astype(F32)).astype(BF16)
    b = _dot(tril, hi) + _dot(tril, mid) + _dot(tril, lo)
    b_mid = b[c // 2 - 1:c // 2, :]
    b_last = b[c - 1:c, :]

    q = _silu(wide(q_ref))
    qd = q * jnp.exp(b - b_mid)
    kd = k * jnp.exp(b_mid - b)
    qe = (qd * jnp.exp(b_mid)).astype(BF16)
    kl = (kd * jnp.exp(b_last - b_mid)).astype(BF16)
    qd = qd.astype(BF16)
    kd = kd.astype(BF16)
    v = wide(i_ref).astype(BF16)
    decay = jnp.exp(b_last)

    gr = group * c
    r_i = lax.broadcasted_iota(jnp.int32, (gr, gr), 0)
    c_i = lax.broadcasted_iota(jnp.int32, (gr, gr), 1)
    keep = (c_i <= r_i) & ((r_i // c) == (c_i // c))
    groups = range(0, n_chunks, group)
    scores = [_dot_nt(rows_of(qd, n0, n0 + group), rows_of(kd, n0, n0 + group)) for n0 in groups]

    grow = [_dot_tn(chunk(v, n), chunk(kl, n)) for n in range(n_chunks)]
    intra = [_dot(jnp.where(keep, a, 0.0).astype(BF16), rows_of(v, n0, n0 + group))
             for a, n0 in zip(scores, groups)]
    before = []
    for hh in range(heads):
        st = st_ref[hh]
        for n in range(hh * per_head, (hh + 1) * per_head):
            before.append(st.astype(BF16))
            st = chunk(decay, n) * st + grow[n]
        st_ref[hh] = st
    inter = [_dot_nt(chunk(qe, n), before[n]) for n in range(n_chunks)]
    o = jnp.concatenate(intra, axis=0) + jnp.concatenate(inter, axis=0)

    o = o * lax.rsqrt(jnp.mean(o * o, axis=-1, keepdims=True) + EPS)
    o = o * _silu(tall(g_ref))
    rows = q_ref.shape[0]
    for hh in range(heads):
        o_ref[:, hh * dk:(hh + 1) * dk] = (o[hh * rows:(hh + 1) * rows]
                                           * gain_ref[:, hh * dk:(hh + 1) * dk]).astype(o_ref.dtype)


def hgrn_recurrence(proj, lb_logits, out_gain, *, layer, batch, seq):
    t = proj.shape[0]
    d = proj.shape[1] // 4
    heads = d // HEAD_DIM
    tiles = seq // HGRN_TILE
    n_layers = lb_logits.shape[0]
    hp = HGRN_HEADS_PER_STEP
    width = hp * HEAD_DIM
    steps = heads // hp

    def act(part):
        return pl.BlockSpec((HGRN_TILE, width),
                            lambda b, h, s: (b * tiles + s, part * steps + h))

    return pl.pallas_call(
        functools.partial(_hgrn_kernel, layer=layer),
        grid=(batch, steps, tiles),
        in_specs=[
            act(0), act(1), act(2), act(3),
            pl.BlockSpec((n_layers, width), lambda b, h, s: (0, h)),
            pl.BlockSpec((1, width), lambda b, h, s: (0, h)),
        ],
        out_specs=pl.BlockSpec((HGRN_TILE, width), lambda b, h, s: (b * tiles + s, h)),
        out_shape=jax.ShapeDtypeStruct((t, d), BF16),
        scratch_shapes=[pltpu.VMEM((hp, HEAD_DIM, HEAD_DIM), F32)],
        compiler_params=_params("parallel", "parallel", "arbitrary"),
    )(proj, proj, proj, proj, lb_logits, out_gain.reshape(1, d))


def _moba_kernel(q_ref, k_ref, vt_ref, km_ref, slope_ref, o_ref):
    bs = MOBA_BLOCK
    dh = HEAD_DIM
    n_blk = km_ref.shape[1]
    heads = q_ref.shape[1] // dh
    scale2 = HEAD_DIM ** -0.5 * LOG2E
    rel = (lax.broadcasted_iota(jnp.int32, (bs, bs), 1)
           - lax.broadcasted_iota(jnp.int32, (bs, bs), 0))
    rel_f = rel.astype(F32)

    def gate_rank(q, km, n):
        gate = lax.dot_general(km, q, (((1,), (1,)), ((), ())),
                               precision=lax.Precision.HIGHEST, preferred_element_type=F32)
        blk = lax.broadcasted_iota(jnp.int32, gate.shape, 0)
        rank = jnp.zeros(gate.shape, F32)
        for m in range(n):
            g_m = gate[m:m + 1, :]
            rank = rank + ((g_m > gate) | ((g_m == gate) & (m < blk))).astype(F32)
        return rank

    def scores(n, hh):
        cols = slice(hh * dh, (hh + 1) * dh)
        q = q_ref[n * bs:(n + 1) * bs, cols]
        q16 = (q * scale2).astype(BF16)
        rank = gate_rank(q, km_ref[0][:, cols], n) if n > MOBA_TOPK else None
        return _dot_nt(k_ref[0:(n + 1) * bs, cols], q16), rank

    def probabilities(n, hh, s_all, rank):
        slope2 = slope_ref[hh][0:1, 0:1] * LOG2E
        bias = -slope2 * rel_f
        us, shifts, maxes = [], [], []
        for kb in range(n + 1):
            u = s_all[kb * bs:(kb + 1) * bs] + bias
            if kb == n:
                u = jnp.where(rel >= 0, u, -jnp.inf)
            elif rank is not None:
                u = jnp.where(rank[kb:kb + 1, :] < MOBA_TOPK, u, -jnp.inf)
            shift = slope2 * float((n - kb) * bs)
            us.append(u)
            shifts.append(shift)
            maxes.append(jnp.max(u, axis=0, keepdims=True) - shift)
        m = functools.reduce(jnp.maximum, maxes)
        ps = [jnp.exp2(u - (m + shift)) for u, shift in zip(us, shifts)]
        return jnp.concatenate(ps, axis=0).astype(BF16)

    def output(n, hh, p_all):
        cols = slice(hh * dh, (hh + 1) * dh)
        vt = jnp.concatenate([vt_ref[kb, cols, :] for kb in range(n + 1)], axis=1)
        ones = jnp.ones((ROW_ALIGN, vt.shape[1]), BF16)
        pv = _dot(jnp.concatenate([vt, ones], axis=0), p_all)
        o_ref[n * bs:(n + 1) * bs, cols] = (pv[0:dh] / pv[dh:dh + 1]).T.astype(o_ref.dtype)

    scored = [scores(0, hh) for hh in range(heads)]
    for n in range(n_blk):
        probs = [probabilities(n, hh, *scored[hh]) for hh in range(heads)]
        if n + 1 < n_blk:
            scored = [scores(n + 1, hh) for hh in range(heads)]
        for hh in range(heads):
            output(n, hh, probs[hh])


def moba_attention(q, k, v_t, k_mean, *, batch, seq):
    t, d = q.shape
    heads = d // HEAD_DIM
    n_blk = seq // MOBA_BLOCK
    hp = MOBA_HEADS_PER_STEP
    width = hp * HEAD_DIM
    slopes = (2.0 ** (-8.0 * jnp.arange(1, heads + 1, dtype=F32) / heads))
    slopes = jnp.broadcast_to(slopes[:, None, None], (heads, 1, LANES))
    return pl.pallas_call(
        _moba_kernel,
        grid=(batch, heads // hp),
        in_specs=[
            pl.BlockSpec((seq, width), lambda b, h: (b, h)),
            pl.BlockSpec((seq, width), lambda b, h: (b, h)),
            pl.BlockSpec((n_blk, width, MOBA_BLOCK), lambda b, h: (b, h, 0)),
            pl.BlockSpec((1, n_blk, width), lambda b, h: (b, 0, h)),
            pl.BlockSpec((hp, 1, LANES), lambda b, h: (h, 0, 0)),
        ],
        out_specs=pl.BlockSpec((seq, width), lambda b, h: (b, h)),
        out_shape=jax.ShapeDtypeStruct((t, d), BF16),
        compiler_params=_params("parallel", "parallel"),
    )(q, k, v_t, k_mean.reshape(batch, n_blk, d), slopes)


def _router_kernel(x_ref, a_ref, wo_ref, g_ref, w_ref, x1_ref, h_ref, info_ref, info_t_ref,
                   cnt_ref, *, n_experts):
    x1 = x_ref[...] + _dot(a_ref[...], wo_ref[...])
    x1_ref[...] = x1
    h = _rms(x1, g_ref[...])
    h_hi = h.astype(BF16)
    h_ref[...] = h_hi
    h_lo = (h - h_hi.astype(F32)).astype(BF16)
    w = w_ref[...]
    w_hi = w.astype(BF16)
    w_lo = (w - w_hi.astype(F32)).astype(BF16)
    logits = _dot(h_hi, w_hi) + _dot(h_lo, w_hi) + _dot(h_hi, w_lo)
    tm = logits.shape[0]
    lane = lax.broadcasted_iota(jnp.int32, logits.shape, 1)
    neg = -jnp.inf
    lg = jnp.where(lane < n_experts, logits, neg)
    m1 = jnp.max(lg, axis=1, keepdims=True)
    i1 = jnp.min(jnp.where(lg == m1, lane, LANES), axis=1, keepdims=True)
    lg2 = jnp.where(lane == i1, neg, lg)
    m2 = jnp.max(lg2, axis=1, keepdims=True)
    i2 = jnp.min(jnp.where(lg2 == m2, lane, LANES), axis=1, keepdims=True)
    e2 = jnp.exp(m2 - m1)
    w1 = 1.0 / (1.0 + e2)
    w2 = e2 / (1.0 + e2)

    routed = ((lane == i1) | (lane == i2)).astype(F32)
    cnt = jnp.sum(routed, axis=0, keepdims=True)
    cnt_ref[0] = cnt.astype(jnp.int32)
    earlier = (lax.broadcasted_iota(jnp.int32, (tm, tm), 1)
               < lax.broadcasted_iota(jnp.int32, (tm, tm), 0)).astype(BF16)
    rank = _dot(earlier, routed.astype(BF16))
    group_rows = jnp.floor((cnt + (ROW_ALIGN - 1)) * (1.0 / ROW_ALIGN)) * ROW_ALIGN
    lower = (lax.broadcasted_iota(jnp.int32, (LANES, LANES), 0)
             < lax.broadcasted_iota(jnp.int32, (LANES, LANES), 1)).astype(BF16)
    group_start = _dot(jnp.broadcast_to(group_rows, (8, LANES)).astype(BF16), lower)[0:1, :]
    slab_row = group_start + rank
    first = i1 < i2
    i_lo = jnp.where(first, i1, i2)
    i_hi = jnp.where(first, i2, i1)
    row_lo = jnp.sum(jnp.where(lane == i_lo, slab_row, 0.0), axis=1, keepdims=True)
    row_hi = jnp.sum(jnp.where(lane == i_hi, slab_row, 0.0), axis=1, keepdims=True)
    info = jnp.where(lane == 0, row_lo, 0.0) + jnp.where(lane == 1, row_hi, 0.0)
    info_t_ref[0] = info.T[0:8, :]
    info = info + jnp.where(lane == 2, jnp.where(first, w1, w2), 0.0)
    info_ref[...] = info + jnp.where(lane == 3, jnp.where(first, w2, w1), 0.0)


def router(x, a, w_o, gain, w_router):
    t, d = x.shape
    n_experts = w_router.shape[1]
    n_tiles = t // TOK_TILE
    w = jnp.pad(w_router, ((0, 0), (0, LANES - n_experts)))
    return pl.pallas_call(
        functools.partial(_router_kernel, n_experts=n_experts),
        grid=(n_tiles,),
        in_specs=[
            pl.BlockSpec((TOK_TILE, d), lambda i: (i, 0)),
            pl.BlockSpec((TOK_TILE, d), lambda i: (i, 0)),
            pl.BlockSpec((d, d), lambda i: (0, 0)),
            pl.BlockSpec((1, d), lambda i: (0, 0)),
            pl.BlockSpec((d, LANES), lambda i: (0, 0)),
        ],
        out_specs=[
            pl.BlockSpec((TOK_TILE, d), lambda i: (i, 0)),
            pl.BlockSpec((TOK_TILE, d), lambda i: (i, 0)),
            pl.BlockSpec((TOK_TILE, LANES), lambda i: (i, 0)),
            pl.BlockSpec((1, 8, TOK_TILE), lambda i: (i, 0, 0)),
            pl.BlockSpec((1, 1, LANES), lambda i: (i, 0, 0)),
        ],
        out_shape=[
            jax.ShapeDtypeStruct((t, d), F32),
            jax.ShapeDtypeStruct((t, d), BF16),
            jax.ShapeDtypeStruct((t, LANES), F32),
            jax.ShapeDtypeStruct((n_tiles, 8, TOK_TILE), F32),
            jax.ShapeDtypeStruct((n_tiles, 1, LANES), jnp.int32),
        ],
        compiler_params=_params("parallel"),
    )(x, a, w_o, gain.reshape(1, d), w)


def _slab_rows(n_experts):
    bound = MOE_TOPK * TOK_TILE + n_experts * (ROW_ALIGN - 1)
    return -(-bound // LANES) * LANES


def _copy_rows(n, max_size, make_copy, wait):
    off = jnp.int32(0)
    size = max_size
    while size >= ROW_ALIGN:
        @pl.when((n & size) != 0)
        def _(off=off, size=size):
            cp = make_copy(off, size)
            if wait:
                cp.wait()
            else:
                cp.start()
        off = off + (n & size)
        size //= 2


def _group_copies(n_ref, rs_ref, tile, n_experts, make_copy, wait):
    slab_row = jnp.int32(0)
    for e in range(n_experts):
        n = n_ref[tile * n_experts + e]
        base = rs_ref[tile * n_experts + e]

        def group_copy(off, size, slab_row=slab_row, base=base):
            return make_copy(pl.multiple_of(slab_row + off, ROW_ALIGN),
                             pl.multiple_of(base + off, ROW_ALIGN), size)

        _copy_rows(n, TOK_TILE, group_copy, wait)
        slab_row = slab_row + n


def _dispatch_kernel(rs_ref, n_ref, ps_ref, pn_ref, h_ref, info_t_ref, xs_ref,
                     slab_ref, zero_ref, sem, zero_sem, *, n_experts):
    i = pl.program_id(0)
    last = pl.num_programs(0) - 1
    slot = i % 2
    rows, tm = slab_ref.shape[1], h_ref.shape[0]
    info_t = info_t_ref[0]
    r = lax.broadcasted_iota(jnp.int32, (rows, tm), 0).astype(F32)
    onehot = ((r == info_t[0:1, :]) | (r == info_t[1:2, :])).astype(BF16)
    slab_ref[slot] = _dot(onehot, h_ref[...]).astype(BF16)

    def copies(tile, slot, wait):
        def make_copy(slab_row, buffer_row, size):
            return pltpu.make_async_copy(slab_ref.at[slot, pl.ds(slab_row, size)],
                                         xs_ref.at[pl.ds(buffer_row, size)], sem.at[slot])
        _group_copies(n_ref, rs_ref, tile, n_experts, make_copy, wait)

    def pad_copies(wait):
        for e in range(n_experts):
            def make_copy(off, size, e=e):
                start = pl.multiple_of(ps_ref[e] + off, ROW_ALIGN)
                return pltpu.make_async_copy(zero_ref.at[pl.ds(0, size)],
                                             xs_ref.at[pl.ds(start, size)], zero_sem)
            _copy_rows(pn_ref[e], zero_ref.shape[0], make_copy, wait)

        def idle_chunk(j, carry):
            start = pl.multiple_of(ps_ref[n_experts] + j * zero_ref.shape[0], ROW_ALIGN)
            cp = pltpu.make_async_copy(zero_ref, xs_ref.at[pl.ds(start, zero_ref.shape[0])],
                                       zero_sem)
            if wait:
                cp.wait()
            else:
                cp.start()
            return carry
        lax.fori_loop(0, pn_ref[n_experts], idle_chunk, 0)

    @pl.when(i > 0)
    def _():
        copies(i - 1, 1 - slot, wait=True)

    copies(i, slot, wait=False)

    @pl.when(i == last)
    def _():
        zero_ref[...] = jnp.zeros_like(zero_ref)
        pad_copies(wait=False)
        copies(i, slot, wait=True)
        pad_copies(wait=True)


def dispatch(h, info_t, row_start, n_rows, pad_start, pad_rows, *, total_rows, n_experts):
    t, d = h.shape
    n_tiles = t // TOK_TILE
    return pl.pallas_call(
        functools.partial(_dispatch_kernel, n_experts=n_experts),
        grid_spec=pltpu.PrefetchScalarGridSpec(
            num_scalar_prefetch=4,
            grid=(n_tiles,),
            in_specs=[
                pl.BlockSpec((TOK_TILE, d), lambda i, *_: (i, 0)),
                pl.BlockSpec((1, 8, TOK_TILE), lambda i, *_: (i, 0, 0)),
            ],
            out_specs=pl.BlockSpec(memory_space=pl.ANY),
            scratch_shapes=[pltpu.VMEM((2, _slab_rows(n_experts), d), BF16),
                            pltpu.VMEM((FFN_ROWS // 2, d), BF16),
                            pltpu.SemaphoreType.DMA((2,)),
                            pltpu.SemaphoreType.DMA],
        ),
        out_shape=jax.ShapeDtypeStruct((total_rows, d), BF16),
        compiler_params=_params("arbitrary"),
    )(row_start, n_rows, pad_start, pad_rows, h, info_t)


def _grouped_ffn_kernel(te_ref, na_ref, x_ref, wg_ref, wu_ref, wd_ref, o_ref, acc_ref):
    j = pl.program_id(0)
    f = pl.program_id(1)
    last_f = pl.num_programs(1) - 1
    active = j < na_ref[0]

    @pl.when(active)
    def _():
        @pl.when(f == 0)
        def _():
            acc_ref[...] = jnp.zeros_like(acc_ref)

        x = x_ref[...]
        act = _silu(_dot(x, wg_ref[...])) * _dot(x, wu_ref[...])
        acc_ref[...] += _dot(act.astype(BF16), wd_ref[...])

        @pl.when(f == last_f)
        def _():
            o_ref[...] = acc_ref[...].astype(o_ref.dtype)

    @pl.when(jnp.logical_not(active) & (f == last_f))
    def _():
        o_ref[...] = jnp.zeros_like(o_ref)


def grouped_ffn(xs, w_gu, w_down, layer, tile_expert, n_active, *, f_tile):
    rows, d = xs.shape
    ff = w_down.shape[2]
    nf = ff // f_tile

    def f_idx(j, f, na):
        return jnp.where(j < na[0], f, nf - 1)

    return pl.pallas_call(
        _grouped_ffn_kernel,
        grid_spec=pltpu.PrefetchScalarGridSpec(
            num_scalar_prefetch=2,
            grid=(rows // FFN_ROWS, nf),
            in_specs=[
                pl.BlockSpec((FFN_ROWS, d), lambda j, f, te, na: (jnp.minimum(j, na[0] - 1), 0)),
                pl.BlockSpec((None, None, d, f_tile),
                             lambda j, f, te, na: (layer, te[j], 0, f_idx(j, f, na))),
                pl.BlockSpec((None, None, d, f_tile),
                             lambda j, f, te, na: (layer, te[j], 0, nf + f_idx(j, f, na))),
                pl.BlockSpec((None, None, f_tile, d),
                             lambda j, f, te, na: (layer, te[j], f_idx(j, f, na), 0)),
            ],
            out_specs=pl.BlockSpec((FFN_ROWS, d), lambda j, f, te, na: (j, 0)),
            scratch_shapes=[pltpu.VMEM((FFN_ROWS, d), F32)],
        ),
        out_shape=jax.ShapeDtypeStruct((rows, d), BF16),
        compiler_params=_params("arbitrary", "arbitrary"),
    )(tile_expert, n_active, xs, w_gu, w_gu, w_down)


def _combine_kernel(rs_ref, n_ref, x_ref, info_ref, fg_ref, y_ref, o_ref, ybuf_ref, sem,
                    *, n_experts, final_norm):
    i = pl.program_id(0)
    slot = i % 2
    rows, tm = ybuf_ref.shape[1], x_ref.shape[0]

    def copies(tile, slot, wait):
        def make_copy(slab_row, buffer_row, size):
            return pltpu.make_async_copy(y_ref.at[pl.ds(buffer_row, size)],
                                         ybuf_ref.at[slot, pl.ds(slab_row, size)], sem.at[slot])
        _group_copies(n_ref, rs_ref, tile, n_experts, make_copy, wait)

    @pl.when(i == 0)
    def _():
        ybuf_ref[...] = jnp.zeros_like(ybuf_ref)
        copies(i, slot, wait=False)

    @pl.when(i + 1 < pl.num_programs(0))
    def _():
        copies(i + 1, 1 - slot, wait=False)

    copies(i, slot, wait=True)
    y = ybuf_ref[slot]
    info = info_ref[...]
    r = lax.broadcasted_iota(jnp.int32, (tm, rows), 1).astype(F32)
    pick_lo = (r == info[:, 0:1]).astype(BF16)
    pick_hi = (r == info[:, 1:2]).astype(BF16)
    out = x_ref[...] + info[:, 2:3] * _dot(pick_lo, y) + info[:, 3:4] * _dot(pick_hi, y)
    o_ref[...] = _rms(out, fg_ref[...]) if final_norm else out


def combine(x, info, y, row_start, n_rows, final_gain, *, n_experts, final_norm):
    t, d = x.shape
    n_tiles = t // TOK_TILE
    return pl.pallas_call(
        functools.partial(_combine_kernel, n_experts=n_experts, final_norm=final_norm),
        grid_spec=pltpu.PrefetchScalarGridSpec(
            num_scalar_prefetch=2,
            grid=(n_tiles,),
            in_specs=[
                pl.BlockSpec((TOK_TILE, d), lambda i, rs, n: (i, 0)),
                pl.BlockSpec((TOK_TILE, LANES), lambda i, rs, n: (i, 0)),
                pl.BlockSpec((1, d), lambda i, rs, n: (0, 0)),
                pl.BlockSpec(memory_space=pl.ANY),
            ],
            out_specs=pl.BlockSpec((TOK_TILE, d), lambda i, rs, n: (i, 0)),
            scratch_shapes=[pltpu.VMEM((2, _slab_rows(n_experts), d), BF16),
                            pltpu.SemaphoreType.DMA((2,))],
        ),
        out_shape=jax.ShapeDtypeStruct((t, d), F32),
        compiler_params=_params("arbitrary"),
    )(row_start, n_rows, x, info, final_gain.reshape(1, d), y)


def _ffn_kernel(x_ref, a_ref, wo_ref, g_ref, wg_ref, wu_ref, wd_ref, o_ref):
    x1 = x_ref[...] + _dot(a_ref[...], wo_ref[...])
    h = _rms(x1, g_ref[...]).astype(BF16)
    act = _silu(_dot(h, wg_ref[...])) * _dot(h, wu_ref[...])
    o_ref[...] = x1 + _dot(act.astype(BF16), wd_ref[...])


def ffn(x, a, w_o, gain, w_gu, w_down):
    t, d = x.shape
    ff = w_down.shape[0]
    once = pl.Buffered(1)
    rows = pl.BlockSpec((FFN_DENSE_ROWS, d), lambda i: (i, 0))
    return pl.pallas_call(
        _ffn_kernel,
        grid=(t // FFN_DENSE_ROWS,),
        in_specs=[
            rows,
            rows,
            pl.BlockSpec((d, d), lambda i: (0, 0), pipeline_mode=once),
            pl.BlockSpec((1, d), lambda i: (0, 0)),
            pl.BlockSpec((d, ff), lambda i: (0, 0), pipeline_mode=once),
            pl.BlockSpec((d, ff), lambda i: (0, 1), pipeline_mode=once),
            pl.BlockSpec((ff, d), lambda i: (0, 0), pipeline_mode=once),
        ],
        out_specs=rows,
        out_shape=jax.ShapeDtypeStruct((t, d), F32),
        compiler_params=_params("parallel"),
    )(x, a, w_o, gain.reshape(1, d), w_gu, w_gu, w_down)


def _rmsnorm_kernel(x_ref, g_ref, o_ref):
    o_ref[...] = _rms(x_ref[...], g_ref[...])


def rmsnorm(x, gain):
    t, d = x.shape
    return pl.pallas_call(
        _rmsnorm_kernel,
        grid=(t // ROW_TILE,),
        in_specs=[pl.BlockSpec((ROW_TILE, d), lambda i: (i, 0)),
                  pl.BlockSpec((1, d), lambda i: (0, 0))],
        out_specs=pl.BlockSpec((ROW_TILE, d), lambda i: (i, 0)),
        out_shape=jax.ShapeDtypeStruct((t, d), F32),
        compiler_params=_params("parallel"),
    )(x, gain.reshape(1, d))


def _mxu_tile(n, target):
    best = None
    for cand in range(MXU_WIDTH, target + 1, MXU_WIDTH):
        if n % cand == 0:
            best = cand
    assert best is not None, (n, target)
    return best


def moe_layer(x, a, w_o, gain, w_router, w_gu, w_down, layer, final_gain, *, final_norm):
    t, d = x.shape
    n_experts = w_router.shape[1]
    n_tiles = t // TOK_TILE
    x, h, info, info_t, cnt = router(x, a, w_o, gain, w_router)
    cnt = cnt.reshape(n_tiles, LANES)[:, :n_experts]
    n_rows = (cnt + ROW_ALIGN - 1) // ROW_ALIGN * ROW_ALIGN
    seg_rows = jnp.sum(n_rows, axis=0)
    seg_cap = (seg_rows + FFN_ROWS - 1) // FFN_ROWS * FFN_ROWS
    seg_end = jnp.cumsum(seg_cap)
    row_start = (seg_end - seg_cap)[None, :] + jnp.cumsum(n_rows, axis=0) - n_rows
    max_rows = (MOE_TOPK * t + n_tiles * n_experts * (ROW_ALIGN - 1)
                + n_experts * (FFN_ROWS - ROW_ALIGN))
    total_tiles = -(-max_rows // FFN_ROWS)
    n_active = (seg_end[-1:] // FFN_ROWS).astype(jnp.int32)
    tile_expert = jnp.minimum(
        jnp.searchsorted(seg_end, jnp.arange(total_tiles, dtype=jnp.int32) * FFN_ROWS, side="right"),
        n_experts - 1).astype(jnp.int32)
    row_start = row_start.reshape(-1).astype(jnp.int32)
    n_rows = n_rows.reshape(-1).astype(jnp.int32)
    total_rows = total_tiles * FFN_ROWS
    pad_start = jnp.concatenate([seg_end - seg_cap + seg_rows, seg_end[-1:]]).astype(jnp.int32)
    pad_rows = jnp.concatenate([seg_cap - seg_rows,
                                (total_rows - seg_end[-1:]) // (FFN_ROWS // 2)]).astype(jnp.int32)
    xs = dispatch(h, info_t, row_start, n_rows, pad_start, pad_rows,
                  total_rows=total_rows, n_experts=n_experts)
    y = grouped_ffn(xs, w_gu, w_down, layer, tile_expert, n_active,
                    f_tile=_mxu_tile(w_down.shape[2], 2048))
    return combine(x, info, y, row_start, n_rows, final_gain,
                   n_experts=n_experts, final_norm=final_norm)


def kernel(x, attn_norm, ffn_norm, hgrn_w_in, hgrn_lb_logits, hgrn_out_norm, hgrn_w_o, kv_norm, w_kv, moba_w_q, moba_w_o, ffn_w_gu, ffn_w_down, moe_router, moe_w_gu, moe_w_down, final_norm):
    batch, seq, d = x.shape
    depth = attn_norm.shape[0]
    n_a = hgrn_w_in.shape[0]
    x = x.reshape(batch * seq, d)
    k = v_t = k_mean = None
    moe_gu = moe_w_gu.astype(BF16)
    moe_down = moe_w_down.astype(BF16)
    for layer in range(depth):
        if layer < n_a:
            proj = norm_matmul(x, attn_norm[layer], hgrn_w_in[layer].astype(BF16), F32)
            o = hgrn_recurrence(proj, hgrn_lb_logits, hgrn_out_norm[layer],
                                layer=layer, batch=batch, seq=seq)
            w_o = hgrn_w_o[layer].astype(BF16)
        else:
            if layer == n_a:
                k, v_t, k_mean = shared_kv(x, kv_norm, w_kv[:, :d].astype(BF16),
                                           w_kv[:, d:].T.astype(BF16))
            j = layer - n_a
            q = norm_matmul(x, attn_norm[layer], moba_w_q[j].astype(BF16), F32)
            o = moba_attention(q, k, v_t, k_mean, batch=batch, seq=seq)
            w_o = moba_w_o[j].astype(BF16)
        j = layer // 2
        if layer % 2 == 0:
            x = ffn(x, o, w_o, ffn_norm[layer], ffn_w_gu[j].astype(BF16),
                    ffn_w_down[j].astype(BF16))
        else:
            last = layer == depth - 1
            x = moe_layer(x, o, w_o, ffn_norm[layer], moe_router[j], moe_gu, moe_down, j,
                          final_norm, final_norm=last)
    if depth % 2 == 1:
        x = rmsnorm(x, final_norm)
    return x.reshape(batch, seq, d)
```

```python
import functools

import jax
import jax.numpy as jnp
from jax import lax
from jax.experimental import pallas as pl
from jax.experimental.pallas import tpu as pltpu

F32 = jnp.float32
BF16 = jnp.bfloat16

EPS = 1e-6
HEAD_DIM = 128
HGRN_CHUNK = 64
MOBA_BLOCK = 256
MOBA_TOPK = 3
MOE_TOPK = 2
MOBA_HEADS_PER_STEP = 2
LOG2E = 1.4426950408889634
LANES = 128
MXU_WIDTH = 256
FFN_DENSE_ROWS = 256
VMEM_LIMIT = 56 * 1024 * 1024

HGRN_TILE = 512
HGRN_HEADS_PER_STEP = 2
ROW_TILE = 512
TOK_TILE = 512
FFN_ROWS = 512
ROW_ALIGN = 16


def _params(*semantics):
    return pltpu.CompilerParams(dimension_semantics=semantics, vmem_limit_bytes=VMEM_LIMIT)


def _rms(x, gain):
    return x * lax.rsqrt(jnp.mean(x * x, axis=-1, keepdims=True) + EPS) * gain


def _dot(a, b):
    return jnp.dot(a, b, preferred_element_type=F32)


def _dot_nt(a, b):
    return lax.dot_general(a, b, (((1,), (1,)), ((), ())), preferred_element_type=F32)


def _dot_tn(a, b):
    return lax.dot_general(a, b, (((0,), (0,)), ((), ())), preferred_element_type=F32)


def _silu(x):
    half = 0.5 * x
    return half + half * jnp.tanh(half)


def _norm_matmul_kernel(x_ref, g_ref, w_ref, o_ref):
    h = _rms(x_ref[...], g_ref[...]).astype(BF16)
    o_ref[...] = _dot(h, w_ref[...]).astype(o_ref.dtype)


def norm_matmul(x, gain, w, out_dtype):
    t, d = x.shape
    n = w.shape[1]
    return pl.pallas_call(
        _norm_matmul_kernel,
        grid=(t // ROW_TILE,),
        in_specs=[
            pl.BlockSpec((ROW_TILE, d), lambda i: (i, 0)),
            pl.BlockSpec((1, d), lambda i: (0, 0)),
            pl.BlockSpec((d, n), lambda i: (0, 0)),
        ],
        out_specs=pl.BlockSpec((ROW_TILE, n), lambda i: (i, 0)),
        out_shape=jax.ShapeDtypeStruct((t, n), out_dtype),
        compiler_params=_params("parallel"),
    )(x, gain.reshape(1, d), w)


def _kv_kernel(x_ref, g_ref, wk_ref, wvt_ref, k_ref, vt_ref, km_ref):
    h = _rms(x_ref[...], g_ref[...]).astype(BF16)
    k = _dot(h, wk_ref[...])
    k_ref[...] = k.astype(k_ref.dtype)
    vt_ref[0] = _dot_nt(wvt_ref[...], h).astype(vt_ref.dtype)
    km_ref[0] = jnp.mean(k, axis=0, keepdims=True)


def shared_kv(x, gain, w_k, w_v_t):
    t, d = x.shape
    nblk = t // MOBA_BLOCK
    return pl.pallas_call(
        _kv_kernel,
        grid=(nblk,),
        in_specs=[
            pl.BlockSpec((MOBA_BLOCK, d), lambda i: (i, 0)),
            pl.BlockSpec((1, d), lambda i: (0, 0)),
            pl.BlockSpec((d, d), lambda i: (0, 0)),
            pl.BlockSpec((d, d), lambda i: (0, 0)),
        ],
        out_specs=[
            pl.BlockSpec((MOBA_BLOCK, d), lambda i: (i, 0)),
            pl.BlockSpec((1, d, MOBA_BLOCK), lambda i: (i, 0, 0)),
            pl.BlockSpec((1, 1, d), lambda i: (i, 0, 0)),
        ],
        out_shape=[
            jax.ShapeDtypeStruct((t, d), BF16),
            jax.ShapeDtypeStruct((nblk, d, MOBA_BLOCK), BF16),
            jax.ShapeDtypeStruct((nblk, 1, d), F32),
        ],
        compiler_params=_params("parallel"),
    )(x, gain.reshape(1, d), w_k, w_v_t)


def _hgrn_heads(q_ref, f_ref, i_ref, g_ref, lbl_ref, gain_ref, o_ref, st_ref, *, layer, first):
    c = HGRN_CHUNK
    dk = HEAD_DIM
    heads = q_ref.shape[1] // dk
    per_head = q_ref.shape[0] // c
    n_chunks = heads * per_head
    group = MOBA_BLOCK // c

    def wide(ref):
        return jnp.concatenate([ref[n * c:(n + 1) * c, hh * dk:(hh + 1) * dk]
                                for hh in range(heads) for n in range(per_head)], axis=1)

    def tall(ref):
        return jnp.concatenate([ref[:, hh * dk:(hh + 1) * dk] for hh in range(heads)], axis=0)

    def chunk(x, n):
        return x[:, n * dk:(n + 1) * dk]

    def rows_of(x, n0, n1):
        return jnp.concatenate([chunk(x, n) for n in range(n0, n1)], axis=0)

    lanes = slice(first * dk, (first + heads) * dk)
    fz = wide(f_ref)
    t = jnp.exp(-jnp.abs(fz))
    log_sig = jnp.minimum(fz, 0.0) - jnp.log(1.0 + t)
    sig_neg = jnp.exp(log_sig - fz)
    if layer > 0:
        logits = lbl_ref[:, lanes]
        e = jnp.exp(logits - jnp.max(logits, axis=0, keepdims=True))
        p = e / jnp.sum(e, axis=0, keepdims=True)
        lb = jnp.sum(p[1:layer + 1], axis=0, keepdims=True)
        lb = jnp.concatenate([chunk(lb, hh) for hh in range(heads) for _ in range(per_head)],
                             axis=1)
        log_lb = jnp.log(lb)
        y = jnp.log(1.0 - lb) + log_sig
        log_f = jnp.maximum(log_lb, y) + jnp.log(1.0 + jnp.exp(-jnp.abs(log_lb - y)))
        k = (1.0 - lb) * sig_neg
    else:
        log_f = log_sig
        k = sig_neg

    tril = (lax.broadcasted_iota(jnp.int32, (c, c), 1)
            <= lax.broadcasted_iota(jnp.int32, (c, c), 0)).astype(BF16)
    hi = log_f.astype(BF16)
    rest = log_f - hi.astype(F32)
    mid = rest.astype(BF16)
    lo = (rest - mid.astype(F32)).astype(BF16)
    b = _dot(tril, hi) + _dot(tril, mid) + _dot(tril, lo)
    b_mid = b[c // 2 - 1:c // 2, :]
    b_last = b[c - 1:c, :]

    q = _silu(wide(q_ref))
    qd = q * jnp.exp(b - b_mid)
    kd = k * jnp.exp(b_mid - b)
    qe = (qd * jnp.exp(b_mid)).astype(BF16)
    kl = (kd * jnp.exp(b_last - b_mid)).astype(BF16)
    qd = qd.astype(BF16)
    kd = kd.astype(BF16)
    v = wide(i_ref).astype(BF16)
    decay = jnp.exp(b_last)

    gr = group * c
    r_i = lax.broadcasted_iota(jnp.int32, (gr, gr), 0)
    c_i = lax.broadcasted_iota(jnp.int32, (gr, gr), 1)
    keep = (c_i <= r_i) & ((r_i // c) == (c_i // c))
    groups = range(0, n_chunks, group)
    scores = [_dot_nt(rows_of(qd, n0, n0 + group), rows_of(kd, n0, n0 + group)) for n0 in groups]

    grow = [_dot_tn(chunk(v, n), chunk(kl, n)) for n in range(n_chunks)]
    intra = [_dot(jnp.where(keep, a, 0.0).astype(BF16), rows_of(v, n0, n0 + group))
             for a, n0 in zip(scores, groups)]
    before = []
    for hh in range(heads):
        st = st_ref[first + hh]
        for n in range(hh * per_head, (hh + 1) * per_head):
            before.append(st.astype(BF16))
            st = chunk(decay, n) * st + grow[n]
        st_ref[first + hh] = st
    inter = [_dot_nt(chunk(qe, n), before[n]) for n in range(n_chunks)]
    o = jnp.concatenate(intra, axis=0) + jnp.concatenate(inter, axis=0)

    o = o * lax.rsqrt(jnp.mean(o * o, axis=-1, keepdims=True) + EPS)
    o = o * _silu(tall(g_ref))
    rows = q_ref.shape[0]
    for hh in range(heads):
        cols = slice((first + hh) * dk, (first + hh + 1) * dk)
        o_ref[:, cols] = (o[hh * rows:(hh + 1) * rows] * gain_ref[:, cols]).astype(o_ref.dtype)


def _hgrn_layer_kernel(x_ref, ng_ref, w_ref, lbl_ref, gain_ref, o_ref, st_ref, *, layer):
    @pl.when(pl.program_id(1) == 0)
    def _():
        st_ref[...] = jnp.zeros_like(st_ref)

    d = x_ref.shape[1]
    hp = HGRN_HEADS_PER_STEP
    width = hp * HEAD_DIM
    h = _rms(x_ref[...], ng_ref[...]).astype(BF16)

    def project(group):
        return [_dot(h, w_ref[:, part * d + group * width:part * d + (group + 1) * width])
                for part in range(4)]

    n_groups = d // width
    nxt = project(0)
    for group in range(n_groups):
        cur = nxt
        if group + 1 < n_groups:
            nxt = project(group + 1)
        _hgrn_heads(*cur, lbl_ref, gain_ref, o_ref, st_ref, layer=layer, first=group * hp)


def hgrn_layer(x, norm_gain, w_in, lb_logits, out_gain, *, layer, batch, seq):
    t, d = x.shape
    heads = d // HEAD_DIM
    tiles = seq // HGRN_TILE
    n_layers = lb_logits.shape[0]
    return pl.pallas_call(
        functools.partial(_hgrn_layer_kernel, layer=layer),
        grid=(batch, tiles),
        in_specs=[
            pl.BlockSpec((HGRN_TILE, d), lambda b, s: (b * tiles + s, 0)),
            pl.BlockSpec((1, d), lambda b, s: (0, 0)),
            pl.BlockSpec((d, 4 * d), lambda b, s: (0, 0), pipeline_mode=pl.Buffered(1)),
            pl.BlockSpec((n_layers, d), lambda b, s: (0, 0)),
            pl.BlockSpec((1, d), lambda b, s: (0, 0)),
        ],
        out_specs=pl.BlockSpec((HGRN_TILE, d), lambda b, s: (b * tiles + s, 0)),
        out_shape=jax.ShapeDtypeStruct((t, d), BF16),
        scratch_shapes=[pltpu.VMEM((heads, HEAD_DIM, HEAD_DIM), F32)],
        compiler_params=_params("parallel", "arbitrary"),
    )(x, norm_gain.reshape(1, d), w_in, lb_logits, out_gain.reshape(1, d))


def _moba_kernel(q_ref, k_ref, vt_ref, km_ref, slope_ref, o_ref):
    bs = MOBA_BLOCK
    dh = HEAD_DIM
    n_blk = km_ref.shape[1]
    heads = q_ref.shape[1] // dh
    scale2 = HEAD_DIM ** -0.5 * LOG2E
    rel = (lax.broadcasted_iota(jnp.int32, (bs, bs), 1)
           - lax.broadcasted_iota(jnp.int32, (bs, bs), 0))
    rel_f = rel.astype(F32)

    def gate_rank(q, km, n):
        gate = lax.dot_general(km, q, (((1,), (1,)), ((), ())),
                               precision=lax.Precision.HIGHEST, preferred_element_type=F32)
        blk = lax.broadcasted_iota(jnp.int32, gate.shape, 0)
        rank = jnp.zeros(gate.shape, F32)
        for m in range(n):
            g_m = gate[m:m + 1, :]
            rank = rank + ((g_m > gate) | ((g_m == gate) & (m < blk))).astype(F32)
        return rank

    def scores(n, hh):
        cols = slice(hh * dh, (hh + 1) * dh)
        q = q_ref[n * bs:(n + 1) * bs, cols]
        q16 = (q * scale2).astype(BF16)
        rank = gate_rank(q, km_ref[0][:, cols], n) if n > MOBA_TOPK else None
        return _dot_nt(k_ref[0:(n + 1) * bs, cols], q16), rank

    def probabilities(n, hh, s_all, rank):
        slope2 = slope_ref[hh][0:1, 0:1] * LOG2E
        bias = -slope2 * rel_f
        us, shifts, maxes = [], [], []
        for kb in range(n + 1):
            u = s_all[kb * bs:(kb + 1) * bs] + bias
            if kb == n:
                u = jnp.where(rel >= 0, u, -jnp.inf)
            elif rank is not None:
                u = jnp.where(rank[kb:kb + 1, :] < MOBA_TOPK, u, -jnp.inf)
            shift = slope2 * float((n - kb) * bs)
            us.append(u)
            shifts.append(shift)
            maxes.append(jnp.max(u, axis=0, keepdims=True) - shift)
        m = functools.reduce(jnp.maximum, maxes)
        ps = [jnp.exp2(u - (m + shift)) for u, shift in zip(us, shifts)]
        return jnp.concatenate(ps, axis=0).astype(BF16)

    def output(n, hh, p_all):
        cols = slice(hh * dh, (hh + 1) * dh)
        vt = jnp.concatenate([vt_ref[kb, cols, :] for kb in range(n + 1)], axis=1)
        ones = jnp.ones((ROW_ALIGN, vt.shape[1]), BF16)
        pv = _dot(jnp.concatenate([vt, ones], axis=0), p_all)
        o_ref[n * bs:(n + 1) * bs, cols] = (pv[0:dh] / pv[dh:dh + 1]).T.astype(o_ref.dtype)

    scored = [scores(0, hh) for hh in range(heads)]
    for n in range(n_blk):
        probs = [probabilities(n, hh, *scored[hh]) for hh in range(heads)]
        if n + 1 < n_blk:
            scored = [scores(n + 1, hh) for hh in range(heads)]
        for hh in range(heads):
            output(n, hh, probs[hh])


def moba_attention(q, k, v_t, k_mean, *, batch, seq):
    t, d = q.shape
    heads = d // HEAD_DIM
    n_blk = seq // MOBA_BLOCK
    hp = MOBA_HEADS_PER_STEP
    width = hp * HEAD_DIM
    slopes = (2.0 ** (-8.0 * jnp.arange(1, heads + 1, dtype=F32) / heads))
    slopes = jnp.broadcast_to(slopes[:, None, None], (heads, 1, LANES))
    return pl.pallas_call(
        _moba_kernel,
        grid=(batch, heads // hp),
        in_specs=[
            pl.BlockSpec((seq, width), lambda b, h: (b, h)),
            pl.BlockSpec((seq, width), lambda b, h: (b, h)),
            pl.BlockSpec((n_blk, width, MOBA_BLOCK), lambda b, h: (b, h, 0)),
            pl.BlockSpec((1, n_blk, width), lambda b, h: (b, 0, h)),
            pl.BlockSpec((hp, 1, LANES), lambda b, h: (h, 0, 0)),
        ],
        out_specs=pl.BlockSpec((seq, width), lambda b, h: (b, h)),
        out_shape=jax.ShapeDtypeStruct((t, d), BF16),
        compiler_params=_params("parallel", "parallel"),
    )(q, k, v_t, k_mean.reshape(batch, n_blk, d), slopes)


def _router_kernel(x_ref, a_ref, wo_ref, g_ref, w_ref, x1_ref, h_ref, info_ref, info_t_ref,
                   cnt_ref, *, n_experts):
    x1 = x_ref[...] + _dot(a_ref[...], wo_ref[...])
    x1_ref[...] = x1
    h = _rms(x1, g_ref[...])
    h_hi = h.astype(BF16)
    h_ref[...] = h_hi
    h_lo = (h - h_hi.astype(F32)).astype(BF16)
    w = w_ref[...]
    w_hi = w.astype(BF16)
    w_lo = (w - w_hi.astype(F32)).astype(BF16)
    logits = _dot(h_hi, w_hi) + _dot(h_lo, w_hi) + _dot(h_hi, w_lo)
    tm = logits.shape[0]
    lane = lax.broadcasted_iota(jnp.int32, logits.shape, 1)
    neg = -jnp.inf
    lg = jnp.where(lane < n_experts, logits, neg)
    m1 = jnp.max(lg, axis=1, keepdims=True)
    i1 = jnp.min(jnp.where(lg == m1, lane, LANES), axis=1, keepdims=True)
    lg2 = jnp.where(lane == i1, neg, lg)
    m2 = jnp.max(lg2, axis=1, keepdims=True)
    i2 = jnp.min(jnp.where(lg2 == m2, lane, LANES), axis=1, keepdims=True)
    e2 = jnp.exp(m2 - m1)
    w1 = 1.0 / (1.0 + e2)
    w2 = e2 / (1.0 + e2)

    routed = ((lane == i1) | (lane == i2)).astype(F32)
    cnt = jnp.sum(routed, axis=0, keepdims=True)
    cnt_ref[0] = cnt.astype(jnp.int32)
    earlier = (lax.broadcasted_iota(jnp.int32, (tm, tm), 1)
               < lax.broadcasted_iota(jnp.int32, (tm, tm), 0)).astype(BF16)
    rank = _dot(earlier, routed.astype(BF16))
    group_rows = jnp.floor((cnt + (ROW_ALIGN - 1)) * (1.0 / ROW_ALIGN)) * ROW_ALIGN
    lower = (lax.broadcasted_iota(jnp.int32, (LANES, LANES), 0)
             < lax.broadcasted_iota(jnp.int32, (LANES, LANES), 1)).astype(BF16)
    group_start = _dot(jnp.broadcast_to(group_rows, (8, LANES)).astype(BF16), lower)[0:1, :]
    slab_row = group_start + rank
    first = i1 < i2
    i_lo = jnp.where(first, i1, i2)
    i_hi = jnp.where(first, i2, i1)
    row_lo = jnp.sum(jnp.where(lane == i_lo, slab_row, 0.0), axis=1, keepdims=True)
    row_hi = jnp.sum(jnp.where(lane == i_hi, slab_row, 0.0), axis=1, keepdims=True)
    info = jnp.where(lane == 0, row_lo, 0.0) + jnp.where(lane == 1, row_hi, 0.0)
    info_t_ref[0] = info.T[0:8, :]
    info = info + jnp.where(lane == 2, jnp.where(first, w1, w2), 0.0)
    info_ref[...] = info + jnp.where(lane == 3, jnp.where(first, w2, w1), 0.0)


def router(x, a, w_o, gain, w_router):
    t, d = x.shape
    n_experts = w_router.shape[1]
    n_tiles = t // TOK_TILE
    w = jnp.pad(w_router, ((0, 0), (0, LANES - n_experts)))
    return pl.pallas_call(
        functools.partial(_router_kernel, n_experts=n_experts),
        grid=(n_tiles,),
        in_specs=[
            pl.BlockSpec((TOK_TILE, d), lambda i: (i, 0)),
            pl.BlockSpec((TOK_TILE, d), lambda i: (i, 0)),
            pl.BlockSpec((d, d), lambda i: (0, 0)),
            pl.BlockSpec((1, d), lambda i: (0, 0)),
            pl.BlockSpec((d, LANES), lambda i: (0, 0)),
        ],
        out_specs=[
            pl.BlockSpec((TOK_TILE, d), lambda i: (i, 0)),
            pl.BlockSpec((TOK_TILE, d), lambda i: (i, 0)),
            pl.BlockSpec((TOK_TILE, LANES), lambda i: (i, 0)),
            pl.BlockSpec((1, 8, TOK_TILE), lambda i: (i, 0, 0)),
            pl.BlockSpec((1, 1, LANES), lambda i: (i, 0, 0)),
        ],
        out_shape=[
            jax.ShapeDtypeStruct((t, d), F32),
            jax.ShapeDtypeStruct((t, d), BF16),
            jax.ShapeDtypeStruct((t, LANES), F32),
            jax.ShapeDtypeStruct((n_tiles, 8, TOK_TILE), F32),
            jax.ShapeDtypeStruct((n_tiles, 1, LANES), jnp.int32),
        ],
        compiler_params=_params("parallel"),
    )(x, a, w_o, gain.reshape(1, d), w)


def _slab_rows(n_experts):
    bound = MOE_TOPK * TOK_TILE + n_experts * (ROW_ALIGN - 1)
    return -(-bound // LANES) * LANES


def _copy_rows(n, max_size, make_copy, wait):
    off = jnp.int32(0)
    size = max_size
    while size >= ROW_ALIGN:
        @pl.when((n & size) != 0)
        def _(off=off, size=size):
            cp = make_copy(off, size)
            if wait:
                cp.wait()
            else:
                cp.start()
        off = off + (n & size)
        size //= 2


def _group_copies(n_ref, rs_ref, tile, n_experts, make_copy, wait):
    slab_row = jnp.int32(0)
    for e in range(n_experts):
        n = n_ref[tile * n_experts + e]
        base = rs_ref[tile * n_experts + e]

        def group_copy(off, size, slab_row=slab_row, base=base):
            return make_copy(pl.multiple_of(slab_row + off, ROW_ALIGN),
                             pl.multiple_of(base + off, ROW_ALIGN), size)

        _copy_rows(n, TOK_TILE, group_copy, wait)
        slab_row = slab_row + n


def _dispatch_kernel(rs_ref, n_ref, ps_ref, pn_ref, h_ref, info_t_ref, xs_ref,
                     slab_ref, zero_ref, sem, zero_sem, *, n_experts):
    i = pl.program_id(0)
    last = pl.num_programs(0) - 1
    slot = i % 2
    rows, tm = slab_ref.shape[1], h_ref.shape[0]
    info_t = info_t_ref[0]
    r = lax.broadcasted_iota(jnp.int32, (rows, tm), 0).astype(F32)
    onehot = ((r == info_t[0:1, :]) | (r == info_t[1:2, :])).astype(BF16)
    slab_ref[slot] = _dot(onehot, h_ref[...]).astype(BF16)

    def copies(tile, slot, wait):
        def make_copy(slab_row, buffer_row, size):
            return pltpu.make_async_copy(slab_ref.at[slot, pl.ds(slab_row, size)],
                                         xs_ref.at[pl.ds(buffer_row, size)], sem.at[slot])
        _group_copies(n_ref, rs_ref, tile, n_experts, make_copy, wait)

    def pad_copies(wait):
        for e in range(n_experts):
            def make_copy(off, size, e=e):
                start = pl.multiple_of(ps_ref[e] + off, ROW_ALIGN)
                return pltpu.make_async_copy(zero_ref.at[pl.ds(0, size)],
                                             xs_ref.at[pl.ds(start, size)], zero_sem)
            _copy_rows(pn_ref[e], zero_ref.shape[0], make_copy, wait)

        def idle_chunk(j, carry):
            start = pl.multiple_of(ps_ref[n_experts] + j * zero_ref.shape[0], ROW_ALIGN)
            cp = pltpu.make_async_copy(zero_ref, xs_ref.at[pl.ds(start, zero_ref.shape[0])],
                                       zero_sem)
            if wait:
                cp.wait()
            else:
                cp.start()
            return carry
        lax.fori_loop(0, pn_ref[n_experts], idle_chunk, 0)

    @pl.when(i > 0)
    def _():
        copies(i - 1, 1 - slot, wait=True)

    copies(i, slot, wait=False)

    @pl.when(i == last)
    def _():
        zero_ref[...] = jnp.zeros_like(zero_ref)
        pad_copies(wait=False)
        copies(i, slot, wait=True)
        pad_copies(wait=True)


def dispatch(h, info_t, row_start, n_rows, pad_start, pad_rows, *, total_rows, n_experts):
    t, d = h.shape
    n_tiles = t // TOK_TILE
    return pl.pallas_call(
        functools.partial(_dispatch_kernel, n_experts=n_experts),
        grid_spec=pltpu.PrefetchScalarGridSpec(
            num_scalar_prefetch=4,
            grid=(n_tiles,),
            in_specs=[
                pl.BlockSpec((TOK_TILE, d), lambda i, *_: (i, 0)),
                pl.BlockSpec((1, 8, TOK_TILE), lambda i, *_: (i, 0, 0)),
            ],
            out_specs=pl.BlockSpec(memory_space=pl.ANY),
            scratch_shapes=[pltpu.VMEM((2, _slab_rows(n_experts), d), BF16),
                            pltpu.VMEM((FFN_ROWS // 2, d), BF16),
                            pltpu.SemaphoreType.DMA((2,)),
                            pltpu.SemaphoreType.DMA],
        ),
        out_shape=jax.ShapeDtypeStruct((total_rows, d), BF16),
        compiler_params=_params("arbitrary"),
    )(row_start, n_rows, pad_start, pad_rows, h, info_t)


def _grouped_ffn_kernel(te_ref, na_ref, x_ref, wg_ref, wu_ref, wd_ref, o_ref, acc_ref):
    j = pl.program_id(0)
    f = pl.program_id(1)
    last_f = pl.num_programs(1) - 1
    active = j < na_ref[0]

    @pl.when(active)
    def _():
        @pl.when(f == 0)
        def _():
            acc_ref[...] = jnp.zeros_like(acc_ref)

        x = x_ref[...]
        act = _silu(_dot(x, wg_ref[...])) * _dot(x, wu_ref[...])
        acc_ref[...] += _dot(act.astype(BF16), wd_ref[...])

        @pl.when(f == last_f)
        def _():
            o_ref[...] = acc_ref[...].astype(o_ref.dtype)

    @pl.when(jnp.logical_not(active) & (f == last_f))
    def _():
        o_ref[...] = jnp.zeros_like(o_ref)


def grouped_ffn(xs, w_gu, w_down, layer, tile_expert, n_active, *, f_tile):
    rows, d = xs.shape
    ff = w_down.shape[2]
    nf = ff // f_tile

    def f_idx(j, f, na):
        return jnp.where(j < na[0], f, nf - 1)

    return pl.pallas_call(
        _grouped_ffn_kernel,
        grid_spec=pltpu.PrefetchScalarGridSpec(
            num_scalar_prefetch=2,
            grid=(rows // FFN_ROWS, nf),
            in_specs=[
                pl.BlockSpec((FFN_ROWS, d), lambda j, f, te, na: (jnp.minimum(j, na[0] - 1), 0)),
                pl.BlockSpec((None, None, d, f_tile),
                             lambda j, f, te, na: (layer, te[j], 0, f_idx(j, f, na))),
                pl.BlockSpec((None, None, d, f_tile),
                             lambda j, f, te, na: (layer, te[j], 0, nf + f_idx(j, f, na))),
                pl.BlockSpec((None, None, f_tile, d),
                             lambda j, f, te, na: (layer, te[j], f_idx(j, f, na), 0)),
            ],
            out_specs=pl.BlockSpec((FFN_ROWS, d), lambda j, f, te, na: (j, 0)),
            scratch_shapes=[pltpu.VMEM((FFN_ROWS, d), F32)],
        ),
        out_shape=jax.ShapeDtypeStruct((rows, d), BF16),
        compiler_params=_params("arbitrary", "arbitrary"),
    )(tile_expert, n_active, xs, w_gu, w_gu, w_down)


def _combine_kernel(rs_ref, n_ref, x_ref, info_ref, fg_ref, y_ref, o_ref, ybuf_ref, sem,
                    *, n_experts, final_norm):
    i = pl.program_id(0)
    slot = i % 2
    rows, tm = ybuf_ref.shape[1], x_ref.shape[0]

    def copies(tile, slot, wait):
        def make_copy(slab_row, buffer_row, size):
            return pltpu.make_async_copy(y_ref.at[pl.ds(buffer_row, size)],
                                         ybuf_ref.at[slot, pl.ds(slab_row, size)], sem.at[slot])
        _group_copies(n_ref, rs_ref, tile, n_experts, make_copy, wait)

    @pl.when(i == 0)
    def _():
        ybuf_ref[...] = jnp.zeros_like(ybuf_ref)
        copies(i, slot, wait=False)

    @pl.when(i + 1 < pl.num_programs(0))
    def _():
        copies(i + 1, 1 - slot, wait=False)

    copies(i, slot, wait=True)
    y = ybuf_ref[slot]
    info = info_ref[...]
    r = lax.broadcasted_iota(jnp.int32, (tm, rows), 1).astype(F32)
    pick_lo = (r == info[:, 0:1]).astype(BF16)
    pick_hi = (r == info[:, 1:2]).astype(BF16)
    out = x_ref[...] + info[:, 2:3] * _dot(pick_lo, y) + info[:, 3:4] * _dot(pick_hi, y)
    o_ref[...] = _rms(out, fg_ref[...]) if final_norm else out


def combine(x, info, y, row_start, n_rows, final_gain, *, n_experts, final_norm):
    t, d = x.shape
    n_tiles = t // TOK_TILE
    return pl.pallas_call(
        functools.partial(_combine_kernel, n_experts=n_experts, final_norm=final_norm),
        grid_spec=pltpu.PrefetchScalarGridSpec(
            num_scalar_prefetch=2,
            grid=(n_tiles,),
            in_specs=[
                pl.BlockSpec((TOK_TILE, d), lambda i, rs, n: (i, 0)),
                pl.BlockSpec((TOK_TILE, LANES), lambda i, rs, n: (i, 0)),
                pl.BlockSpec((1, d), lambda i, rs, n: (0, 0)),
                pl.BlockSpec(memory_space=pl.ANY),
            ],
            out_specs=pl.BlockSpec((TOK_TILE, d), lambda i, rs, n: (i, 0)),
            scratch_shapes=[pltpu.VMEM((2, _slab_rows(n_experts), d), BF16),
                            pltpu.SemaphoreType.DMA((2,))],
        ),
        out_shape=jax.ShapeDtypeStruct((t, d), F32),
        compiler_params=_params("arbitrary"),
    )(row_start, n_rows, x, info, final_gain.reshape(1, d), y)


def _ffn_kernel(x_ref, a_ref, wo_ref, g_ref, wg_ref, wu_ref, wd_ref, o_ref):
    x1 = x_ref[...] + _dot(a_ref[...], wo_ref[...])
    h = _rms(x1, g_ref[...]).astype(BF16)
    act = _silu(_dot(h, wg_ref[...])) * _dot(h, wu_ref[...])
    o_ref[...] = x1 + _dot(act.astype(BF16), wd_ref[...])


def ffn(x, a, w_o, gain, w_gu, w_down):
    t, d = x.shape
    ff = w_down.shape[0]
    once = pl.Buffered(1)
    rows = pl.BlockSpec((FFN_DENSE_ROWS, d), lambda i: (i, 0))
    return pl.pallas_call(
        _ffn_kernel,
        grid=(t // FFN_DENSE_ROWS,),
        in_specs=[
            rows,
            rows,
            pl.BlockSpec((d, d), lambda i: (0, 0), pipeline_mode=once),
            pl.BlockSpec((1, d), lambda i: (0, 0)),
            pl.BlockSpec((d, ff), lambda i: (0, 0), pipeline_mode=once),
            pl.BlockSpec((d, ff), lambda i: (0, 1), pipeline_mode=once),
            pl.BlockSpec((ff, d), lambda i: (0, 0), pipeline_mode=once),
        ],
        out_specs=rows,
        out_shape=jax.ShapeDtypeStruct((t, d), F32),
        compiler_params=_params("parallel"),
    )(x, a, w_o, gain.reshape(1, d), w_gu, w_gu, w_down)


def _rmsnorm_kernel(x_ref, g_ref, o_ref):
    o_ref[...] = _rms(x_ref[...], g_ref[...])


def rmsnorm(x, gain):
    t, d = x.shape
    return pl.pallas_call(
        _rmsnorm_kernel,
        grid=(t // ROW_TILE,),
        in_specs=[pl.BlockSpec((ROW_TILE, d), lambda i: (i, 0)),
                  pl.BlockSpec((1, d), lambda i: (0, 0))],
        out_specs=pl.BlockSpec((ROW_TILE, d), lambda i: (i, 0)),
        out_shape=jax.ShapeDtypeStruct((t, d), F32),
        compiler_params=_params("parallel"),
    )(x, gain.reshape(1, d))


def _mxu_tile(n, target):
    best = None
    for cand in range(MXU_WIDTH, target + 1, MXU_WIDTH):
        if n % cand == 0:
            best = cand
    assert best is not None, (n, target)
    return best


def moe_layer(x, a, w_o, gain, w_router, w_gu, w_down, layer, final_gain, *, final_norm):
    t, d = x.shape
    n_experts = w_router.shape[1]
    n_tiles = t // TOK_TILE
    x, h, info, info_t, cnt = router(x, a, w_o, gain, w_router)
    cnt = cnt.reshape(n_tiles, LANES)[:, :n_experts]
    n_rows = (cnt + ROW_ALIGN - 1) // ROW_ALIGN * ROW_ALIGN
    seg_rows = jnp.sum(n_rows, axis=0)
    seg_cap = (seg_rows + FFN_ROWS - 1) // FFN_ROWS * FFN_ROWS
    seg_end = jnp.cumsum(seg_cap)
    row_start = (seg_end - seg_cap)[None, :] + jnp.cumsum(n_rows, axis=0) - n_rows
    max_rows = (MOE_TOPK * t + n_tiles * n_experts * (ROW_ALIGN - 1)
                + n_experts * (FFN_ROWS - ROW_ALIGN))
    total_tiles = -(-max_rows // FFN_ROWS)
    n_active = (seg_end[-1:] // FFN_ROWS).astype(jnp.int32)
    tile_expert = jnp.minimum(
        jnp.searchsorted(seg_end, jnp.arange(total_tiles, dtype=jnp.int32) * FFN_ROWS, side="right"),
        n_experts - 1).astype(jnp.int32)
    row_start = row_start.reshape(-1).astype(jnp.int32)
    n_rows = n_rows.reshape(-1).astype(jnp.int32)
    total_rows = total_tiles * FFN_ROWS
    pad_start = jnp.concatenate([seg_end - seg_cap + seg_rows, seg_end[-1:]]).astype(jnp.int32)
    pad_rows = jnp.concatenate([seg_cap - seg_rows,
                                (total_rows - seg_end[-1:]) // (FFN_ROWS // 2)]).astype(jnp.int32)
    xs = dispatch(h, info_t, row_start, n_rows, pad_start, pad_rows,
                  total_rows=total_rows, n_experts=n_experts)
    y = grouped_ffn(xs, w_gu, w_down, layer, tile_expert, n_active,
                    f_tile=_mxu_tile(w_down.shape[2], 2048))
    return combine(x, info, y, row_start, n_rows, final_gain,
                   n_experts=n_experts, final_norm=final_norm)


def kernel(x, attn_norm, ffn_norm, hgrn_w_in, hgrn_lb_logits, hgrn_out_norm, hgrn_w_o, kv_norm, w_kv, moba_w_q, moba_w_o, ffn_w_gu, ffn_w_down, moe_router, moe_w_gu, moe_w_down, final_norm):
    batch, seq, d = x.shape
    depth = attn_norm.shape[0]
    n_a = hgrn_w_in.shape[0]
    x = x.reshape(batch * seq, d)
    k = v_t = k_mean = None
    moe_gu = moe_w_gu.astype(BF16)
    moe_down = moe_w_down.astype(BF16)
    for layer in range(depth):
        if layer < n_a:
            o = hgrn_layer(x, attn_norm[layer], hgrn_w_in[layer].astype(BF16), hgrn_lb_logits,
                           hgrn_out_norm[layer], layer=layer, batch=batch, seq=seq)
            w_o = hgrn_w_o[layer].astype(BF16)
        else:
            if layer == n_a:
                k, v_t, k_mean = shared_kv(x, kv_norm, w_kv[:, :d].astype(BF16),
                                           w_kv[:, d:].T.astype(BF16))
            j = layer - n_a
            q = norm_matmul(x, attn_norm[layer], moba_w_q[j].astype(BF16), F32)
            o = moba_attention(q, k, v_t, k_mean, batch=batch, seq=seq)
            w_o = moba_w_o[j].astype(BF16)
        j = layer // 2
        if layer % 2 == 0:
            x = ffn(x, o, w_o, ffn_norm[layer], ffn_w_gu[j].astype(BF16),
                    ffn_w_down[j].astype(BF16))
        else:
            last = layer == depth - 1
            x = moe_layer(x, o, w_o, ffn_norm[layer], moe_router[j], moe_gu, moe_down, j,
                          final_norm, final_norm=last)
    if depth % 2 == 1:
        x = rmsnorm(x, final_norm)
    return x.reshape(batch, seq, d)
```

```python
import functools

import jax
import jax.numpy as jnp
from jax import lax
from jax.experimental import pallas as pl
from jax.experimental.pallas import tpu as pltpu

F32 = jnp.float32
BF16 = jnp.bfloat16

EPS = 1e-6
HEAD_DIM = 128
HGRN_CHUNK = 64
MOBA_BLOCK = 256
MOBA_TOPK = 3
MOE_TOPK = 2
MOBA_HEADS_PER_STEP = 2
LOG2E = 1.4426950408889634
LANES = 128
MXU_WIDTH = 256
FFN_DENSE_ROWS = 256
VMEM_LIMIT = 56 * 1024 * 1024

HGRN_TILE = 512
HGRN_HEADS_PER_STEP = 2
ROW_TILE = 512
TOK_TILE = 512
FFN_ROWS = 512
ROW_ALIGN = 16
EXPERT_ROWS = 8


def _params(*semantics):
    return pltpu.CompilerParams(dimension_semantics=semantics, vmem_limit_bytes=VMEM_LIMIT)


def _rms(x, gain):
    return x * lax.rsqrt(jnp.mean(x * x, axis=-1, keepdims=True) + EPS) * gain


def _dot(a, b):
    return jnp.dot(a, b, preferred_element_type=F32)


def _dot_nt(a, b):
    return lax.dot_general(a, b, (((1,), (1,)), ((), ())), preferred_element_type=F32)


def _dot_tn(a, b):
    return lax.dot_general(a, b, (((0,), (0,)), ((), ())), preferred_element_type=F32)


def _silu(x):
    half = 0.5 * x
    return half + half * jnp.tanh(half)


def _norm_matmul_kernel(x_ref, g_ref, w_ref, o_ref):
    h = _rms(x_ref[...], g_ref[...]).astype(BF16)
    o_ref[...] = _dot(h, w_ref[...]).astype(o_ref.dtype)


def norm_matmul(x, gain, w, out_dtype):
    t, d = x.shape
    n = w.shape[1]
    return pl.pallas_call(
        _norm_matmul_kernel,
        grid=(t // ROW_TILE,),
        in_specs=[
            pl.BlockSpec((ROW_TILE, d), lambda i: (i, 0)),
            pl.BlockSpec((1, d), lambda i: (0, 0)),
            pl.BlockSpec((d, n), lambda i: (0, 0)),
        ],
        out_specs=pl.BlockSpec((ROW_TILE, n), lambda i: (i, 0)),
        out_shape=jax.ShapeDtypeStruct((t, n), out_dtype),
        compiler_params=_params("parallel"),
    )(x, gain.reshape(1, d), w)


def _kv_kernel(x_ref, g_ref, wk_ref, wvt_ref, k_ref, vt_ref, km_ref):
    h = _rms(x_ref[...], g_ref[...]).astype(BF16)
    k = _dot(h, wk_ref[...])
    k_ref[...] = k.astype(k_ref.dtype)
    vt_ref[0] = _dot_nt(wvt_ref[...], h).astype(vt_ref.dtype)
    km_ref[0] = jnp.mean(k, axis=0, keepdims=True)


def shared_kv(x, gain, w_k, w_v_t):
    t, d = x.shape
    nblk = t // MOBA_BLOCK
    return pl.pallas_call(
        _kv_kernel,
        grid=(nblk,),
        in_specs=[
            pl.BlockSpec((MOBA_BLOCK, d), lambda i: (i, 0)),
            pl.BlockSpec((1, d), lambda i: (0, 0)),
            pl.BlockSpec((d, d), lambda i: (0, 0)),
            pl.BlockSpec((d, d), lambda i: (0, 0)),
        ],
        out_specs=[
            pl.BlockSpec((MOBA_BLOCK, d), lambda i: (i, 0)),
            pl.BlockSpec((1, d, MOBA_BLOCK), lambda i: (i, 0, 0)),
            pl.BlockSpec((1, 1, d), lambda i: (i, 0, 0)),
        ],
        out_shape=[
            jax.ShapeDtypeStruct((t, d), BF16),
            jax.ShapeDtypeStruct((nblk, d, MOBA_BLOCK), BF16),
            jax.ShapeDtypeStruct((nblk, 1, d), F32),
        ],
        compiler_params=_params("parallel"),
    )(x, gain.reshape(1, d), w_k, w_v_t)


def _hgrn_heads(q_ref, f_ref, i_ref, g_ref, lbl_ref, gain_ref, o_ref, st_ref, *, layer, first):
    c = HGRN_CHUNK
    dk = HEAD_DIM
    heads = q_ref.shape[1] // dk
    per_head = q_ref.shape[0] // c
    n_chunks = heads * per_head
    group = MOBA_BLOCK // c

    def wide(ref):
        return jnp.concatenate([ref[n * c:(n + 1) * c, hh * dk:(hh + 1) * dk]
                                for hh in range(heads) for n in range(per_head)], axis=1)

    def tall(ref):
        return jnp.concatenate([ref[:, hh * dk:(hh + 1) * dk] for hh in range(heads)], axis=0)

    def chunk(x, n):
        return x[:, n * dk:(n + 1) * dk]

    def rows_of(x, n0, n1):
        return jnp.concatenate([chunk(x, n) for n in range(n0, n1)], axis=0)

    lanes = slice(first * dk, (first + heads) * dk)
    fz = wide(f_ref)
    t = jnp.exp(-jnp.abs(fz))
    log_sig = jnp.minimum(fz, 0.0) - jnp.log(1.0 + t)
    sig_neg = jnp.exp(log_sig - fz)
    if layer > 0:
        logits = lbl_ref[:, lanes]
        e = jnp.exp(logits - jnp.max(logits, axis=0, keepdims=True))
        p = e / jnp.sum(e, axis=0, keepdims=True)
        lb = jnp.sum(p[1:layer + 1], axis=0, keepdims=True)
        lb = jnp.concatenate([chunk(lb, hh) for hh in range(heads) for _ in range(per_head)],
                             axis=1)
        log_lb = jnp.log(lb)
        y = jnp.log(1.0 - lb) + log_sig
        log_f = jnp.maximum(log_lb, y) + jnp.log(1.0 + jnp.exp(-jnp.abs(log_lb - y)))
        k = (1.0 - lb) * sig_neg
    else:
        log_f = log_sig
        k = sig_neg

    tril = (lax.broadcasted_iota(jnp.int32, (c, c), 1)
            <= lax.broadcasted_iota(jnp.int32, (c, c), 0)).astype(BF16)
    hi = log_f.astype(BF16)
    rest = log_f - hi.astype(F32)
    mid = rest.astype(BF16)
    lo = (rest - mid.astype(F32)).astype(BF16)
    b = _dot(tril, hi) + _dot(tril, mid) + _dot(tril, lo)
    b_mid = b[c // 2 - 1:c // 2, :]
    b_last = b[c - 1:c, :]

    q = _silu(wide(q_ref))
    qd = q * jnp.exp(b - b_mid)
    kd = k * jnp.exp(b_mid - b)
    qe = (qd * jnp.exp(b_mid)).astype(BF16)
    kl = (kd * jnp.exp(b_last - b_mid)).astype(BF16)
    qd = qd.astype(BF16)
    kd = kd.astype(BF16)
    v = wide(i_ref).astype(BF16)
    decay = jnp.exp(b_last)

    gr = group * c
    r_i = lax.broadcasted_iota(jnp.int32, (gr, gr), 0)
    c_i = lax.broadcasted_iota(jnp.int32, (gr, gr), 1)
    keep = (c_i <= r_i) & ((r_i // c) == (c_i // c))
    groups = range(0, n_chunks, group)
    scores = [_dot_nt(rows_of(qd, n0, n0 + group), rows_of(kd, n0, n0 + group)) for n0 in groups]

    grow = [_dot_tn(chunk(v, n), chunk(kl, n)) for n in range(n_chunks)]
    intra = [_dot(jnp.where(keep, a, 0.0).astype(BF16), rows_of(v, n0, n0 + group))
             for a, n0 in zip(scores, groups)]
    before = []
    for hh in range(heads):
        st = st_ref[first + hh]
        for n in range(hh * per_head, (hh + 1) * per_head):
            before.append(st.astype(BF16))
            st = chunk(decay, n) * st + grow[n]
        st_ref[first + hh] = st
    inter = [_dot_nt(chunk(qe, n), before[n]) for n in range(n_chunks)]
    o = jnp.concatenate(intra, axis=0) + jnp.concatenate(inter, axis=0)

    o = o * lax.rsqrt(jnp.mean(o * o, axis=-1, keepdims=True) + EPS)
    o = o * _silu(tall(g_ref))
    rows = q_ref.shape[0]
    for hh in range(heads):
        cols = slice((first + hh) * dk, (first + hh + 1) * dk)
        o_ref[:, cols] = (o[hh * rows:(hh + 1) * rows] * gain_ref[:, cols]).astype(o_ref.dtype)


def _hgrn_layer_kernel(x_ref, ng_ref, w_ref, lbl_ref, gain_ref, o_ref, st_ref, *, layer):
    @pl.when(pl.program_id(1) == 0)
    def _():
        st_ref[...] = jnp.zeros_like(st_ref)

    d = x_ref.shape[1]
    hp = HGRN_HEADS_PER_STEP
    width = hp * HEAD_DIM
    h = _rms(x_ref[...], ng_ref[...]).astype(BF16)

    def project(group):
        return [_dot(h, w_ref[:, part * d + group * width:part * d + (group + 1) * width])
                for part in range(4)]

    n_groups = d // width
    nxt = project(0)
    for group in range(n_groups):
        cur = nxt
        if group + 1 < n_groups:
            nxt = project(group + 1)
        _hgrn_heads(*cur, lbl_ref, gain_ref, o_ref, st_ref, layer=layer, first=group * hp)


def hgrn_layer(x, norm_gain, w_in, lb_logits, out_gain, *, layer, batch, seq):
    t, d = x.shape
    heads = d // HEAD_DIM
    tiles = seq // HGRN_TILE
    n_layers = lb_logits.shape[0]
    return pl.pallas_call(
        functools.partial(_hgrn_layer_kernel, layer=layer),
        grid=(batch, tiles),
        in_specs=[
            pl.BlockSpec((HGRN_TILE, d), lambda b, s: (b * tiles + s, 0)),
            pl.BlockSpec((1, d), lambda b, s: (0, 0)),
            pl.BlockSpec((d, 4 * d), lambda b, s: (0, 0), pipeline_mode=pl.Buffered(1)),
            pl.BlockSpec((n_layers, d), lambda b, s: (0, 0)),
            pl.BlockSpec((1, d), lambda b, s: (0, 0)),
        ],
        out_specs=pl.BlockSpec((HGRN_TILE, d), lambda b, s: (b * tiles + s, 0)),
        out_shape=jax.ShapeDtypeStruct((t, d), BF16),
        scratch_shapes=[pltpu.VMEM((heads, HEAD_DIM, HEAD_DIM), F32)],
        compiler_params=_params("parallel", "arbitrary"),
    )(x, norm_gain.reshape(1, d), w_in, lb_logits, out_gain.reshape(1, d))


def _moba_kernel(q_ref, k_ref, vt_ref, km_ref, slope_ref, o_ref):
    bs = MOBA_BLOCK
    dh = HEAD_DIM
    n_blk = km_ref.shape[1]
    heads = q_ref.shape[1] // dh
    scale2 = HEAD_DIM ** -0.5 * LOG2E
    rel = (lax.broadcasted_iota(jnp.int32, (bs, bs), 1)
           - lax.broadcasted_iota(jnp.int32, (bs, bs), 0))
    rel_f = rel.astype(F32)

    def gate_rank(q, km, n):
        gate = lax.dot_general(km, q, (((1,), (1,)), ((), ())),
                               precision=lax.Precision.HIGHEST, preferred_element_type=F32)
        blk = lax.broadcasted_iota(jnp.int32, gate.shape, 0)
        rank = jnp.zeros(gate.shape, F32)
        for m in range(n):
            g_m = gate[m:m + 1, :]
            rank = rank + ((g_m > gate) | ((g_m == gate) & (m < blk))).astype(F32)
        return rank

    def scores(n, hh):
        cols = slice(hh * dh, (hh + 1) * dh)
        q = q_ref[n * bs:(n + 1) * bs, cols]
        q16 = (q * scale2).astype(BF16)
        rank = gate_rank(q, km_ref[0][:, cols], n) if n > MOBA_TOPK else None
        return _dot_nt(k_ref[0:(n + 1) * bs, cols], q16), rank

    def probabilities(n, hh, s_all, rank):
        slope2 = slope_ref[hh][0:1, 0:1] * LOG2E
        bias = -slope2 * rel_f
        us, shifts, maxes = [], [], []
        for kb in range(n + 1):
            u = s_all[kb * bs:(kb + 1) * bs] + bias
            if kb == n:
                u = jnp.where(rel >= 0, u, -jnp.inf)
            elif rank is not None:
                u = jnp.where(rank[kb:kb + 1, :] < MOBA_TOPK, u, -jnp.inf)
            shift = slope2 * float((n - kb) * bs)
            us.append(u)
            shifts.append(shift)
            maxes.append(jnp.max(u, axis=0, keepdims=True) - shift)
        m = functools.reduce(jnp.maximum, maxes)
        ps = [jnp.exp2(u - (m + shift)) for u, shift in zip(us, shifts)]
        return jnp.concatenate(ps, axis=0).astype(BF16)

    def output(n, hh, p_all):
        cols = slice(hh * dh, (hh + 1) * dh)
        vt = jnp.concatenate([vt_ref[kb, cols, :] for kb in range(n + 1)], axis=1)
        ones = jnp.ones((ROW_ALIGN, vt.shape[1]), BF16)
        pv = _dot(jnp.concatenate([vt, ones], axis=0), p_all)
        o_ref[n * bs:(n + 1) * bs, cols] = (pv[0:dh] / pv[dh:dh + 1]).T.astype(o_ref.dtype)

    scored = [scores(0, hh) for hh in range(heads)]
    for n in range(n_blk):
        probs = [probabilities(n, hh, *scored[hh]) for hh in range(heads)]
        if n + 1 < n_blk:
            scored = [scores(n + 1, hh) for hh in range(heads)]
        for hh in range(heads):
            output(n, hh, probs[hh])


def moba_attention(q, k, v_t, k_mean, *, batch, seq):
    t, d = q.shape
    heads = d // HEAD_DIM
    n_blk = seq // MOBA_BLOCK
    hp = MOBA_HEADS_PER_STEP
    width = hp * HEAD_DIM
    slopes = (2.0 ** (-8.0 * jnp.arange(1, heads + 1, dtype=F32) / heads))
    slopes = jnp.broadcast_to(slopes[:, None, None], (heads, 1, LANES))
    return pl.pallas_call(
        _moba_kernel,
        grid=(batch, heads // hp),
        in_specs=[
            pl.BlockSpec((seq, width), lambda b, h: (b, h)),
            pl.BlockSpec((seq, width), lambda b, h: (b, h)),
            pl.BlockSpec((n_blk, width, MOBA_BLOCK), lambda b, h: (b, h, 0)),
            pl.BlockSpec((1, n_blk, width), lambda b, h: (b, 0, h)),
            pl.BlockSpec((hp, 1, LANES), lambda b, h: (h, 0, 0)),
        ],
        out_specs=pl.BlockSpec((seq, width), lambda b, h: (b, h)),
        out_shape=jax.ShapeDtypeStruct((t, d), BF16),
        compiler_params=_params("parallel", "parallel"),
    )(q, k, v_t, k_mean.reshape(batch, n_blk, d), slopes)


def _router_kernel(x_ref, a_ref, wo_ref, g_ref, wt_ref, x1_ref, h_ref, info_ref, info_t_ref,
                   cnt_ref, *, n_experts):
    x1 = x_ref[...] + _dot(a_ref[...], wo_ref[...])
    x1_ref[...] = x1
    h = _rms(x1, g_ref[...])
    h_hi = h.astype(BF16)
    h_ref[...] = h_hi
    h_lo = (h - h_hi.astype(F32)).astype(BF16)
    wt = wt_ref[...]
    wt_hi = wt.astype(BF16)
    wt_lo = (wt - wt_hi.astype(F32)).astype(BF16)
    by_hi = _dot_nt(jnp.concatenate([wt_hi, wt_lo], axis=0), h_hi)
    by_lo = _dot_nt(wt_hi, h_lo)
    rows = EXPERT_ROWS
    logits = by_hi[0:rows] + by_hi[ROW_ALIGN:ROW_ALIGN + rows] + by_lo[0:rows]
    tm = logits.shape[1]
    sub = lax.broadcasted_iota(jnp.int32, (rows, tm), 0)
    neg = -jnp.inf
    lg = jnp.where(sub < n_experts, logits, neg)
    m1 = jnp.max(lg, axis=0, keepdims=True)
    i1 = jnp.min(jnp.where(lg == m1, sub, rows), axis=0, keepdims=True)
    lg2 = jnp.where(sub == i1, neg, lg)
    m2 = jnp.max(lg2, axis=0, keepdims=True)
    i2 = jnp.min(jnp.where(lg2 == m2, sub, rows), axis=0, keepdims=True)
    e2 = jnp.exp(m2 - m1)
    w1 = 1.0 / (1.0 + e2)
    w2 = e2 / (1.0 + e2)

    routed = ((sub == i1) | (sub == i2)).astype(F32)
    routed_pad = jnp.concatenate([routed, jnp.zeros((LANES - rows, tm), F32)], axis=0).astype(BF16)
    cnt_ref[0] = _dot_nt(jnp.ones((ROW_ALIGN, tm), BF16), routed_pad)[0:1, :].astype(jnp.int32)
    later = (lax.broadcasted_iota(jnp.int32, (tm, tm), 0)
             < lax.broadcasted_iota(jnp.int32, (tm, tm), 1)).astype(BF16)
    rank = _dot(routed_pad[0:ROW_ALIGN], later)[0:rows]
    cnt = jnp.sum(routed, axis=1, keepdims=True)
    group_rows = jnp.floor((cnt + (ROW_ALIGN - 1)) * (1.0 / ROW_ALIGN)) * ROW_ALIGN
    expert = lax.broadcasted_iota(jnp.int32, (rows, 1), 0)
    group_start = jnp.zeros((rows, 1), F32)
    for e in range(n_experts - 1):
        group_start = group_start + jnp.where(expert > e, group_rows[e:e + 1, :], 0.0)
    slab_row = group_start + rank
    first = i1 < i2
    i_lo = jnp.where(first, i1, i2)
    i_hi = jnp.where(first, i2, i1)
    row_lo = jnp.sum(jnp.where(sub == i_lo, slab_row, 0.0), axis=0, keepdims=True)
    row_hi = jnp.sum(jnp.where(sub == i_hi, slab_row, 0.0), axis=0, keepdims=True)
    info_t = (jnp.where(sub == 0, row_lo, 0.0) + jnp.where(sub == 1, row_hi, 0.0)
              + jnp.where(sub == 2, jnp.where(first, w1, w2), 0.0)
              + jnp.where(sub == 3, jnp.where(first, w2, w1), 0.0))
    info_t_ref[0] = info_t
    info_ref[...] = jnp.concatenate([info_t, jnp.zeros((LANES - rows, tm), F32)], axis=0).T


def router(x, a, w_o, gain, w_router):
    t, d = x.shape
    n_experts = w_router.shape[1]
    n_tiles = t // TOK_TILE
    assert n_experts <= EXPERT_ROWS, n_experts
    w_t = jnp.pad(w_router.T, ((0, ROW_ALIGN - n_experts), (0, 0)))
    return pl.pallas_call(
        functools.partial(_router_kernel, n_experts=n_experts),
        grid=(n_tiles,),
        in_specs=[
            pl.BlockSpec((TOK_TILE, d), lambda i: (i, 0)),
            pl.BlockSpec((TOK_TILE, d), lambda i: (i, 0)),
            pl.BlockSpec((d, d), lambda i: (0, 0)),
            pl.BlockSpec((1, d), lambda i: (0, 0)),
            pl.BlockSpec((ROW_ALIGN, d), lambda i: (0, 0)),
        ],
        out_specs=[
            pl.BlockSpec((TOK_TILE, d), lambda i: (i, 0)),
            pl.BlockSpec((TOK_TILE, d), lambda i: (i, 0)),
            pl.BlockSpec((TOK_TILE, LANES), lambda i: (i, 0)),
            pl.BlockSpec((1, EXPERT_ROWS, TOK_TILE), lambda i: (i, 0, 0)),
            pl.BlockSpec((1, 1, LANES), lambda i: (i, 0, 0)),
        ],
        out_shape=[
            jax.ShapeDtypeStruct((t, d), F32),
            jax.ShapeDtypeStruct((t, d), BF16),
            jax.ShapeDtypeStruct((t, LANES), F32),
            jax.ShapeDtypeStruct((n_tiles, EXPERT_ROWS, TOK_TILE), F32),
            jax.ShapeDtypeStruct((n_tiles, 1, LANES), jnp.int32),
        ],
        compiler_params=_params("parallel"),
    )(x, a, w_o, gain.reshape(1, d), w_t)


def _slab_rows(n_experts):
    bound = MOE_TOPK * TOK_TILE + n_experts * (ROW_ALIGN - 1)
    return -(-bound // LANES) * LANES


def _copy_rows(n, max_size, make_copy, wait):
    off = jnp.int32(0)
    size = max_size
    while size >= ROW_ALIGN:
        @pl.when((n & size) != 0)
        def _(off=off, size=size):
            cp = make_copy(off, size)
            if wait:
                cp.wait()
            else:
                cp.start()
        off = off + (n & size)
        size //= 2


def _group_copies(n_ref, rs_ref, tile, n_experts, make_copy, wait):
    slab_row = jnp.int32(0)
    for e in range(n_experts):
        n = n_ref[tile * n_experts + e]
        base = rs_ref[tile * n_experts + e]

        def group_copy(off, size, slab_row=slab_row, base=base):
            return make_copy(pl.multiple_of(slab_row + off, ROW_ALIGN),
                             pl.multiple_of(base + off, ROW_ALIGN), size)

        _copy_rows(n, TOK_TILE, group_copy, wait)
        slab_row = slab_row + n


def _dispatch_kernel(rs_ref, n_ref, ps_ref, pn_ref, h_ref, info_t_ref, xs_ref,
                     slab_ref, zero_ref, sem, zero_sem, *, n_experts):
    i = pl.program_id(0)
    last = pl.num_programs(0) - 1
    slot = i % 2
    rows, tm = slab_ref.shape[1], h_ref.shape[0]
    info_t = info_t_ref[0]
    r = lax.broadcasted_iota(jnp.int32, (rows, tm), 0).astype(F32)
    onehot = ((r == info_t[0:1, :]) | (r == info_t[1:2, :])).astype(BF16)
    slab_ref[slot] = _dot(onehot, h_ref[...]).astype(BF16)

    def copies(tile, slot, wait):
        def make_copy(slab_row, buffer_row, size):
            return pltpu.make_async_copy(slab_ref.at[slot, pl.ds(slab_row, size)],
                                         xs_ref.at[pl.ds(buffer_row, size)], sem.at[slot])
        _group_copies(n_ref, rs_ref, tile, n_experts, make_copy, wait)

    def pad_copies(wait):
        for e in range(n_experts):
            def make_copy(off, size, e=e):
                start = pl.multiple_of(ps_ref[e] + off, ROW_ALIGN)
                return pltpu.make_async_copy(zero_ref.at[pl.ds(0, size)],
                                             xs_ref.at[pl.ds(start, size)], zero_sem)
            _copy_rows(pn_ref[e], zero_ref.shape[0], make_copy, wait)

        def idle_chunk(j, carry):
            start = pl.multiple_of(ps_ref[n_experts] + j * zero_ref.shape[0], ROW_ALIGN)
            cp = pltpu.make_async_copy(zero_ref, xs_ref.at[pl.ds(start, zero_ref.shape[0])],
                                       zero_sem)
            if wait:
                cp.wait()
            else:
                cp.start()
            return carry
        lax.fori_loop(0, pn_ref[n_experts], idle_chunk, 0)

    @pl.when(i > 0)
    def _():
        copies(i - 1, 1 - slot, wait=True)

    copies(i, slot, wait=False)

    @pl.when(i == last)
    def _():
        zero_ref[...] = jnp.zeros_like(zero_ref)
        pad_copies(wait=False)
        copies(i, slot, wait=True)
        pad_copies(wait=True)


def dispatch(h, info_t, row_start, n_rows, pad_start, pad_rows, *, total_rows, n_experts):
    t, d = h.shape
    n_tiles = t // TOK_TILE
    return pl.pallas_call(
        functools.partial(_dispatch_kernel, n_experts=n_experts),
        grid_spec=pltpu.PrefetchScalarGridSpec(
            num_scalar_prefetch=4,
            grid=(n_tiles,),
            in_specs=[
                pl.BlockSpec((TOK_TILE, d), lambda i, *_: (i, 0)),
                pl.BlockSpec((1, 8, TOK_TILE), lambda i, *_: (i, 0, 0)),
            ],
            out_specs=pl.BlockSpec(memory_space=pl.ANY),
            scratch_shapes=[pltpu.VMEM((2, _slab_rows(n_experts), d), BF16),
                            pltpu.VMEM((FFN_ROWS // 2, d), BF16),
                            pltpu.SemaphoreType.DMA((2,)),
                            pltpu.SemaphoreType.DMA],
        ),
        out_shape=jax.ShapeDtypeStruct((total_rows, d), BF16),
        compiler_params=_params("arbitrary"),
    )(row_start, n_rows, pad_start, pad_rows, h, info_t)


def _grouped_ffn_kernel(te_ref, na_ref, x_ref, wg_ref, wu_ref, wd_ref, o_ref, acc_ref):
    j = pl.program_id(0)
    f = pl.program_id(1)
    last_f = pl.num_programs(1) - 1
    active = j < na_ref[0]

    @pl.when(active)
    def _():
        @pl.when(f == 0)
        def _():
            acc_ref[...] = jnp.zeros_like(acc_ref)

        x = x_ref[...]
        act = _silu(_dot(x, wg_ref[...])) * _dot(x, wu_ref[...])
        acc_ref[...] += _dot(act.astype(BF16), wd_ref[...])

        @pl.when(f == last_f)
        def _():
            o_ref[...] = acc_ref[...].astype(o_ref.dtype)

    @pl.when(jnp.logical_not(active) & (f == last_f))
    def _():
        o_ref[...] = jnp.zeros_like(o_ref)


def grouped_ffn(xs, w_gu, w_down, layer, tile_expert, n_active, *, f_tile):
    rows, d = xs.shape
    ff = w_down.shape[2]
    nf = ff // f_tile

    def f_idx(j, f, na):
        return jnp.where(j < na[0], f, nf - 1)

    return pl.pallas_call(
        _grouped_ffn_kernel,
        grid_spec=pltpu.PrefetchScalarGridSpec(
            num_scalar_prefetch=2,
            grid=(rows // FFN_ROWS, nf),
            in_specs=[
                pl.BlockSpec((FFN_ROWS, d), lambda j, f, te, na: (jnp.minimum(j, na[0] - 1), 0)),
                pl.BlockSpec((None, None, d, f_tile),
                             lambda j, f, te, na: (layer, te[j], 0, f_idx(j, f, na))),
                pl.BlockSpec((None, None, d, f_tile),
                             lambda j, f, te, na: (layer, te[j], 0, nf + f_idx(j, f, na))),
                pl.BlockSpec((None, None, f_tile, d),
                             lambda j, f, te, na: (layer, te[j], f_idx(j, f, na), 0)),
            ],
            out_specs=pl.BlockSpec((FFN_ROWS, d), lambda j, f, te, na: (j, 0)),
            scratch_shapes=[pltpu.VMEM((FFN_ROWS, d), F32)],
        ),
        out_shape=jax.ShapeDtypeStruct((rows, d), BF16),
        compiler_params=_params("arbitrary", "arbitrary"),
    )(tile_expert, n_active, xs, w_gu, w_gu, w_down)


def _combine_kernel(rs_ref, n_ref, x_ref, info_ref, fg_ref, y_ref, o_ref, ybuf_ref, sem,
                    *, n_experts, final_norm):
    i = pl.program_id(0)
    slot = i % 2
    rows, tm = ybuf_ref.shape[1], x_ref.shape[0]

    def copies(tile, slot, wait):
        def make_copy(slab_row, buffer_row, size):
            return pltpu.make_async_copy(y_ref.at[pl.ds(buffer_row, size)],
                                         ybuf_ref.at[slot, pl.ds(slab_row, size)], sem.at[slot])
        _group_copies(n_ref, rs_ref, tile, n_experts, make_copy, wait)

    @pl.when(i == 0)
    def _():
        ybuf_ref[...] = jnp.zeros_like(ybuf_ref)
        copies(i, slot, wait=False)

    @pl.when(i + 1 < pl.num_programs(0))
    def _():
        copies(i + 1, 1 - slot, wait=False)

    copies(i, slot, wait=True)
    y = ybuf_ref[slot]
    info = info_ref[...]
    r = lax.broadcasted_iota(jnp.int32, (tm, rows), 1).astype(F32)
    pick_lo = (r == info[:, 0:1]).astype(BF16)
    pick_hi = (r == info[:, 1:2]).astype(BF16)
    out = x_ref[...] + info[:, 2:3] * _dot(pick_lo, y) + info[:, 3:4] * _dot(pick_hi, y)
    o_ref[...] = _rms(out, fg_ref[...]) if final_norm else out


def combine(x, info, y, row_start, n_rows, final_gain, *, n_experts, final_norm):
    t, d = x.shape
    n_tiles = t // TOK_TILE
    return pl.pallas_call(
        functools.partial(_combine_kernel, n_experts=n_experts, final_norm=final_norm),
        grid_spec=pltpu.PrefetchScalarGridSpec(
            num_scalar_prefetch=2,
            grid=(n_tiles,),
            in_specs=[
                pl.BlockSpec((TOK_TILE, d), lambda i, rs, n: (i, 0)),
                pl.BlockSpec((TOK_TILE, LANES), lambda i, rs, n: (i, 0)),
                pl.BlockSpec((1, d), lambda i, rs, n: (0, 0)),
                pl.BlockSpec(memory_space=pl.ANY),
            ],
            out_specs=pl.BlockSpec((TOK_TILE, d), lambda i, rs, n: (i, 0)),
            scratch_shapes=[pltpu.VMEM((2, _slab_rows(n_experts), d), BF16),
                            pltpu.SemaphoreType.DMA((2,))],
        ),
        out_shape=jax.ShapeDtypeStruct((t, d), F32),
        compiler_params=_params("arbitrary"),
    )(row_start, n_rows, x, info, final_gain.reshape(1, d), y)


def _ffn_kernel(x_ref, a_ref, wo_ref, g_ref, wg_ref, wu_ref, wd_ref, o_ref):
    x1 = x_ref[...] + _dot(a_ref[...], wo_ref[...])
    h = _rms(x1, g_ref[...]).astype(BF16)
    act = _silu(_dot(h, wg_ref[...])) * _dot(h, wu_ref[...])
    o_ref[...] = x1 + _dot(act.astype(BF16), wd_ref[...])


def ffn(x, a, w_o, gain, w_gu, w_down):
    t, d = x.shape
    ff = w_down.shape[0]
    once = pl.Buffered(1)
    rows = pl.BlockSpec((FFN_DENSE_ROWS, d), lambda i: (i, 0))
    return pl.pallas_call(
        _ffn_kernel,
        grid=(t // FFN_DENSE_ROWS,),
        in_specs=[
            rows,
            rows,
            pl.BlockSpec((d, d), lambda i: (0, 0), pipeline_mode=once),
            pl.BlockSpec((1, d), lambda i: (0, 0)),
            pl.BlockSpec((d, ff), lambda i: (0, 0), pipeline_mode=once),
            pl.BlockSpec((d, ff), lambda i: (0, 1), pipeline_mode=once),
            pl.BlockSpec((ff, d), lambda i: (0, 0), pipeline_mode=once),
        ],
        out_specs=rows,
        out_shape=jax.ShapeDtypeStruct((t, d), F32),
        compiler_params=_params("parallel"),
    )(x, a, w_o, gain.reshape(1, d), w_gu, w_gu, w_down)


def _rmsnorm_kernel(x_ref, g_ref, o_ref):
    o_ref[...] = _rms(x_ref[...], g_ref[...])


def rmsnorm(x, gain):
    t, d = x.shape
    return pl.pallas_call(
        _rmsnorm_kernel,
        grid=(t // ROW_TILE,),
        in_specs=[pl.BlockSpec((ROW_TILE, d), lambda i: (i, 0)),
                  pl.BlockSpec((1, d), lambda i: (0, 0))],
        out_specs=pl.BlockSpec((ROW_TILE, d), lambda i: (i, 0)),
        out_shape=jax.ShapeDtypeStruct((t, d), F32),
        compiler_params=_params("parallel"),
    )(x, gain.reshape(1, d))


def _mxu_tile(n, target):
    best = None
    for cand in range(MXU_WIDTH, target + 1, MXU_WIDTH):
        if n % cand == 0:
            best = cand
    assert best is not None, (n, target)
    return best


def moe_layer(x, a, w_o, gain, w_router, w_gu, w_down, layer, final_gain, *, final_norm):
    t, d = x.shape
    n_experts = w_router.shape[1]
    n_tiles = t // TOK_TILE
    x, h, info, info_t, cnt = router(x, a, w_o, gain, w_router)
    cnt = cnt.reshape(n_tiles, LANES)[:, :n_experts]
    n_rows = (cnt + ROW_ALIGN - 1) // ROW_ALIGN * ROW_ALIGN
    seg_rows = jnp.sum(n_rows, axis=0)
    seg_cap = (seg_rows + FFN_ROWS - 1) // FFN_ROWS * FFN_ROWS
    seg_end = jnp.cumsum(seg_cap)
    row_start = (seg_end - seg_cap)[None, :] + jnp.cumsum(n_rows, axis=0) - n_rows
    max_rows = (MOE_TOPK * t + n_tiles * n_experts * (ROW_ALIGN - 1)
                + n_experts * (FFN_ROWS - ROW_ALIGN))
    total_tiles = -(-max_rows // FFN_ROWS)
    n_active = (seg_end[-1:] // FFN_ROWS).astype(jnp.int32)
    tile_row = jnp.arange(total_tiles, dtype=jnp.int32) * FFN_ROWS
    tile_expert = jnp.minimum(jnp.sum(seg_end[None, :] <= tile_row[:, None], axis=1),
                              n_experts - 1).astype(jnp.int32)
    row_start = row_start.reshape(-1).astype(jnp.int32)
    n_rows = n_rows.reshape(-1).astype(jnp.int32)
    total_rows = total_tiles * FFN_ROWS
    pad_start = jnp.concatenate([seg_end - seg_cap + seg_rows, seg_end[-1:]]).astype(jnp.int32)
    pad_rows = jnp.concatenate([seg_cap - seg_rows,
                                (total_rows - seg_end[-1:]) // (FFN_ROWS // 2)]).astype(jnp.int32)
    xs = dispatch(h, info_t, row_start, n_rows, pad_start, pad_rows,
                  total_rows=total_rows, n_experts=n_experts)
    y = grouped_ffn(xs, w_gu, w_down, layer, tile_expert, n_active,
                    f_tile=_mxu_tile(w_down.shape[2], 2048))
    return combine(x, info, y, row_start, n_rows, final_gain,
                   n_experts=n_experts, final_norm=final_norm)


def kernel(x, attn_norm, ffn_norm, hgrn_w_in, hgrn_lb_logits, hgrn_out_norm, hgrn_w_o, kv_norm, w_kv, moba_w_q, moba_w_o, ffn_w_gu, ffn_w_down, moe_router, moe_w_gu, moe_w_down, final_norm):
    batch, seq, d = x.shape
    depth = attn_norm.shape[0]
    n_a = hgrn_w_in.shape[0]
    x = x.reshape(batch * seq, d)
    k = v_t = k_mean = None
    moe_gu = moe_w_gu.astype(BF16)
    moe_down = moe_w_down.astype(BF16)
    for layer in range(depth):
        if layer < n_a:
            o = hgrn_layer(x, attn_norm[layer], hgrn_w_in[layer].astype(BF16), hgrn_lb_logits,
                           hgrn_out_norm[layer], layer=layer, batch=batch, seq=seq)
            w_o = hgrn_w_o[layer].astype(BF16)
        else:
            if layer == n_a:
                k, v_t, k_mean = shared_kv(x, kv_norm, w_kv[:, :d].astype(BF16),
                                           w_kv[:, d:].T.astype(BF16))
            j = layer - n_a
            q = norm_matmul(x, attn_norm[layer], moba_w_q[j].astype(BF16), F32)
            o = moba_attention(q, k, v_t, k_mean, batch=batch, seq=seq)
            w_o = moba_w_o[j].astype(BF16)
        j = layer // 2
        if layer % 2 == 0:
            x = ffn(x, o, w_o, ffn_norm[layer], ffn_w_gu[j].astype(BF16),
                    ffn_w_down[j].astype(BF16))
        else:
            last = layer == depth - 1
            x = moe_layer(x, o, w_o, ffn_norm[layer], moe_router[j], moe_gu, moe_down, j,
                          final_norm, final_norm=last)
    if depth % 2 == 1:
        x = rmsnorm(x, final_norm)
    return x.reshape(batch, seq, d)
```

```python
import functools

import jax
import jax.numpy as jnp
from jax import lax
from jax.experimental import pallas as pl
from jax.experimental.pallas import tpu as pltpu

F32 = jnp.float32
BF16 = jnp.bfloat16

EPS = 1e-6
HEAD_DIM = 128
HGRN_CHUNK = 64
MOBA_BLOCK = 256
MOBA_TOPK = 3
MOE_TOPK = 2
MOBA_HEADS_PER_STEP = 2
LOG2E = 1.4426950408889634
LANES = 128
MXU_WIDTH = 256
FFN_DENSE_ROWS = 256
VMEM_LIMIT = 56 * 1024 * 1024

HGRN_TILE = 512
HGRN_HEADS_PER_STEP = 2
ROW_TILE = 512
TOK_TILE = 512
FFN_ROWS = 512
ROW_ALIGN = 16
SUBLANES = 8
EXPERT_ROWS = SUBLANES


def _params(*semantics):
    return pltpu.CompilerParams(dimension_semantics=semantics, vmem_limit_bytes=VMEM_LIMIT)


def _rms(x, gain):
    return x * lax.rsqrt(jnp.mean(x * x, axis=-1, keepdims=True) + EPS) * gain


def _dot(a, b):
    return jnp.dot(a, b, preferred_element_type=F32)


def _dot_nt(a, b):
    return lax.dot_general(a, b, (((1,), (1,)), ((), ())), preferred_element_type=F32)


def _dot_tn(a, b):
    return lax.dot_general(a, b, (((0,), (0,)), ((), ())), preferred_element_type=F32)


def _silu(x):
    half = 0.5 * x
    return half + half * jnp.tanh(half)


def _norm_matmul_kernel(x_ref, g_ref, w_ref, o_ref):
    h = _rms(x_ref[...], g_ref[...]).astype(BF16)
    o_ref[...] = _dot(h, w_ref[...]).astype(o_ref.dtype)


def norm_matmul(x, gain, w, out_dtype):
    t, d = x.shape
    n = w.shape[1]
    return pl.pallas_call(
        _norm_matmul_kernel,
        grid=(t // ROW_TILE,),
        in_specs=[
            pl.BlockSpec((ROW_TILE, d), lambda i: (i, 0)),
            pl.BlockSpec((1, d), lambda i: (0, 0)),
            pl.BlockSpec((d, n), lambda i: (0, 0)),
        ],
        out_specs=pl.BlockSpec((ROW_TILE, n), lambda i: (i, 0)),
        out_shape=jax.ShapeDtypeStruct((t, n), out_dtype),
        compiler_params=_params("parallel"),
    )(x, gain.reshape(1, d), w)


def _kv_kernel(x_ref, g_ref, wk_ref, wvt_ref, k_ref, vt_ref, km_ref):
    h = _rms(x_ref[...], g_ref[...]).astype(BF16)
    k = _dot(h, wk_ref[...])
    k_ref[...] = k.astype(k_ref.dtype)
    vt_ref[0] = _dot_nt(wvt_ref[...], h).astype(vt_ref.dtype)
    km_ref[0] = jnp.mean(k, axis=0, keepdims=True)


def shared_kv(x, gain, w_k, w_v_t):
    t, d = x.shape
    nblk = t // MOBA_BLOCK
    return pl.pallas_call(
        _kv_kernel,
        grid=(nblk,),
        in_specs=[
            pl.BlockSpec((MOBA_BLOCK, d), lambda i: (i, 0)),
            pl.BlockSpec((1, d), lambda i: (0, 0)),
            pl.BlockSpec((d, d), lambda i: (0, 0)),
            pl.BlockSpec((d, d), lambda i: (0, 0)),
        ],
        out_specs=[
            pl.BlockSpec((MOBA_BLOCK, d), lambda i: (i, 0)),
            pl.BlockSpec((1, d, MOBA_BLOCK), lambda i: (i, 0, 0)),
            pl.BlockSpec((1, 1, d), lambda i: (i, 0, 0)),
        ],
        out_shape=[
            jax.ShapeDtypeStruct((t, d), BF16),
            jax.ShapeDtypeStruct((nblk, d, MOBA_BLOCK), BF16),
            jax.ShapeDtypeStruct((nblk, 1, d), F32),
        ],
        compiler_params=_params("parallel"),
    )(x, gain.reshape(1, d), w_k, w_v_t)


def _hgrn_heads(q_ref, f_ref, i_ref, g_ref, lbl_ref, gain_ref, o_ref, st_ref, *, layer, first):
    c = HGRN_CHUNK
    dk = HEAD_DIM
    heads = q_ref.shape[1] // dk
    per_head = q_ref.shape[0] // c
    n_chunks = heads * per_head
    group = MOBA_BLOCK // c

    def wide(ref):
        return jnp.concatenate([ref[n * c:(n + 1) * c, hh * dk:(hh + 1) * dk]
                                for hh in range(heads) for n in range(per_head)], axis=1)

    def tall(ref):
        return jnp.concatenate([ref[:, hh * dk:(hh + 1) * dk] for hh in range(heads)], axis=0)

    def chunk(x, n):
        return x[:, n * dk:(n + 1) * dk]

    def rows_of(x, n0, n1):
        return jnp.concatenate([chunk(x, n) for n in range(n0, n1)], axis=0)

    lanes = slice(first * dk, (first + heads) * dk)
    fz = wide(f_ref)
    t = jnp.exp(-jnp.abs(fz))
    log_sig = jnp.minimum(fz, 0.0) - jnp.log(1.0 + t)
    sig_neg = jnp.exp(log_sig - fz)
    if layer > 0:
        logits = lbl_ref[:, lanes]
        e = jnp.exp(logits - jnp.max(logits, axis=0, keepdims=True))
        p = e / jnp.sum(e, axis=0, keepdims=True)
        lb = jnp.sum(p[1:layer + 1], axis=0, keepdims=True)
        lb = jnp.concatenate([chunk(lb, hh) for hh in range(heads) for _ in range(per_head)],
                             axis=1)
        log_lb = jnp.log(lb)
        y = jnp.log(1.0 - lb) + log_sig
        log_f = jnp.maximum(log_lb, y) + jnp.log(1.0 + jnp.exp(-jnp.abs(log_lb - y)))
        k = (1.0 - lb) * sig_neg
    else:
        log_f = log_sig
        k = sig_neg

    slab = SUBLANES
    row = lax.broadcasted_iota(jnp.int32, (slab, log_f.shape[1]), 0)
    parts, above = [], None
    for s0 in range(0, c, slab):
        part = log_f[s0:s0 + slab, :]
        shift = 1
        while shift < slab:
            part = part + jnp.where(row >= shift, pltpu.roll(part, shift, axis=0), 0.0)
            shift *= 2
        if above is not None:
            part = part + above
        above = part[slab - 1:slab, :]
        parts.append(part)
    b = jnp.concatenate(parts, axis=0)
    b_mid = b[c // 2 - 1:c // 2, :]
    b_last = b[c - 1:c, :]

    q = _silu(wide(q_ref))
    qd = q * jnp.exp(b - b_mid)
    kd = k * jnp.exp(b_mid - b)
    qe = (qd * jnp.exp(b_mid)).astype(BF16)
    kl = (kd * jnp.exp(b_last - b_mid)).astype(BF16)
    qd = qd.astype(BF16)
    kd = kd.astype(BF16)
    v = wide(i_ref).astype(BF16)
    decay = jnp.exp(b_last)

    gr = group * c
    r_i = lax.broadcasted_iota(jnp.int32, (gr, gr), 0)
    c_i = lax.broadcasted_iota(jnp.int32, (gr, gr), 1)
    keep = (c_i <= r_i) & ((r_i // c) == (c_i // c))
    groups = range(0, n_chunks, group)
    scores = [_dot_nt(rows_of(qd, n0, n0 + group), rows_of(kd, n0, n0 + group)) for n0 in groups]

    grow = []
    zero = jnp.zeros((c, dk), BF16)
    for n in range(0, n_chunks, 2):
        keys = jnp.concatenate([jnp.concatenate([chunk(kl, n), zero], axis=1),
                                jnp.concatenate([zero, chunk(kl, n + 1)], axis=1)], axis=0)
        both = _dot_tn(rows_of(v, n, n + 2), keys)
        grow += [both[:, :dk], both[:, dk:]]
    intra = [_dot(jnp.where(keep, a, 0.0).astype(BF16), rows_of(v, n0, n0 + group))
             for a, n0 in zip(scores, groups)]
    before = []
    for hh in range(heads):
        st = st_ref[first + hh]
        for n in range(hh * per_head, (hh + 1) * per_head):
            before.append(st.astype(BF16))
            st = chunk(decay, n) * st + grow[n]
        st_ref[first + hh] = st
    inter = [_dot_nt(chunk(qe, n), before[n]) for n in range(n_chunks)]
    o = jnp.concatenate(intra, axis=0) + jnp.concatenate(inter, axis=0)

    o = o * lax.rsqrt(jnp.mean(o * o, axis=-1, keepdims=True) + EPS)
    o = o * _silu(tall(g_ref))
    rows = q_ref.shape[0]
    for hh in range(heads):
        cols = slice((first + hh) * dk, (first + hh + 1) * dk)
        o_ref[:, cols] = (o[hh * rows:(hh + 1) * rows] * gain_ref[:, cols]).astype(o_ref.dtype)


def _hgrn_layer_kernel(x_ref, ng_ref, w_ref, lbl_ref, gain_ref, o_ref, st_ref, *, layer):
    @pl.when(pl.program_id(1) == 0)
    def _():
        st_ref[...] = jnp.zeros_like(st_ref)

    d = x_ref.shape[1]
    hp = HGRN_HEADS_PER_STEP
    width = hp * HEAD_DIM
    h = _rms(x_ref[...], ng_ref[...]).astype(BF16)

    def project(group):
        return [_dot(h, w_ref[:, part * d + group * width:part * d + (group + 1) * width])
                for part in range(4)]

    n_groups = d // width
    nxt = project(0)
    for group in range(n_groups):
        cur = nxt
        if group + 1 < n_groups:
            nxt = project(group + 1)
        _hgrn_heads(*cur, lbl_ref, gain_ref, o_ref, st_ref, layer=layer, first=group * hp)


def hgrn_layer(x, norm_gain, w_in, lb_logits, out_gain, *, layer, batch, seq):
    t, d = x.shape
    heads = d // HEAD_DIM
    tiles = seq // HGRN_TILE
    n_layers = lb_logits.shape[0]
    return pl.pallas_call(
        functools.partial(_hgrn_layer_kernel, layer=layer),
        grid=(batch, tiles),
        in_specs=[
            pl.BlockSpec((HGRN_TILE, d), lambda b, s: (b * tiles + s, 0)),
            pl.BlockSpec((1, d), lambda b, s: (0, 0)),
            pl.BlockSpec((d, 4 * d), lambda b, s: (0, 0), pipeline_mode=pl.Buffered(1)),
            pl.BlockSpec((n_layers, d), lambda b, s: (0, 0)),
            pl.BlockSpec((1, d), lambda b, s: (0, 0)),
        ],
        out_specs=pl.BlockSpec((HGRN_TILE, d), lambda b, s: (b * tiles + s, 0)),
        out_shape=jax.ShapeDtypeStruct((t, d), BF16),
        scratch_shapes=[pltpu.VMEM((heads, HEAD_DIM, HEAD_DIM), F32)],
        compiler_params=_params("parallel", "arbitrary"),
    )(x, norm_gain.reshape(1, d), w_in, lb_logits, out_gain.reshape(1, d))


def _moba_kernel(q_ref, k_ref, vt_ref, km_ref, slope_ref, o_ref):
    bs = MOBA_BLOCK
    dh = HEAD_DIM
    n_blk = km_ref.shape[1]
    heads = q_ref.shape[1] // dh
    scale2 = HEAD_DIM ** -0.5 * LOG2E
    rel = (lax.broadcasted_iota(jnp.int32, (bs, bs), 1)
           - lax.broadcasted_iota(jnp.int32, (bs, bs), 0))
    rel_f = rel.astype(F32)

    def gate_rank(q, km, n):
        gate = lax.dot_general(km, q, (((1,), (1,)), ((), ())),
                               precision=lax.Precision.HIGHEST, preferred_element_type=F32)
        blk = lax.broadcasted_iota(jnp.int32, gate.shape, 0)
        rank = jnp.zeros(gate.shape, F32)
        for m in range(n):
            g_m = gate[m:m + 1, :]
            rank = rank + ((g_m > gate) | ((g_m == gate) & (m < blk))).astype(F32)
        return rank

    def scores(n, hh):
        cols = slice(hh * dh, (hh + 1) * dh)
        q = q_ref[n * bs:(n + 1) * bs, cols]
        q16 = (q * scale2).astype(BF16)
        rank = gate_rank(q, km_ref[0][:, cols], n) if n > MOBA_TOPK else None
        return _dot_nt(k_ref[0:(n + 1) * bs, cols], q16), rank

    def probabilities(n, hh, s_all, rank):
        slope2 = slope_ref[hh][0:1, 0:1] * LOG2E
        bias = -slope2 * rel_f
        us, shifts, maxes = [], [], []
        for kb in range(n + 1):
            u = s_all[kb * bs:(kb + 1) * bs] + bias
            if kb == n:
                u = jnp.where(rel >= 0, u, -jnp.inf)
            elif rank is not None:
                u = jnp.where(rank[kb:kb + 1, :] < MOBA_TOPK, u, -jnp.inf)
            shift = slope2 * float((n - kb) * bs)
            us.append(u)
            shifts.append(shift)
            maxes.append(jnp.max(u, axis=0, keepdims=True) - shift)
        m = functools.reduce(jnp.maximum, maxes)
        ps = [jnp.exp2(u - (m + shift)) for u, shift in zip(us, shifts)]
        return jnp.concatenate(ps, axis=0).astype(BF16)

    def output(n, hh, p_all):
        cols = slice(hh * dh, (hh + 1) * dh)
        vt = jnp.concatenate([vt_ref[kb, cols, :] for kb in range(n + 1)], axis=1)
        ones = jnp.ones((ROW_ALIGN, vt.shape[1]), BF16)
        pv = _dot(jnp.concatenate([vt, ones], axis=0), p_all)
        o_ref[n * bs:(n + 1) * bs, cols] = (pv[0:dh] / pv[dh:dh + 1]).T.astype(o_ref.dtype)

    scored = [scores(0, hh) for hh in range(heads)]
    for n in range(n_blk):
        probs = [probabilities(n, hh, *scored[hh]) for hh in range(heads)]
        if n + 1 < n_blk:
            scored = [scores(n + 1, hh) for hh in range(heads)]
        for hh in range(heads):
            output(n, hh, probs[hh])


def moba_attention(q, k, v_t, k_mean, *, batch, seq):
    t, d = q.shape
    heads = d // HEAD_DIM
    n_blk = seq // MOBA_BLOCK
    hp = MOBA_HEADS_PER_STEP
    width = hp * HEAD_DIM
    slopes = (2.0 ** (-8.0 * jnp.arange(1, heads + 1, dtype=F32) / heads))
    slopes = jnp.broadcast_to(slopes[:, None, None], (heads, 1, LANES))
    return pl.pallas_call(
        _moba_kernel,
        grid=(batch, heads // hp),
        in_specs=[
            pl.BlockSpec((seq, width), lambda b, h: (b, h)),
            pl.BlockSpec((seq, width), lambda b, h: (b, h)),
            pl.BlockSpec((n_blk, width, MOBA_BLOCK), lambda b, h: (b, h, 0)),
            pl.BlockSpec((1, n_blk, width), lambda b, h: (b, 0, h)),
            pl.BlockSpec((hp, 1, LANES), lambda b, h: (h, 0, 0)),
        ],
        out_specs=pl.BlockSpec((seq, width), lambda b, h: (b, h)),
        out_shape=jax.ShapeDtypeStruct((t, d), BF16),
        compiler_params=_params("parallel", "parallel"),
    )(q, k, v_t, k_mean.reshape(batch, n_blk, d), slopes)


def _router_kernel(x_ref, a_ref, wo_ref, g_ref, wt_ref, x1_ref, h_ref, info_ref, info_t_ref,
                   cnt_ref, *, n_experts):
    x1 = x_ref[...] + _dot(a_ref[...], wo_ref[...])
    x1_ref[...] = x1
    h = _rms(x1, g_ref[...])
    h_hi = h.astype(BF16)
    h_ref[...] = h_hi
    h_lo = (h - h_hi.astype(F32)).astype(BF16)
    wt = wt_ref[...]
    wt_hi = wt.astype(BF16)
    wt_lo = (wt - wt_hi.astype(F32)).astype(BF16)
    by_hi = _dot_nt(jnp.concatenate([wt_hi, wt_lo], axis=0), h_hi)
    by_lo = _dot_nt(wt_hi, h_lo)
    rows = EXPERT_ROWS
    logits = by_hi[0:rows] + by_hi[ROW_ALIGN:ROW_ALIGN + rows] + by_lo[0:rows]
    tm = logits.shape[1]
    sub = lax.broadcasted_iota(jnp.int32, (rows, tm), 0)
    neg = -jnp.inf
    lg = jnp.where(sub < n_experts, logits, neg)
    m1 = jnp.max(lg, axis=0, keepdims=True)
    i1 = jnp.min(jnp.where(lg == m1, sub, rows), axis=0, keepdims=True)
    lg2 = jnp.where(sub == i1, neg, lg)
    m2 = jnp.max(lg2, axis=0, keepdims=True)
    i2 = jnp.min(jnp.where(lg2 == m2, sub, rows), axis=0, keepdims=True)
    e2 = jnp.exp(m2 - m1)
    w1 = 1.0 / (1.0 + e2)
    w2 = e2 / (1.0 + e2)

    routed = ((sub == i1) | (sub == i2)).astype(F32)
    routed_pad = jnp.concatenate([routed, jnp.zeros((LANES - rows, tm), F32)], axis=0).astype(BF16)
    cnt_ref[0] = _dot_nt(jnp.ones((ROW_ALIGN, tm), BF16), routed_pad)[0:1, :].astype(jnp.int32)
    later = (lax.broadcasted_iota(jnp.int32, (tm, tm), 0)
             < lax.broadcasted_iota(jnp.int32, (tm, tm), 1)).astype(BF16)
    rank = _dot(routed_pad[0:ROW_ALIGN], later)[0:rows]
    cnt = jnp.sum(routed, axis=1, keepdims=True)
    group_rows = jnp.floor((cnt + (ROW_ALIGN - 1)) * (1.0 / ROW_ALIGN)) * ROW_ALIGN
    expert = lax.broadcasted_iota(jnp.int32, (rows, 1), 0)
    group_start = jnp.zeros((rows, 1), F32)
    for e in range(n_experts - 1):
        group_start = group_start + jnp.where(expert > e, group_rows[e:e + 1, :], 0.0)
    slab_row = group_start + rank
    first = i1 < i2
    i_lo = jnp.where(first, i1, i2)
    i_hi = jnp.where(first, i2, i1)
    row_lo = jnp.sum(jnp.where(sub == i_lo, slab_row, 0.0), axis=0, keepdims=True)
    row_hi = jnp.sum(jnp.where(sub == i_hi, slab_row, 0.0), axis=0, keepdims=True)
    info_t = (jnp.where(sub == 0, row_lo, 0.0) + jnp.where(sub == 1, row_hi, 0.0)
              + jnp.where(sub == 2, jnp.where(first, w1, w2), 0.0)
              + jnp.where(sub == 3, jnp.where(first, w2, w1), 0.0))
    info_t_ref[0] = info_t
    info_ref[...] = jnp.concatenate([info_t, jnp.zeros((LANES - rows, tm), F32)], axis=0).T


def router(x, a, w_o, gain, w_router):
    t, d = x.shape
    n_experts = w_router.shape[1]
    n_tiles = t // TOK_TILE
    assert n_experts <= EXPERT_ROWS, n_experts
    w_t = jnp.pad(w_router.T, ((0, ROW_ALIGN - n_experts), (0, 0)))
    return pl.pallas_call(
        functools.partial(_router_kernel, n_experts=n_experts),
        grid=(n_tiles,),
        in_specs=[
            pl.BlockSpec((TOK_TILE, d), lambda i: (i, 0)),
            pl.BlockSpec((TOK_TILE, d), lambda i: (i, 0)),
            pl.BlockSpec((d, d), lambda i: (0, 0)),
            pl.BlockSpec((1, d), lambda i: (0, 0)),
            pl.BlockSpec((ROW_ALIGN, d), lambda i: (0, 0)),
        ],
        out_specs=[
            pl.BlockSpec((TOK_TILE, d), lambda i: (i, 0)),
            pl.BlockSpec((TOK_TILE, d), lambda i: (i, 0)),
            pl.BlockSpec((TOK_TILE, LANES), lambda i: (i, 0)),
            pl.BlockSpec((1, EXPERT_ROWS, TOK_TILE), lambda i: (i, 0, 0)),
            pl.BlockSpec((1, 1, LANES), lambda i: (i, 0, 0)),
        ],
        out_shape=[
            jax.ShapeDtypeStruct((t, d), F32),
            jax.ShapeDtypeStruct((t, d), BF16),
            jax.ShapeDtypeStruct((t, LANES), F32),
            jax.ShapeDtypeStruct((n_tiles, EXPERT_ROWS, TOK_TILE), F32),
            jax.ShapeDtypeStruct((n_tiles, 1, LANES), jnp.int32),
        ],
        compiler_params=_params("parallel"),
    )(x, a, w_o, gain.reshape(1, d), w_t)


def _slab_rows(n_experts):
    bound = MOE_TOPK * TOK_TILE + n_experts * (ROW_ALIGN - 1)
    return -(-bound // LANES) * LANES


def _copy_rows(n, max_size, make_copy, wait):
    off = jnp.int32(0)
    size = max_size
    while size >= ROW_ALIGN:
        @pl.when((n & size) != 0)
        def _(off=off, size=size):
            cp = make_copy(off, size)
            if wait:
                cp.wait()
            else:
                cp.start()
        off = off + (n & size)
        size //= 2


def _group_copies(n_ref, rs_ref, tile, n_experts, make_copy, wait):
    slab_row = jnp.int32(0)
    for e in range(n_experts):
        n = n_ref[tile * n_experts + e]
        base = rs_ref[tile * n_experts + e]

        def group_copy(off, size, slab_row=slab_row, base=base):
            return make_copy(pl.multiple_of(slab_row + off, ROW_ALIGN),
                             pl.multiple_of(base + off, ROW_ALIGN), size)

        _copy_rows(n, TOK_TILE, group_copy, wait)
        slab_row = slab_row + n


def _dispatch_kernel(rs_ref, n_ref, ps_ref, pn_ref, h_ref, info_t_ref, xs_ref,
                     slab_ref, zero_ref, sem, zero_sem, *, n_experts):
    i = pl.program_id(0)
    last = pl.num_programs(0) - 1
    slot = i % 2
    rows, tm = slab_ref.shape[1], h_ref.shape[0]
    info_t = info_t_ref[0]
    r = lax.broadcasted_iota(jnp.int32, (rows, tm), 0).astype(F32)
    onehot = ((r == info_t[0:1, :]) | (r == info_t[1:2, :])).astype(BF16)
    slab_ref[slot] = _dot(onehot, h_ref[...]).astype(BF16)

    def copies(tile, slot, wait):
        def make_copy(slab_row, buffer_row, size):
            return pltpu.make_async_copy(slab_ref.at[slot, pl.ds(slab_row, size)],
                                         xs_ref.at[pl.ds(buffer_row, size)], sem.at[slot])
        _group_copies(n_ref, rs_ref, tile, n_experts, make_copy, wait)

    def pad_copies(wait):
        for e in range(n_experts):
            def make_copy(off, size, e=e):
                start = pl.multiple_of(ps_ref[e] + off, ROW_ALIGN)
                return pltpu.make_async_copy(zero_ref.at[pl.ds(0, size)],
                                             xs_ref.at[pl.ds(start, size)], zero_sem)
            _copy_rows(pn_ref[e], zero_ref.shape[0], make_copy, wait)

        def idle_chunk(j, carry):
            start = pl.multiple_of(ps_ref[n_experts] + j * zero_ref.shape[0], ROW_ALIGN)
            cp = pltpu.make_async_copy(zero_ref, xs_ref.at[pl.ds(start, zero_ref.shape[0])],
                                       zero_sem)
            if wait:
                cp.wait()
            else:
                cp.start()
            return carry
        lax.fori_loop(0, pn_ref[n_experts], idle_chunk, 0)

    @pl.when(i > 0)
    def _():
        copies(i - 1, 1 - slot, wait=True)

    copies(i, slot, wait=False)

    @pl.when(i == last)
    def _():
        zero_ref[...] = jnp.zeros_like(zero_ref)
        pad_copies(wait=False)
        copies(i, slot, wait=True)
        pad_copies(wait=True)


def dispatch(h, info_t, row_start, n_rows, pad_start, pad_rows, *, total_rows, n_experts):
    t, d = h.shape
    n_tiles = t // TOK_TILE
    return pl.pallas_call(
        functools.partial(_dispatch_kernel, n_experts=n_experts),
        grid_spec=pltpu.PrefetchScalarGridSpec(
            num_scalar_prefetch=4,
            grid=(n_tiles,),
            in_specs=[
                pl.BlockSpec((TOK_TILE, d), lambda i, *_: (i, 0)),
                pl.BlockSpec((1, 8, TOK_TILE), lambda i, *_: (i, 0, 0)),
            ],
            out_specs=pl.BlockSpec(memory_space=pl.ANY),
            scratch_shapes=[pltpu.VMEM((2, _slab_rows(n_experts), d), BF16),
                            pltpu.VMEM((FFN_ROWS // 2, d), BF16),
                            pltpu.SemaphoreType.DMA((2,)),
                            pltpu.SemaphoreType.DMA],
        ),
        out_shape=jax.ShapeDtypeStruct((total_rows, d), BF16),
        compiler_params=_params("arbitrary"),
    )(row_start, n_rows, pad_start, pad_rows, h, info_t)


def _grouped_ffn_kernel(te_ref, na_ref, x_ref, wg_ref, wu_ref, wd_ref, o_ref, acc_ref):
    j = pl.program_id(0)
    f = pl.program_id(1)
    last_f = pl.num_programs(1) - 1
    active = j < na_ref[0]

    @pl.when(active)
    def _():
        @pl.when(f == 0)
        def _():
            acc_ref[...] = jnp.zeros_like(acc_ref)

        x = x_ref[...]
        act = _silu(_dot(x, wg_ref[...])) * _dot(x, wu_ref[...])
        acc_ref[...] += _dot(act.astype(BF16), wd_ref[...])

        @pl.when(f == last_f)
        def _():
            o_ref[...] = acc_ref[...].astype(o_ref.dtype)

    @pl.when(jnp.logical_not(active) & (f == last_f))
    def _():
        o_ref[...] = jnp.zeros_like(o_ref)


def grouped_ffn(xs, w_gu, w_down, layer, tile_expert, n_active, *, f_tile):
    rows, d = xs.shape
    ff = w_down.shape[2]
    nf = ff // f_tile

    def f_idx(j, f, na):
        return jnp.where(j < na[0], f, nf - 1)

    return pl.pallas_call(
        _grouped_ffn_kernel,
        grid_spec=pltpu.PrefetchScalarGridSpec(
            num_scalar_prefetch=2,
            grid=(rows // FFN_ROWS, nf),
            in_specs=[
                pl.BlockSpec((FFN_ROWS, d), lambda j, f, te, na: (jnp.minimum(j, na[0] - 1), 0)),
                pl.BlockSpec((None, None, d, f_tile),
                             lambda j, f, te, na: (layer, te[j], 0, f_idx(j, f, na))),
                pl.BlockSpec((None, None, d, f_tile),
                             lambda j, f, te, na: (layer, te[j], 0, nf + f_idx(j, f, na))),
                pl.BlockSpec((None, None, f_tile, d),
                             lambda j, f, te, na: (layer, te[j], f_idx(j, f, na), 0)),
            ],
            out_specs=pl.BlockSpec((FFN_ROWS, d), lambda j, f, te, na: (j, 0)),
            scratch_shapes=[pltpu.VMEM((FFN_ROWS, d), F32)],
        ),
        out_shape=jax.ShapeDtypeStruct((rows, d), BF16),
        compiler_params=_params("arbitrary", "arbitrary"),
    )(tile_expert, n_active, xs, w_gu, w_gu, w_down)


def _combine_kernel(rs_ref, n_ref, x_ref, info_ref, fg_ref, y_ref, o_ref, ybuf_ref, sem,
                    *, n_experts, final_norm):
    i = pl.program_id(0)
    slot = i % 2
    rows, tm = ybuf_ref.shape[1], x_ref.shape[0]

    def copies(tile, slot, wait):
        def make_copy(slab_row, buffer_row, size):
            return pltpu.make_async_copy(y_ref.at[pl.ds(buffer_row, size)],
                                         ybuf_ref.at[slot, pl.ds(slab_row, size)], sem.at[slot])
        _group_copies(n_ref, rs_ref, tile, n_experts, make_copy, wait)

    @pl.when(i == 0)
    def _():
        ybuf_ref[...] = jnp.zeros_like(ybuf_ref)
        copies(i, slot, wait=False)

    @pl.when(i + 1 < pl.num_programs(0))
    def _():
        copies(i + 1, 1 - slot, wait=False)

    copies(i, slot, wait=True)
    y = ybuf_ref[slot]
    info = info_ref[...]
    r = lax.broadcasted_iota(jnp.int32, (tm, rows), 1).astype(F32)
    zero = jnp.zeros((), F32)
    weights = jnp.where(r == info[:, 0:1], info[:, 2:3],
                        jnp.where(r == info[:, 1:2], info[:, 3:4], zero))
    out = x_ref[...] + _dot(weights.astype(BF16), y)
    o_ref[...] = _rms(out, fg_ref[...]) if final_norm else out


def combine(x, info, y, row_start, n_rows, final_gain, *, n_experts, final_norm):
    t, d = x.shape
    n_tiles = t // TOK_TILE
    return pl.pallas_call(
        functools.partial(_combine_kernel, n_experts=n_experts, final_norm=final_norm),
        grid_spec=pltpu.PrefetchScalarGridSpec(
            num_scalar_prefetch=2,
            grid=(n_tiles,),
            in_specs=[
                pl.BlockSpec((TOK_TILE, d), lambda i, rs, n: (i, 0)),
                pl.BlockSpec((TOK_TILE, LANES), lambda i, rs, n: (i, 0)),
                pl.BlockSpec((1, d), lambda i, rs, n: (0, 0)),
                pl.BlockSpec(memory_space=pl.ANY),
            ],
            out_specs=pl.BlockSpec((TOK_TILE, d), lambda i, rs, n: (i, 0)),
            scratch_shapes=[pltpu.VMEM((2, _slab_rows(n_experts), d), BF16),
                            pltpu.SemaphoreType.DMA((2,))],
        ),
        out_shape=jax.ShapeDtypeStruct((t, d), F32),
        compiler_params=_params("arbitrary"),
    )(row_start, n_rows, x, info, final_gain.reshape(1, d), y)


def _ffn_kernel(x_ref, a_ref, wo_ref, g_ref, wg_ref, wu_ref, wd_ref, o_ref):
    x1 = x_ref[...] + _dot(a_ref[...], wo_ref[...])
    h = _rms(x1, g_ref[...]).astype(BF16)
    act = _silu(_dot(h, wg_ref[...])) * _dot(h, wu_ref[...])
    o_ref[...] = x1 + _dot(act.astype(BF16), wd_ref[...])


def ffn(x, a, w_o, gain, w_gu, w_down):
    t, d = x.shape
    ff = w_down.shape[0]
    once = pl.Buffered(1)
    rows = pl.BlockSpec((FFN_DENSE_ROWS, d), lambda i: (i, 0))
    return pl.pallas_call(
        _ffn_kernel,
        grid=(t // FFN_DENSE_ROWS,),
        in_specs=[
            rows,
            rows,
            pl.BlockSpec((d, d), lambda i: (0, 0), pipeline_mode=once),
            pl.BlockSpec((1, d), lambda i: (0, 0)),
            pl.BlockSpec((d, ff), lambda i: (0, 0), pipeline_mode=once),
            pl.BlockSpec((d, ff), lambda i: (0, 1), pipeline_mode=once),
            pl.BlockSpec((ff, d), lambda i: (0, 0), pipeline_mode=once),
        ],
        out_specs=rows,
        out_shape=jax.ShapeDtypeStruct((t, d), F32),
        compiler_params=_params("parallel"),
    )(x, a, w_o, gain.reshape(1, d), w_gu, w_gu, w_down)


def _rmsnorm_kernel(x_ref, g_ref, o_ref):
    o_ref[...] = _rms(x_ref[...], g_ref[...])


def rmsnorm(x, gain):
    t, d = x.shape
    return pl.pallas_call(
        _rmsnorm_kernel,
        grid=(t // ROW_TILE,),
        in_specs=[pl.BlockSpec((ROW_TILE, d), lambda i: (i, 0)),
                  pl.BlockSpec((1, d), lambda i: (0, 0))],
        out_specs=pl.BlockSpec((ROW_TILE, d), lambda i: (i, 0)),
        out_shape=jax.ShapeDtypeStruct((t, d), F32),
        compiler_params=_params("parallel"),
    )(x, gain.reshape(1, d))


def _mxu_tile(n, target):
    best = None
    for cand in range(MXU_WIDTH, target + 1, MXU_WIDTH):
        if n % cand == 0:
            best = cand
    assert best is not None, (n, target)
    return best


def moe_layer(x, a, w_o, gain, w_router, w_gu, w_down, layer, final_gain, *, final_norm):
    t, d = x.shape
    n_experts = w_router.shape[1]
    n_tiles = t // TOK_TILE
    x, h, info, info_t, cnt = router(x, a, w_o, gain, w_router)
    cnt = cnt.reshape(n_tiles, LANES)[:, :n_experts]
    n_rows = (cnt + ROW_ALIGN - 1) // ROW_ALIGN * ROW_ALIGN
    seg_rows = jnp.sum(n_rows, axis=0)
    seg_cap = (seg_rows + FFN_ROWS - 1) // FFN_ROWS * FFN_ROWS
    seg_end = jnp.cumsum(seg_cap)
    row_start = (seg_end - seg_cap)[None, :] + jnp.cumsum(n_rows, axis=0) - n_rows
    max_rows = (MOE_TOPK * t + n_tiles * n_experts * (ROW_ALIGN - 1)
                + n_experts * (FFN_ROWS - ROW_ALIGN))
    total_tiles = -(-max_rows // FFN_ROWS)
    n_active = (seg_end[-1:] // FFN_ROWS).astype(jnp.int32)
    tile_row = jnp.arange(total_tiles, dtype=jnp.int32) * FFN_ROWS
    tile_expert = jnp.minimum(jnp.sum(seg_end[None, :] <= tile_row[:, None], axis=1),
                              n_experts - 1).astype(jnp.int32)
    row_start = row_start.reshape(-1).astype(jnp.int32)
    n_rows = n_rows.reshape(-1).astype(jnp.int32)
    total_rows = total_tiles * FFN_ROWS
    pad_start = jnp.concatenate([seg_end - seg_cap + seg_rows, seg_end[-1:]]).astype(jnp.int32)
    pad_rows = jnp.concatenate([seg_cap - seg_rows,
                                (total_rows - seg_end[-1:]) // (FFN_ROWS // 2)]).astype(jnp.int32)
    xs = dispatch(h, info_t, row_start, n_rows, pad_start, pad_rows,
                  total_rows=total_rows, n_experts=n_experts)
    y = grouped_ffn(xs, w_gu, w_down, layer, tile_expert, n_active,
                    f_tile=_mxu_tile(w_down.shape[2], 2048))
    return combine(x, info, y, row_start, n_rows, final_gain,
                   n_experts=n_experts, final_norm=final_norm)


def kernel(x, attn_norm, ffn_norm, hgrn_w_in, hgrn_lb_logits, hgrn_out_norm, hgrn_w_o, kv_norm, w_kv, moba_w_q, moba_w_o, ffn_w_gu, ffn_w_down, moe_router, moe_w_gu, moe_w_down, final_norm):
    batch, seq, d = x.shape
    depth = attn_norm.shape[0]
    n_a = hgrn_w_in.shape[0]
    x = x.reshape(batch * seq, d)
    k = v_t = k_mean = None
    moe_gu = moe_w_gu.astype(BF16)
    moe_down = moe_w_down.astype(BF16)
    for layer in range(depth):
        if layer < n_a:
            o = hgrn_layer(x, attn_norm[layer], hgrn_w_in[layer].astype(BF16), hgrn_lb_logits,
                           hgrn_out_norm[layer], layer=layer, batch=batch, seq=seq)
            w_o = hgrn_w_o[layer].astype(BF16)
        else:
            if layer == n_a:
                k, v_t, k_mean = shared_kv(x, kv_norm, w_kv[:, :d].astype(BF16),
                                           w_kv[:, d:].T.astype(BF16))
            j = layer - n_a
            q = norm_matmul(x, attn_norm[layer], moba_w_q[j].astype(BF16), F32)
            o = moba_attention(q, k, v_t, k_mean, batch=batch, seq=seq)
            w_o = moba_w_o[j].astype(BF16)
        j = layer // 2
        if layer % 2 == 0:
            x = ffn(x, o, w_o, ffn_norm[layer], ffn_w_gu[j].astype(BF16),
                    ffn_w_down[j].astype(BF16))
        else:
            last = layer == depth - 1
            x = moe_layer(x, o, w_o, ffn_norm[layer], moe_router[j], moe_gu, moe_down, j,
                          final_norm, final_norm=last)
    if depth % 2 == 1:
        x = rmsnorm(x, final_norm)
    return x.reshape(batch, seq, d)
```

```python
import functools

import jax
import jax.numpy as jnp
from jax import lax
from jax.experimental import pallas as pl
from jax.experimental.pallas import tpu as pltpu

F32 = jnp.float32
BF16 = jnp.bfloat16

EPS = 1e-6
HEAD_DIM = 128
HGRN_CHUNK = 64
MOBA_BLOCK = 256
MOBA_TOPK = 3
MOE_TOPK = 2
MOBA_HEADS_PER_STEP = 2
LOG2E = 1.4426950408889634
LANES = 128
MXU_WIDTH = 256
FFN_DENSE_ROWS = 512
VMEM_LIMIT = 56 * 1024 * 1024

HGRN_TILE = 512
HGRN_HEADS_PER_STEP = 2
ROW_TILE = 512
TOK_TILE = 512
FFN_ROWS = 512
ROW_ALIGN = 16
SUBLANES = 8
EXPERT_ROWS = SUBLANES


def _params(*semantics):
    return pltpu.CompilerParams(dimension_semantics=semantics, vmem_limit_bytes=VMEM_LIMIT)


def _rms(x, gain):
    return x * lax.rsqrt(jnp.mean(x * x, axis=-1, keepdims=True) + EPS) * gain


def _dot(a, b):
    return jnp.dot(a, b, preferred_element_type=F32)


def _dot_nt(a, b):
    return lax.dot_general(a, b, (((1,), (1,)), ((), ())), preferred_element_type=F32)


def _dot_tn(a, b):
    return lax.dot_general(a, b, (((0,), (0,)), ((), ())), preferred_element_type=F32)


def _silu(x):
    half = 0.5 * x
    return half + half * jnp.tanh(half)


def _norm_matmul_kernel(x_ref, g_ref, w_ref, o_ref):
    h = _rms(x_ref[...], g_ref[...]).astype(BF16)
    o_ref[...] = _dot(h, w_ref[...]).astype(o_ref.dtype)


def norm_matmul(x, gain, w, out_dtype):
    t, d = x.shape
    n = w.shape[1]
    return pl.pallas_call(
        _norm_matmul_kernel,
        grid=(t // ROW_TILE,),
        in_specs=[
            pl.BlockSpec((ROW_TILE, d), lambda i: (i, 0)),
            pl.BlockSpec((1, d), lambda i: (0, 0)),
            pl.BlockSpec((d, n), lambda i: (0, 0)),
        ],
        out_specs=pl.BlockSpec((ROW_TILE, n), lambda i: (i, 0)),
        out_shape=jax.ShapeDtypeStruct((t, n), out_dtype),
        compiler_params=_params("parallel"),
    )(x, gain.reshape(1, d), w)


def _kv_kernel(x_ref, g_ref, wk_ref, wvt_ref, k_ref, vt_ref, km_ref):
    h = _rms(x_ref[...], g_ref[...]).astype(BF16)
    k = _dot(h, wk_ref[...])
    k_ref[...] = k.astype(k_ref.dtype)
    vt_ref[0] = _dot_nt(wvt_ref[...], h).astype(vt_ref.dtype)
    km_ref[0] = jnp.mean(k, axis=0, keepdims=True)


def shared_kv(x, gain, w_k, w_v_t):
    t, d = x.shape
    nblk = t // MOBA_BLOCK
    return pl.pallas_call(
        _kv_kernel,
        grid=(nblk,),
        in_specs=[
            pl.BlockSpec((MOBA_BLOCK, d), lambda i: (i, 0)),
            pl.BlockSpec((1, d), lambda i: (0, 0)),
            pl.BlockSpec((d, d), lambda i: (0, 0)),
            pl.BlockSpec((d, d), lambda i: (0, 0)),
        ],
        out_specs=[
            pl.BlockSpec((MOBA_BLOCK, d), lambda i: (i, 0)),
            pl.BlockSpec((1, d, MOBA_BLOCK), lambda i: (i, 0, 0)),
            pl.BlockSpec((1, 1, d), lambda i: (i, 0, 0)),
        ],
        out_shape=[
            jax.ShapeDtypeStruct((t, d), BF16),
            jax.ShapeDtypeStruct((nblk, d, MOBA_BLOCK), BF16),
            jax.ShapeDtypeStruct((nblk, 1, d), F32),
        ],
        compiler_params=_params("parallel"),
    )(x, gain.reshape(1, d), w_k, w_v_t)


def _hgrn_heads(q_ref, f_ref, i_ref, g_ref, lbl_ref, gain_ref, o_ref, st_ref, *, layer, first):
    c = HGRN_CHUNK
    dk = HEAD_DIM
    heads = q_ref.shape[1] // dk
    per_head = q_ref.shape[0] // c
    n_chunks = heads * per_head
    group = MOBA_BLOCK // c

    def wide(ref):
        return jnp.concatenate([ref[n * c:(n + 1) * c, hh * dk:(hh + 1) * dk]
                                for hh in range(heads) for n in range(per_head)], axis=1)

    def tall(ref):
        return jnp.concatenate([ref[:, hh * dk:(hh + 1) * dk] for hh in range(heads)], axis=0)

    def chunk(x, n):
        return x[:, n * dk:(n + 1) * dk]

    def rows_of(x, n0, n1):
        return jnp.concatenate([chunk(x, n) for n in range(n0, n1)], axis=0)

    lanes = slice(first * dk, (first + heads) * dk)
    fz = wide(f_ref)
    t = jnp.exp(-jnp.abs(fz))
    log_sig = jnp.minimum(fz, 0.0) - jnp.log(1.0 + t)
    sig_neg = jnp.exp(log_sig - fz)
    if layer > 0:
        logits = lbl_ref[:, lanes]
        e = jnp.exp(logits - jnp.max(logits, axis=0, keepdims=True))
        p = e / jnp.sum(e, axis=0, keepdims=True)
        lb = jnp.sum(p[1:layer + 1], axis=0, keepdims=True)
        lb = jnp.concatenate([chunk(lb, hh) for hh in range(heads) for _ in range(per_head)],
                             axis=1)
        log_lb = jnp.log(lb)
        y = jnp.log(1.0 - lb) + log_sig
        log_f = jnp.maximum(log_lb, y) + jnp.log(1.0 + jnp.exp(-jnp.abs(log_lb - y)))
        k = (1.0 - lb) * sig_neg
    else:
        log_f = log_sig
        k = sig_neg

    slab = SUBLANES
    row = lax.broadcasted_iota(jnp.int32, (slab, log_f.shape[1]), 0)
    parts, above = [], None
    for s0 in range(0, c, slab):
        part = log_f[s0:s0 + slab, :]
        shift = 1
        while shift < slab:
            part = part + jnp.where(row >= shift, pltpu.roll(part, shift, axis=0), 0.0)
            shift *= 2
        if above is not None:
            part = part + above
        above = part[slab - 1:slab, :]
        parts.append(part)
    b = jnp.concatenate(parts, axis=0)
    b_mid = b[c // 2 - 1:c // 2, :]
    b_last = b[c - 1:c, :]

    q = _silu(wide(q_ref))
    qd = q * jnp.exp(b - b_mid)
    kd = k * jnp.exp(b_mid - b)
    qe = (qd * jnp.exp(b_mid)).astype(BF16)
    kl = (kd * jnp.exp(b_last - b_mid)).astype(BF16)
    qd = qd.astype(BF16)
    kd = kd.astype(BF16)
    v = wide(i_ref).astype(BF16)
    decay = jnp.exp(b_last)

    gr = group * c
    r_i = lax.broadcasted_iota(jnp.int32, (gr, gr), 0)
    c_i = lax.broadcasted_iota(jnp.int32, (gr, gr), 1)
    keep = (c_i <= r_i) & ((r_i // c) == (c_i // c))
    groups = range(0, n_chunks, group)
    scores = [_dot_nt(rows_of(qd, n0, n0 + group), rows_of(kd, n0, n0 + group)) for n0 in groups]

    grow = []
    zero = jnp.zeros((c, dk), BF16)
    for n in range(0, n_chunks, 2):
        keys = jnp.concatenate([jnp.concatenate([chunk(kl, n), zero], axis=1),
                                jnp.concatenate([zero, chunk(kl, n + 1)], axis=1)], axis=0)
        both = _dot_tn(rows_of(v, n, n + 2), keys)
        grow += [both[:, :dk], both[:, dk:]]
    intra = [_dot(jnp.where(keep, a, 0.0).astype(BF16), rows_of(v, n0, n0 + group))
             for a, n0 in zip(scores, groups)]
    before = []
    for hh in range(heads):
        st = st_ref[first + hh]
        for n in range(hh * per_head, (hh + 1) * per_head):
            before.append(st.astype(BF16))
            st = chunk(decay, n) * st + grow[n]
        st_ref[first + hh] = st
    inter = [_dot_nt(chunk(qe, n), before[n]) for n in range(n_chunks)]
    o = jnp.concatenate(intra, axis=0) + jnp.concatenate(inter, axis=0)

    o = o * lax.rsqrt(jnp.mean(o * o, axis=-1, keepdims=True) + EPS)
    o = o * _silu(tall(g_ref))
    rows = q_ref.shape[0]
    for hh in range(heads):
        cols = slice((first + hh) * dk, (first + hh + 1) * dk)
        o_ref[:, cols] = (o[hh * rows:(hh + 1) * rows] * gain_ref[:, cols]).astype(o_ref.dtype)


def _hgrn_layer_kernel(x_ref, ng_ref, w_ref, lbl_ref, gain_ref, o_ref, st_ref, *, layer):
    @pl.when(pl.program_id(1) == 0)
    def _():
        st_ref[...] = jnp.zeros_like(st_ref)

    d = x_ref.shape[1]
    hp = HGRN_HEADS_PER_STEP
    width = hp * HEAD_DIM
    h = _rms(x_ref[...], ng_ref[...]).astype(BF16)

    def project(group):
        return [_dot(h, w_ref[:, part * d + group * width:part * d + (group + 1) * width])
                for part in range(4)]

    n_groups = d // width
    nxt = project(0)
    for group in range(n_groups):
        cur = nxt
        if group + 1 < n_groups:
            nxt = project(group + 1)
        _hgrn_heads(*cur, lbl_ref, gain_ref, o_ref, st_ref, layer=layer, first=group * hp)


def hgrn_layer(x, norm_gain, w_in, lb_logits, out_gain, *, layer, batch, seq):
    t, d = x.shape
    heads = d // HEAD_DIM
    tiles = seq // HGRN_TILE
    n_layers = lb_logits.shape[0]
    return pl.pallas_call(
        functools.partial(_hgrn_layer_kernel, layer=layer),
        grid=(batch, tiles),
        in_specs=[
            pl.BlockSpec((HGRN_TILE, d), lambda b, s: (b * tiles + s, 0)),
            pl.BlockSpec((1, d), lambda b, s: (0, 0)),
            pl.BlockSpec((d, 4 * d), lambda b, s: (0, 0), pipeline_mode=pl.Buffered(1)),
            pl.BlockSpec((n_layers, d), lambda b, s: (0, 0)),
            pl.BlockSpec((1, d), lambda b, s: (0, 0)),
        ],
        out_specs=pl.BlockSpec((HGRN_TILE, d), lambda b, s: (b * tiles + s, 0)),
        out_shape=jax.ShapeDtypeStruct((t, d), BF16),
        scratch_shapes=[pltpu.VMEM((heads, HEAD_DIM, HEAD_DIM), F32)],
        compiler_params=_params("parallel", "arbitrary"),
    )(x, norm_gain.reshape(1, d), w_in, lb_logits, out_gain.reshape(1, d))


def _moba_kernel(q_ref, k_ref, vt_ref, km_ref, slope_ref, o_ref):
    bs = MOBA_BLOCK
    dh = HEAD_DIM
    n_blk = km_ref.shape[1]
    heads = q_ref.shape[1] // dh
    scale2 = HEAD_DIM ** -0.5 * LOG2E
    rel = (lax.broadcasted_iota(jnp.int32, (bs, bs), 1)
           - lax.broadcasted_iota(jnp.int32, (bs, bs), 0))
    rel_f = rel.astype(F32)

    def gate_rank(q, km, n):
        gate = lax.dot_general(km, q, (((1,), (1,)), ((), ())),
                               precision=lax.Precision.HIGHEST, preferred_element_type=F32)
        blk = lax.broadcasted_iota(jnp.int32, gate.shape, 0)
        rank = jnp.zeros(gate.shape, F32)
        for m in range(n):
            g_m = gate[m:m + 1, :]
            rank = rank + ((g_m > gate) | ((g_m == gate) & (m < blk))).astype(F32)
        return rank

    def scores(n, hh):
        cols = slice(hh * dh, (hh + 1) * dh)
        q = q_ref[n * bs:(n + 1) * bs, cols]
        q16 = (q * scale2).astype(BF16)
        rank = gate_rank(q, km_ref[0][:, cols], n) if n > MOBA_TOPK else None
        return _dot_nt(k_ref[0:(n + 1) * bs, cols], q16), rank

    def probabilities(n, hh, s_all, rank):
        slope2 = slope_ref[hh][0:1, 0:1] * LOG2E
        bias = -slope2 * rel_f
        us, shifts, maxes = [], [], []
        for kb in range(n + 1):
            u = s_all[kb * bs:(kb + 1) * bs] + bias
            if kb == n:
                u = jnp.where(rel >= 0, u, -jnp.inf)
            elif rank is not None:
                u = jnp.where(rank[kb:kb + 1, :] < MOBA_TOPK, u, -jnp.inf)
            shift = slope2 * float((n - kb) * bs)
            us.append(u)
            shifts.append(shift)
            maxes.append(jnp.max(u, axis=0, keepdims=True) - shift)
        m = functools.reduce(jnp.maximum, maxes)
        ps = [jnp.exp2(u - (m + shift)) for u, shift in zip(us, shifts)]
        return jnp.concatenate(ps, axis=0).astype(BF16)

    def output(n, hh, p_all):
        cols = slice(hh * dh, (hh + 1) * dh)
        vt = jnp.concatenate([vt_ref[kb, cols, :] for kb in range(n + 1)], axis=1)
        ones = jnp.ones((ROW_ALIGN, vt.shape[1]), BF16)
        pv = _dot(jnp.concatenate([vt, ones], axis=0), p_all)
        o_ref[n * bs:(n + 1) * bs, cols] = (pv[0:dh] / pv[dh:dh + 1]).T.astype(o_ref.dtype)

    scored = [scores(0, hh) for hh in range(heads)]
    for n in range(n_blk):
        probs = [probabilities(n, hh, *scored[hh]) for hh in range(heads)]
        if n + 1 < n_blk:
            scored = [scores(n + 1, hh) for hh in range(heads)]
        for hh in range(heads):
            output(n, hh, probs[hh])


def moba_attention(q, k, v_t, k_mean, *, batch, seq):
    t, d = q.shape
    heads = d // HEAD_DIM
    n_blk = seq // MOBA_BLOCK
    hp = MOBA_HEADS_PER_STEP
    width = hp * HEAD_DIM
    slopes = (2.0 ** (-8.0 * jnp.arange(1, heads + 1, dtype=F32) / heads))
    slopes = jnp.broadcast_to(slopes[:, None, None], (heads, 1, LANES))
    return pl.pallas_call(
        _moba_kernel,
        grid=(batch, heads // hp),
        in_specs=[
            pl.BlockSpec((seq, width), lambda b, h: (b, h)),
            pl.BlockSpec((seq, width), lambda b, h: (b, h)),
            pl.BlockSpec((n_blk, width, MOBA_BLOCK), lambda b, h: (b, h, 0)),
            pl.BlockSpec((1, n_blk, width), lambda b, h: (b, 0, h)),
            pl.BlockSpec((hp, 1, LANES), lambda b, h: (h, 0, 0)),
        ],
        out_specs=pl.BlockSpec((seq, width), lambda b, h: (b, h)),
        out_shape=jax.ShapeDtypeStruct((t, d), BF16),
        compiler_params=_params("parallel", "parallel"),
    )(q, k, v_t, k_mean.reshape(batch, n_blk, d), slopes)


def _router_kernel(x_ref, a_ref, wo_ref, g_ref, wt_ref, x1_ref, h_ref, info_ref, info_t_ref,
                   cnt_ref, *, n_experts):
    x1 = x_ref[...] + _dot(a_ref[...], wo_ref[...])
    x1_ref[...] = x1
    h = _rms(x1, g_ref[...])
    h_hi = h.astype(BF16)
    h_ref[...] = h_hi
    h_lo = (h - h_hi.astype(F32)).astype(BF16)
    wt = wt_ref[...]
    wt_hi = wt.astype(BF16)
    wt_lo = (wt - wt_hi.astype(F32)).astype(BF16)
    by_hi = _dot_nt(jnp.concatenate([wt_hi, wt_lo], axis=0), h_hi)
    by_lo = _dot_nt(wt_hi, h_lo)
    rows = EXPERT_ROWS
    logits = by_hi[0:rows] + by_hi[ROW_ALIGN:ROW_ALIGN + rows] + by_lo[0:rows]
    tm = logits.shape[1]
    sub = lax.broadcasted_iota(jnp.int32, (rows, tm), 0)
    neg = -jnp.inf
    lg = jnp.where(sub < n_experts, logits, neg)
    m1 = jnp.max(lg, axis=0, keepdims=True)
    i1 = jnp.min(jnp.where(lg == m1, sub, rows), axis=0, keepdims=True)
    lg2 = jnp.where(sub == i1, neg, lg)
    m2 = jnp.max(lg2, axis=0, keepdims=True)
    i2 = jnp.min(jnp.where(lg2 == m2, sub, rows), axis=0, keepdims=True)
    e2 = jnp.exp(m2 - m1)
    w1 = 1.0 / (1.0 + e2)
    w2 = e2 / (1.0 + e2)

    routed = ((sub == i1) | (sub == i2)).astype(F32)
    routed_pad = jnp.concatenate([routed, jnp.zeros((LANES - rows, tm), F32)], axis=0).astype(BF16)
    cnt_ref[0] = _dot_nt(jnp.ones((ROW_ALIGN, tm), BF16), routed_pad)[0:1, :].astype(jnp.int32)
    later = (lax.broadcasted_iota(jnp.int32, (tm, tm), 0)
             < lax.broadcasted_iota(jnp.int32, (tm, tm), 1)).astype(BF16)
    rank = _dot(routed_pad[0:ROW_ALIGN], later)[0:rows]
    cnt = jnp.sum(routed, axis=1, keepdims=True)
    group_rows = jnp.floor((cnt + (ROW_ALIGN - 1)) * (1.0 / ROW_ALIGN)) * ROW_ALIGN
    expert = lax.broadcasted_iota(jnp.int32, (rows, 1), 0)
    group_start = jnp.zeros((rows, 1), F32)
    for e in range(n_experts - 1):
        group_start = group_start + jnp.where(expert > e, group_rows[e:e + 1, :], 0.0)
    slab_row = group_start + rank
    first = i1 < i2
    i_lo = jnp.where(first, i1, i2)
    i_hi = jnp.where(first, i2, i1)
    row_lo = jnp.sum(jnp.where(sub == i_lo, slab_row, 0.0), axis=0, keepdims=True)
    row_hi = jnp.sum(jnp.where(sub == i_hi, slab_row, 0.0), axis=0, keepdims=True)
    info_t = (jnp.where(sub == 0, row_lo, 0.0) + jnp.where(sub == 1, row_hi, 0.0)
              + jnp.where(sub == 2, jnp.where(first, w1, w2), 0.0)
              + jnp.where(sub == 3, jnp.where(first, w2, w1), 0.0))
    info_t_ref[0] = info_t
    info_ref[...] = jnp.concatenate([info_t, jnp.zeros((LANES - rows, tm), F32)], axis=0).T


def router(x, a, w_o, gain, w_router):
    t, d = x.shape
    n_experts = w_router.shape[1]
    n_tiles = t // TOK_TILE
    assert n_experts <= EXPERT_ROWS, n_experts
    w_t = jnp.pad(w_router.T, ((0, ROW_ALIGN - n_experts), (0, 0)))
    return pl.pallas_call(
        functools.partial(_router_kernel, n_experts=n_experts),
        grid=(n_tiles,),
        in_specs=[
            pl.BlockSpec((TOK_TILE, d), lambda i: (i, 0)),
            pl.BlockSpec((TOK_TILE, d), lambda i: (i, 0)),
            pl.BlockSpec((d, d), lambda i: (0, 0)),
            pl.BlockSpec((1, d), lambda i: (0, 0)),
            pl.BlockSpec((ROW_ALIGN, d), lambda i: (0, 0)),
        ],
        out_specs=[
            pl.BlockSpec((TOK_TILE, d), lambda i: (i, 0)),
            pl.BlockSpec((TOK_TILE, d), lambda i: (i, 0)),
            pl.BlockSpec((TOK_TILE, LANES), lambda i: (i, 0)),
            pl.BlockSpec((1, EXPERT_ROWS, TOK_TILE), lambda i: (i, 0, 0)),
            pl.BlockSpec((1, 1, LANES), lambda i: (i, 0, 0)),
        ],
        out_shape=[
            jax.ShapeDtypeStruct((t, d), F32),
            jax.ShapeDtypeStruct((t, d), BF16),
            jax.ShapeDtypeStruct((t, LANES), F32),
            jax.ShapeDtypeStruct((n_tiles, EXPERT_ROWS, TOK_TILE), F32),
            jax.ShapeDtypeStruct((n_tiles, 1, LANES), jnp.int32),
        ],
        compiler_params=_params("parallel"),
    )(x, a, w_o, gain.reshape(1, d), w_t)


def _slab_rows(n_experts):
    bound = MOE_TOPK * TOK_TILE + n_experts * (ROW_ALIGN - 1)
    return -(-bound // LANES) * LANES


def _copy_rows(n, max_size, make_copy, wait):
    off = jnp.int32(0)
    size = max_size
    while size >= ROW_ALIGN:
        @pl.when((n & size) != 0)
        def _(off=off, size=size):
            cp = make_copy(off, size)
            if wait:
                cp.wait()
            else:
                cp.start()
        off = off + (n & size)
        size //= 2


def _group_copies(n_ref, rs_ref, tile, n_experts, make_copy, wait):
    slab_row = jnp.int32(0)
    for e in range(n_experts):
        n = n_ref[tile * n_experts + e]
        base = rs_ref[tile * n_experts + e]

        def group_copy(off, size, slab_row=slab_row, base=base):
            return make_copy(pl.multiple_of(slab_row + off, ROW_ALIGN),
                             pl.multiple_of(base + off, ROW_ALIGN), size)

        _copy_rows(n, TOK_TILE, group_copy, wait)
        slab_row = slab_row + n


def _dispatch_kernel(rs_ref, n_ref, ps_ref, pn_ref, h_ref, info_t_ref, xs_ref,
                     slab_ref, zero_ref, sem, zero_sem, *, n_experts):
    i = pl.program_id(0)
    last = pl.num_programs(0) - 1
    slot = i % 2
    rows, tm = slab_ref.shape[1], h_ref.shape[0]
    info_t = info_t_ref[0]
    r = lax.broadcasted_iota(jnp.int32, (rows, tm), 0).astype(F32)
    onehot = ((r == info_t[0:1, :]) | (r == info_t[1:2, :])).astype(BF16)
    slab_ref[slot] = _dot(onehot, h_ref[...]).astype(BF16)

    def copies(tile, slot, wait):
        def make_copy(slab_row, buffer_row, size):
            return pltpu.make_async_copy(slab_ref.at[slot, pl.ds(slab_row, size)],
                                         xs_ref.at[pl.ds(buffer_row, size)], sem.at[slot])
        _group_copies(n_ref, rs_ref, tile, n_experts, make_copy, wait)

    def pad_copies(wait):
        for e in range(n_experts):
            def make_copy(off, size, e=e):
                start = pl.multiple_of(ps_ref[e] + off, ROW_ALIGN)
                return pltpu.make_async_copy(zero_ref.at[pl.ds(0, size)],
                                             xs_ref.at[pl.ds(start, size)], zero_sem)
            _copy_rows(pn_ref[e], zero_ref.shape[0], make_copy, wait)

        def idle_chunk(j, carry):
            start = pl.multiple_of(ps_ref[n_experts] + j * zero_ref.shape[0], ROW_ALIGN)
            cp = pltpu.make_async_copy(zero_ref, xs_ref.at[pl.ds(start, zero_ref.shape[0])],
                                       zero_sem)
            if wait:
                cp.wait()
            else:
                cp.start()
            return carry
        lax.fori_loop(0, pn_ref[n_experts], idle_chunk, 0)

    @pl.when(i > 0)
    def _():
        copies(i - 1, 1 - slot, wait=True)

    copies(i, slot, wait=False)

    @pl.when(i == last)
    def _():
        zero_ref[...] = jnp.zeros_like(zero_ref)
        pad_copies(wait=False)
        copies(i, slot, wait=True)
        pad_copies(wait=True)


def dispatch(h, info_t, row_start, n_rows, pad_start, pad_rows, *, total_rows, n_experts):
    t, d = h.shape
    n_tiles = t // TOK_TILE
    return pl.pallas_call(
        functools.partial(_dispatch_kernel, n_experts=n_experts),
        grid_spec=pltpu.PrefetchScalarGridSpec(
            num_scalar_prefetch=4,
            grid=(n_tiles,),
            in_specs=[
                pl.BlockSpec((TOK_TILE, d), lambda i, *_: (i, 0)),
                pl.BlockSpec((1, 8, TOK_TILE), lambda i, *_: (i, 0, 0)),
            ],
            out_specs=pl.BlockSpec(memory_space=pl.ANY),
            scratch_shapes=[pltpu.VMEM((2, _slab_rows(n_experts), d), BF16),
                            pltpu.VMEM((FFN_ROWS // 2, d), BF16),
                            pltpu.SemaphoreType.DMA((2,)),
                            pltpu.SemaphoreType.DMA],
        ),
        out_shape=jax.ShapeDtypeStruct((total_rows, d), BF16),
        compiler_params=_params("arbitrary"),
    )(row_start, n_rows, pad_start, pad_rows, h, info_t)


def _grouped_ffn_kernel(te_ref, na_ref, x_ref, wg_ref, wu_ref, wd_ref, o_ref, acc_ref):
    j = pl.program_id(0)
    f = pl.program_id(1)
    last_f = pl.num_programs(1) - 1
    active = j < na_ref[0]

    @pl.when(active)
    def _():
        @pl.when(f == 0)
        def _():
            acc_ref[...] = jnp.zeros_like(acc_ref)

        x = x_ref[...]
        act = _silu(_dot(x, wg_ref[...])) * _dot(x, wu_ref[...])
        acc_ref[...] += _dot(act.astype(BF16), wd_ref[...])

        @pl.when(f == last_f)
        def _():
            o_ref[...] = acc_ref[...].astype(o_ref.dtype)

    @pl.when(jnp.logical_not(active) & (f == last_f))
    def _():
        o_ref[...] = jnp.zeros_like(o_ref)


def grouped_ffn(xs, w_gu, w_down, layer, tile_expert, n_active, *, f_tile):
    rows, d = xs.shape
    ff = w_down.shape[2]
    nf = ff // f_tile

    def f_idx(j, f, na):
        return jnp.where(j < na[0], f, nf - 1)

    return pl.pallas_call(
        _grouped_ffn_kernel,
        grid_spec=pltpu.PrefetchScalarGridSpec(
            num_scalar_prefetch=2,
            grid=(rows // FFN_ROWS, nf),
            in_specs=[
                pl.BlockSpec((FFN_ROWS, d), lambda j, f, te, na: (jnp.minimum(j, na[0] - 1), 0)),
                pl.BlockSpec((None, None, d, f_tile),
                             lambda j, f, te, na: (layer, te[j], 0, f_idx(j, f, na))),
                pl.BlockSpec((None, None, d, f_tile),
                             lambda j, f, te, na: (layer, te[j], 0, nf + f_idx(j, f, na))),
                pl.BlockSpec((None, None, f_tile, d),
                             lambda j, f, te, na: (layer, te[j], f_idx(j, f, na), 0)),
            ],
            out_specs=pl.BlockSpec((FFN_ROWS, d), lambda j, f, te, na: (j, 0)),
            scratch_shapes=[pltpu.VMEM((FFN_ROWS, d), F32)],
        ),
        out_shape=jax.ShapeDtypeStruct((rows, d), BF16),
        compiler_params=_params("arbitrary", "arbitrary"),
    )(tile_expert, n_active, xs, w_gu, w_gu, w_down)


def _combine_kernel(rs_ref, n_ref, x_ref, info_ref, fg_ref, y_ref, o_ref, ybuf_ref, sem,
                    *, n_experts, final_norm):
    i = pl.program_id(0)
    slot = i % 2
    rows, tm = ybuf_ref.shape[1], x_ref.shape[0]

    def copies(tile, slot, wait):
        def make_copy(slab_row, buffer_row, size):
            return pltpu.make_async_copy(y_ref.at[pl.ds(buffer_row, size)],
                                         ybuf_ref.at[slot, pl.ds(slab_row, size)], sem.at[slot])
        _group_copies(n_ref, rs_ref, tile, n_experts, make_copy, wait)

    @pl.when(i == 0)
    def _():
        ybuf_ref[...] = jnp.zeros_like(ybuf_ref)
        copies(i, slot, wait=False)

    @pl.when(i + 1 < pl.num_programs(0))
    def _():
        copies(i + 1, 1 - slot, wait=False)

    copies(i, slot, wait=True)
    y = ybuf_ref[slot]
    info = info_ref[...]
    r = lax.broadcasted_iota(jnp.int32, (tm, rows), 1).astype(F32)
    zero = jnp.zeros((), F32)
    weights = jnp.where(r == info[:, 0:1], info[:, 2:3],
                        jnp.where(r == info[:, 1:2], info[:, 3:4], zero))
    out = x_ref[...] + _dot(weights.astype(BF16), y)
    o_ref[...] = _rms(out, fg_ref[...]) if final_norm else out


def combine(x, info, y, row_start, n_rows, final_gain, *, n_experts, final_norm):
    t, d = x.shape
    n_tiles = t // TOK_TILE
    return pl.pallas_call(
        functools.partial(_combine_kernel, n_experts=n_experts, final_norm=final_norm),
        grid_spec=pltpu.PrefetchScalarGridSpec(
            num_scalar_prefetch=2,
            grid=(n_tiles,),
            in_specs=[
                pl.BlockSpec((TOK_TILE, d), lambda i, rs, n: (i, 0)),
                pl.BlockSpec((TOK_TILE, LANES), lambda i, rs, n: (i, 0)),
                pl.BlockSpec((1, d), lambda i, rs, n: (0, 0)),
                pl.BlockSpec(memory_space=pl.ANY),
            ],
            out_specs=pl.BlockSpec((TOK_TILE, d), lambda i, rs, n: (i, 0)),
            scratch_shapes=[pltpu.VMEM((2, _slab_rows(n_experts), d), BF16),
                            pltpu.SemaphoreType.DMA((2,))],
        ),
        out_shape=jax.ShapeDtypeStruct((t, d), F32),
        compiler_params=_params("arbitrary"),
    )(row_start, n_rows, x, info, final_gain.reshape(1, d), y)


def _ffn_kernel(x_ref, a_ref, wo_ref, g_ref, wg_ref, wu_ref, wd_ref, o_ref):
    x1 = x_ref[...] + _dot(a_ref[...], wo_ref[...])
    h = _rms(x1, g_ref[...]).astype(BF16)
    act = _silu(_dot(h, wg_ref[...])) * _dot(h, wu_ref[...])
    o_ref[...] = x1 + _dot(act.astype(BF16), wd_ref[...])


def ffn(x, a, w_o, gain, w_gu, w_down):
    t, d = x.shape
    ff = w_down.shape[0]
    once = pl.Buffered(1)
    rows = pl.BlockSpec((FFN_DENSE_ROWS, d), lambda i: (i, 0))
    return pl.pallas_call(
        _ffn_kernel,
        grid=(t // FFN_DENSE_ROWS,),
        in_specs=[
            rows,
            rows,
            pl.BlockSpec((d, d), lambda i: (0, 0), pipeline_mode=once),
            pl.BlockSpec((1, d), lambda i: (0, 0)),
            pl.BlockSpec((d, ff), lambda i: (0, 0), pipeline_mode=once),
            pl.BlockSpec((d, ff), lambda i: (0, 1), pipeline_mode=once),
            pl.BlockSpec((ff, d), lambda i: (0, 0), pipeline_mode=once),
        ],
        out_specs=rows,
        out_shape=jax.ShapeDtypeStruct((t, d), F32),
        compiler_params=_params("parallel"),
    )(x, a, w_o, gain.reshape(1, d), w_gu, w_gu, w_down)


def _rmsnorm_kernel(x_ref, g_ref, o_ref):
    o_ref[...] = _rms(x_ref[...], g_ref[...])


def rmsnorm(x, gain):
    t, d = x.shape
    return pl.pallas_call(
        _rmsnorm_kernel,
        grid=(t // ROW_TILE,),
        in_specs=[pl.BlockSpec((ROW_TILE, d), lambda i: (i, 0)),
                  pl.BlockSpec((1, d), lambda i: (0, 0))],
        out_specs=pl.BlockSpec((ROW_TILE, d), lambda i: (i, 0)),
        out_shape=jax.ShapeDtypeStruct((t, d), F32),
        compiler_params=_params("parallel"),
    )(x, gain.reshape(1, d))


def _mxu_tile(n, target):
    best = None
    for cand in range(MXU_WIDTH, target + 1, MXU_WIDTH):
        if n % cand == 0:
            best = cand
    assert best is not None, (n, target)
    return best


def moe_layer(x, a, w_o, gain, w_router, w_gu, w_down, layer, final_gain, *, final_norm):
    t, d = x.shape
    n_experts = w_router.shape[1]
    n_tiles = t // TOK_TILE
    x, h, info, info_t, cnt = router(x, a, w_o, gain, w_router)
    cnt = cnt.reshape(n_tiles, LANES)[:, :n_experts]
    n_rows = (cnt + ROW_ALIGN - 1) // ROW_ALIGN * ROW_ALIGN
    seg_rows = jnp.sum(n_rows, axis=0)
    seg_cap = (seg_rows + FFN_ROWS - 1) // FFN_ROWS * FFN_ROWS
    seg_end = jnp.cumsum(seg_cap)
    row_start = (seg_end - seg_cap)[None, :] + jnp.cumsum(n_rows, axis=0) - n_rows
    max_rows = (MOE_TOPK * t + n_tiles * n_experts * (ROW_ALIGN - 1)
                + n_experts * (FFN_ROWS - ROW_ALIGN))
    total_tiles = -(-max_rows // FFN_ROWS)
    n_active = (seg_end[-1:] // FFN_ROWS).astype(jnp.int32)
    tile_row = jnp.arange(total_tiles, dtype=jnp.int32) * FFN_ROWS
    tile_expert = jnp.minimum(jnp.sum(seg_end[None, :] <= tile_row[:, None], axis=1),
                              n_experts - 1).astype(jnp.int32)
    row_start = row_start.reshape(-1).astype(jnp.int32)
    n_rows = n_rows.reshape(-1).astype(jnp.int32)
    total_rows = total_tiles * FFN_ROWS
    pad_start = jnp.concatenate([seg_end - seg_cap + seg_rows, seg_end[-1:]]).astype(jnp.int32)
    pad_rows = jnp.concatenate([seg_cap - seg_rows,
                                (total_rows - seg_end[-1:]) // (FFN_ROWS // 2)]).astype(jnp.int32)
    xs = dispatch(h, info_t, row_start, n_rows, pad_start, pad_rows,
                  total_rows=total_rows, n_experts=n_experts)
    y = grouped_ffn(xs, w_gu, w_down, layer, tile_expert, n_active,
                    f_tile=_mxu_tile(w_down.shape[2], 2048))
    return combine(x, info, y, row_start, n_rows, final_gain,
                   n_experts=n_experts, final_norm=final_norm)


def kernel(x, attn_norm, ffn_norm, hgrn_w_in, hgrn_lb_logits, hgrn_out_norm, hgrn_w_o, kv_norm, w_kv, moba_w_q, moba_w_o, ffn_w_gu, ffn_w_down, moe_router, moe_w_gu, moe_w_down, final_norm):
    batch, seq, d = x.shape
    depth = attn_norm.shape[0]
    n_a = hgrn_w_in.shape[0]
    x = x.reshape(batch * seq, d)
    k = v_t = k_mean = None
    moe_gu = moe_w_gu.astype(BF16)
    moe_down = moe_w_down.astype(BF16)
    for layer in range(depth):
        if layer < n_a:
            o = hgrn_layer(x, attn_norm[layer], hgrn_w_in[layer].astype(BF16), hgrn_lb_logits,
                           hgrn_out_norm[layer], layer=layer, batch=batch, seq=seq)
            w_o = hgrn_w_o[layer].astype(BF16)
        else:
            if layer == n_a:
                k, v_t, k_mean = shared_kv(x, kv_norm, w_kv[:, :d].astype(BF16),
                                           w_kv[:, d:].T.astype(BF16))
            j = layer - n_a
            q = norm_matmul(x, attn_norm[layer], moba_w_q[j].astype(BF16), F32)
            o = moba_attention(q, k, v_t, k_mean, batch=batch, seq=seq)
            w_o = moba_w_o[j].astype(BF16)
        j = layer // 2
        if layer % 2 == 0:
            x = ffn(x, o, w_o, ffn_norm[layer], ffn_w_gu[j].astype(BF16),
                    ffn_w_down[j].astype(BF16))
        else:
            last = layer == depth - 1
            x = moe_layer(x, o, w_o, ffn_norm[layer], moe_router[j], moe_gu, moe_down, j,
                          final_norm, final_norm=last)
    if depth % 2 == 1:
        x = rmsnorm(x, final_norm)
    return x.reshape(batch, seq, d)
```

```python
import functools

import jax
import jax.numpy as jnp
from jax import lax
from jax.experimental import pallas as pl
from jax.experimental.pallas import tpu as pltpu

F32 = jnp.float32
BF16 = jnp.bfloat16

EPS = 1e-6
HEAD_DIM = 128
HGRN_CHUNK = 64
MOBA_BLOCK = 256
MOBA_TOPK = 3
MOE_TOPK = 2
MOBA_HEADS_PER_STEP = 2
LOG2E = 1.4426950408889634
LANES = 128
MXU_WIDTH = 256
FFN_DENSE_ROWS = 512
VMEM_LIMIT = 56 * 1024 * 1024

HGRN_TILE = 512
HGRN_HEADS_PER_STEP = 2
ROW_TILE = 512
TOK_TILE = 512
FFN_ROWS = 512
ROW_ALIGN = 16
SUBLANES = 8
EXPERT_ROWS = SUBLANES


def _params(*semantics):
    return pltpu.CompilerParams(dimension_semantics=semantics, vmem_limit_bytes=VMEM_LIMIT)


def _rms(x, gain):
    return x * lax.rsqrt(jnp.mean(x * x, axis=-1, keepdims=True) + EPS) * gain


def _dot(a, b):
    return jnp.dot(a, b, preferred_element_type=F32)


def _dot_nt(a, b):
    return lax.dot_general(a, b, (((1,), (1,)), ((), ())), preferred_element_type=F32)


def _dot_tn(a, b):
    return lax.dot_general(a, b, (((0,), (0,)), ((), ())), preferred_element_type=F32)


def _silu(x):
    half = 0.5 * x
    return half + half * jnp.tanh(half)


def _norm_matmul_kernel(x_ref, g_ref, w_ref, o_ref):
    h = _rms(x_ref[...], g_ref[...]).astype(BF16)
    o_ref[...] = _dot(h, w_ref[...]).astype(o_ref.dtype)


def norm_matmul(x, gain, w, out_dtype):
    t, d = x.shape
    n = w.shape[1]
    return pl.pallas_call(
        _norm_matmul_kernel,
        grid=(t // ROW_TILE,),
        in_specs=[
            pl.BlockSpec((ROW_TILE, d), lambda i: (i, 0)),
            pl.BlockSpec((1, d), lambda i: (0, 0)),
            pl.BlockSpec((d, n), lambda i: (0, 0)),
        ],
        out_specs=pl.BlockSpec((ROW_TILE, n), lambda i: (i, 0)),
        out_shape=jax.ShapeDtypeStruct((t, n), out_dtype),
        compiler_params=_params("parallel"),
    )(x, gain.reshape(1, d), w)


def _kv_kernel(x_ref, g_ref, wk_ref, wvt_ref, k_ref, vt_ref, km_ref):
    h = _rms(x_ref[...], g_ref[...]).astype(BF16)
    k = _dot(h, wk_ref[...])
    k_ref[...] = k.astype(k_ref.dtype)
    vt_ref[0] = _dot_nt(wvt_ref[...], h).astype(vt_ref.dtype)
    km_ref[0] = jnp.mean(k, axis=0, keepdims=True)


def shared_kv(x, gain, w_k, w_v_t):
    t, d = x.shape
    nblk = t // MOBA_BLOCK
    return pl.pallas_call(
        _kv_kernel,
        grid=(nblk,),
        in_specs=[
            pl.BlockSpec((MOBA_BLOCK, d), lambda i: (i, 0)),
            pl.BlockSpec((1, d), lambda i: (0, 0)),
            pl.BlockSpec((d, d), lambda i: (0, 0)),
            pl.BlockSpec((d, d), lambda i: (0, 0)),
        ],
        out_specs=[
            pl.BlockSpec((MOBA_BLOCK, d), lambda i: (i, 0)),
            pl.BlockSpec((1, d, MOBA_BLOCK), lambda i: (i, 0, 0)),
            pl.BlockSpec((1, 1, d), lambda i: (i, 0, 0)),
        ],
        out_shape=[
            jax.ShapeDtypeStruct((t, d), BF16),
            jax.ShapeDtypeStruct((nblk, d, MOBA_BLOCK), BF16),
            jax.ShapeDtypeStruct((nblk, 1, d), F32),
        ],
        compiler_params=_params("parallel"),
    )(x, gain.reshape(1, d), w_k, w_v_t)


def _hgrn_heads(q_ref, f_ref, i_ref, g_ref, lbl_ref, gain_ref, o_ref, st_ref, *, layer, first):
    c = HGRN_CHUNK
    dk = HEAD_DIM
    heads = q_ref.shape[1] // dk
    per_head = q_ref.shape[0] // c
    n_chunks = heads * per_head
    group = MOBA_BLOCK // c

    def wide(ref):
        return jnp.concatenate([ref[n * c:(n + 1) * c, hh * dk:(hh + 1) * dk]
                                for hh in range(heads) for n in range(per_head)], axis=1)

    def tall(ref):
        return jnp.concatenate([ref[:, hh * dk:(hh + 1) * dk] for hh in range(heads)], axis=0)

    def chunk(x, n):
        return x[:, n * dk:(n + 1) * dk]

    def rows_of(x, n0, n1):
        return jnp.concatenate([chunk(x, n) for n in range(n0, n1)], axis=0)

    lanes = slice(first * dk, (first + heads) * dk)
    fz = wide(f_ref)
    t = jnp.exp(-jnp.abs(fz))
    log_sig = jnp.minimum(fz, 0.0) - jnp.log(1.0 + t)
    sig_neg = jnp.exp(log_sig - fz)
    if layer > 0:
        logits = lbl_ref[:, lanes]
        e = jnp.exp(logits - jnp.max(logits, axis=0, keepdims=True))
        p = e / jnp.sum(e, axis=0, keepdims=True)
        lb = jnp.sum(p[1:layer + 1], axis=0, keepdims=True)
        lb = jnp.concatenate([chunk(lb, hh) for hh in range(heads) for _ in range(per_head)],
                             axis=1)
        log_lb = jnp.log(lb)
        y = jnp.log(1.0 - lb) + log_sig
        log_f = jnp.maximum(log_lb, y) + jnp.log(1.0 + jnp.exp(-jnp.abs(log_lb - y)))
        k = (1.0 - lb) * sig_neg
    else:
        log_f = log_sig
        k = sig_neg

    slab = SUBLANES
    row = lax.broadcasted_iota(jnp.int32, (slab, log_f.shape[1]), 0)
    parts, above = [], None
    for s0 in range(0, c, slab):
        part = log_f[s0:s0 + slab, :]
        shift = 1
        while shift < slab:
            part = part + jnp.where(row >= shift, pltpu.roll(part, shift, axis=0), 0.0)
            shift *= 2
        if above is not None:
            part = part + above
        above = part[slab - 1:slab, :]
        parts.append(part)
    b = jnp.concatenate(parts, axis=0)
    b_mid = b[c // 2 - 1:c // 2, :]
    b_last = b[c - 1:c, :]

    q = _silu(wide(q_ref))
    qd = q * jnp.exp(b - b_mid)
    kd = k * jnp.exp(b_mid - b)
    qe = (qd * jnp.exp(b_mid)).astype(BF16)
    kl = (kd * jnp.exp(b_last - b_mid)).astype(BF16)
    qd = qd.astype(BF16)
    kd = kd.astype(BF16)
    v = wide(i_ref).astype(BF16)
    decay = jnp.exp(b_last)

    gr = group * c
    r_i = lax.broadcasted_iota(jnp.int32, (gr, gr), 0)
    c_i = lax.broadcasted_iota(jnp.int32, (gr, gr), 1)
    keep = (c_i <= r_i) & ((r_i // c) == (c_i // c))
    groups = range(0, n_chunks, group)
    scores = [_dot_nt(rows_of(qd, n0, n0 + group), rows_of(kd, n0, n0 + group)) for n0 in groups]

    grow = []
    zero = jnp.zeros((c, dk), BF16)
    for n in range(0, n_chunks, 2):
        keys = jnp.concatenate([jnp.concatenate([chunk(kl, n), zero], axis=1),
                                jnp.concatenate([zero, chunk(kl, n + 1)], axis=1)], axis=0)
        both = _dot_tn(rows_of(v, n, n + 2), keys)
        grow += [both[:, :dk], both[:, dk:]]
    intra = [_dot(jnp.where(keep, a, 0.0).astype(BF16), rows_of(v, n0, n0 + group))
             for a, n0 in zip(scores, groups)]
    before = []
    for hh in range(heads):
        st = st_ref[first + hh]
        for n in range(hh * per_head, (hh + 1) * per_head):
            before.append(st.astype(BF16))
            st = chunk(decay, n) * st + grow[n]
        st_ref[first + hh] = st
    inter = [_dot_nt(chunk(qe, n), before[n]) for n in range(n_chunks)]
    o = jnp.concatenate(intra, axis=0) + jnp.concatenate(inter, axis=0)

    o = o * lax.rsqrt(jnp.mean(o * o, axis=-1, keepdims=True) + EPS)
    o = o * _silu(tall(g_ref))
    rows = q_ref.shape[0]
    for hh in range(heads):
        cols = slice((first + hh) * dk, (first + hh + 1) * dk)
        o_ref[:, cols] = (o[hh * rows:(hh + 1) * rows] * gain_ref[:, cols]).astype(o_ref.dtype)


def _hgrn_layer_kernel(x_ref, ng_ref, w_ref, lbl_ref, gain_ref, o_ref, st_ref, *, layer):
    @pl.when(pl.program_id(1) == 0)
    def _():
        st_ref[...] = jnp.zeros_like(st_ref)

    d = x_ref.shape[1]
    hp = HGRN_HEADS_PER_STEP
    width = hp * HEAD_DIM
    h = _rms(x_ref[...], ng_ref[...]).astype(BF16)

    def project(group):
        return [_dot(h, w_ref[:, part * d + group * width:part * d + (group + 1) * width])
                for part in range(4)]

    n_groups = d // width
    nxt = project(0)
    for group in range(n_groups):
        cur = nxt
        if group + 1 < n_groups:
            nxt = project(group + 1)
        _hgrn_heads(*cur, lbl_ref, gain_ref, o_ref, st_ref, layer=layer, first=group * hp)


def hgrn_layer(x, norm_gain, w_in, lb_logits, out_gain, *, layer, batch, seq):
    t, d = x.shape
    heads = d // HEAD_DIM
    tiles = seq // HGRN_TILE
    n_layers = lb_logits.shape[0]
    return pl.pallas_call(
        functools.partial(_hgrn_layer_kernel, layer=layer),
        grid=(batch, tiles),
        in_specs=[
            pl.BlockSpec((HGRN_TILE, d), lambda b, s: (b * tiles + s, 0)),
            pl.BlockSpec((1, d), lambda b, s: (0, 0)),
            pl.BlockSpec((d, 4 * d), lambda b, s: (0, 0), pipeline_mode=pl.Buffered(1)),
            pl.BlockSpec((n_layers, d), lambda b, s: (0, 0)),
            pl.BlockSpec((1, d), lambda b, s: (0, 0)),
        ],
        out_specs=pl.BlockSpec((HGRN_TILE, d), lambda b, s: (b * tiles + s, 0)),
        out_shape=jax.ShapeDtypeStruct((t, d), BF16),
        scratch_shapes=[pltpu.VMEM((heads, HEAD_DIM, HEAD_DIM), F32)],
        compiler_params=_params("parallel", "arbitrary"),
    )(x, norm_gain.reshape(1, d), w_in, lb_logits, out_gain.reshape(1, d))


def _moba_kernel(q_ref, k_ref, vt_ref, km_ref, slope_ref, o_ref):
    bs = MOBA_BLOCK
    dh = HEAD_DIM
    n_blk = km_ref.shape[1]
    heads = q_ref.shape[1] // dh
    scale2 = HEAD_DIM ** -0.5 * LOG2E
    rel = (lax.broadcasted_iota(jnp.int32, (bs, bs), 1)
           - lax.broadcasted_iota(jnp.int32, (bs, bs), 0))
    rel_f = rel.astype(F32)

    def gate_rank(q, km, n):
        gate = lax.dot_general(km, q, (((1,), (1,)), ((), ())),
                               precision=lax.Precision.HIGHEST, preferred_element_type=F32)
        blk = lax.broadcasted_iota(jnp.int32, gate.shape, 0)
        rank = jnp.zeros(gate.shape, F32)
        for m in range(n):
            g_m = gate[m:m + 1, :]
            rank = rank + ((g_m > gate) | ((g_m == gate) & (m < blk))).astype(F32)
        return rank

    def scores(n, hh):
        cols = slice(hh * dh, (hh + 1) * dh)
        q = q_ref[n * bs:(n + 1) * bs, cols]
        q16 = (q * scale2).astype(BF16)
        rank = gate_rank(q, km_ref[0][:, cols], n) if n > MOBA_TOPK else None
        return _dot_nt(k_ref[0:(n + 1) * bs, cols], q16), rank

    def probabilities(n, hh, s_all, rank):
        slope2 = slope_ref[hh][0:1, 0:1] * LOG2E
        bias = -slope2 * rel_f
        us, shifts, maxes = [], [], []
        for kb in range(n + 1):
            u = s_all[kb * bs:(kb + 1) * bs] + bias
            if kb == n:
                u = jnp.where(rel >= 0, u, -jnp.inf)
            elif rank is not None:
                u = jnp.where(rank[kb:kb + 1, :] < MOBA_TOPK, u, -jnp.inf)
            shift = slope2 * float((n - kb) * bs)
            us.append(u)
            shifts.append(shift)
            maxes.append(jnp.max(u, axis=0, keepdims=True) - shift)
        m = functools.reduce(jnp.maximum, maxes)
        ps = [jnp.exp2(u - (m + shift)) for u, shift in zip(us, shifts)]
        return jnp.concatenate(ps, axis=0).astype(BF16)

    def output(n, hh, p_all):
        cols = slice(hh * dh, (hh + 1) * dh)
        vt = jnp.concatenate([vt_ref[kb, cols, :] for kb in range(n + 1)], axis=1)
        ones = jnp.ones((ROW_ALIGN, vt.shape[1]), BF16)
        pv = _dot(jnp.concatenate([vt, ones], axis=0), p_all)
        o_ref[n * bs:(n + 1) * bs, cols] = (pv[0:dh] / pv[dh:dh + 1]).T.astype(o_ref.dtype)

    scored = [scores(0, hh) for hh in range(heads)]
    for n in range(n_blk):
        probs = [probabilities(n, hh, *scored[hh]) for hh in range(heads)]
        if n + 1 < n_blk:
            scored = [scores(n + 1, hh) for hh in range(heads)]
        for hh in range(heads):
            output(n, hh, probs[hh])


def moba_attention(q, k, v_t, k_mean, *, batch, seq):
    t, d = q.shape
    heads = d // HEAD_DIM
    n_blk = seq // MOBA_BLOCK
    hp = MOBA_HEADS_PER_STEP
    width = hp * HEAD_DIM
    slopes = (2.0 ** (-8.0 * jnp.arange(1, heads + 1, dtype=F32) / heads))
    slopes = jnp.broadcast_to(slopes[:, None, None], (heads, 1, LANES))
    return pl.pallas_call(
        _moba_kernel,
        grid=(batch, heads // hp),
        in_specs=[
            pl.BlockSpec((seq, width), lambda b, h: (b, h)),
            pl.BlockSpec((seq, width), lambda b, h: (b, h)),
            pl.BlockSpec((n_blk, width, MOBA_BLOCK), lambda b, h: (b, h, 0)),
            pl.BlockSpec((1, n_blk, width), lambda b, h: (b, 0, h)),
            pl.BlockSpec((hp, 1, LANES), lambda b, h: (h, 0, 0)),
        ],
        out_specs=pl.BlockSpec((seq, width), lambda b, h: (b, h)),
        out_shape=jax.ShapeDtypeStruct((t, d), BF16),
        compiler_params=_params("parallel", "parallel"),
    )(q, k, v_t, k_mean.reshape(batch, n_blk, d), slopes)


def _router_kernel(x_ref, a_ref, wo_ref, g_ref, wt_ref, x1_ref, h_ref, info_ref, info_t_ref,
                   cnt_ref, *, n_experts):
    x1 = x_ref[...] + _dot(a_ref[...], wo_ref[...])
    x1_ref[...] = x1
    h = _rms(x1, g_ref[...])
    h_hi = h.astype(BF16)
    h_ref[...] = h_hi
    h_lo = (h - h_hi.astype(F32)).astype(BF16)
    wt = wt_ref[...]
    wt_hi = wt.astype(BF16)
    wt_lo = (wt - wt_hi.astype(F32)).astype(BF16)
    by_hi = _dot_nt(jnp.concatenate([wt_hi, wt_lo], axis=0), h_hi)
    by_lo = _dot_nt(wt_hi, h_lo)
    rows = EXPERT_ROWS
    logits = by_hi[0:rows] + by_hi[ROW_ALIGN:ROW_ALIGN + rows] + by_lo[0:rows]
    tm = logits.shape[1]
    sub = lax.broadcasted_iota(jnp.int32, (rows, tm), 0)
    neg = -jnp.inf
    lg = jnp.where(sub < n_experts, logits, neg)
    m1 = jnp.max(lg, axis=0, keepdims=True)
    i1 = jnp.min(jnp.where(lg == m1, sub, rows), axis=0, keepdims=True)
    lg2 = jnp.where(sub == i1, neg, lg)
    m2 = jnp.max(lg2, axis=0, keepdims=True)
    i2 = jnp.min(jnp.where(lg2 == m2, sub, rows), axis=0, keepdims=True)
    e2 = jnp.exp(m2 - m1)
    w1 = 1.0 / (1.0 + e2)
    w2 = e2 / (1.0 + e2)

    routed = ((sub == i1) | (sub == i2)).astype(F32)
    routed_pad = jnp.concatenate([routed, jnp.zeros((LANES - rows, tm), F32)], axis=0).astype(BF16)
    cnt_ref[0] = _dot_nt(jnp.ones((ROW_ALIGN, tm), BF16), routed_pad)[0:1, :].astype(jnp.int32)
    later = (lax.broadcasted_iota(jnp.int32, (tm, tm), 0)
             < lax.broadcasted_iota(jnp.int32, (tm, tm), 1)).astype(BF16)
    rank = _dot(routed_pad[0:ROW_ALIGN], later)[0:rows]
    cnt = jnp.sum(routed, axis=1, keepdims=True)
    group_rows = jnp.floor((cnt + (ROW_ALIGN - 1)) * (1.0 / ROW_ALIGN)) * ROW_ALIGN
    expert = lax.broadcasted_iota(jnp.int32, (rows, 1), 0)
    group_start = jnp.zeros((rows, 1), F32)
    for e in range(n_experts - 1):
        group_start = group_start + jnp.where(expert > e, group_rows[e:e + 1, :], 0.0)
    slab_row = group_start + rank
    first = i1 < i2
    i_lo = jnp.where(first, i1, i2)
    i_hi = jnp.where(first, i2, i1)
    row_lo = jnp.sum(jnp.where(sub == i_lo, slab_row, 0.0), axis=0, keepdims=True)
    row_hi = jnp.sum(jnp.where(sub == i_hi, slab_row, 0.0), axis=0, keepdims=True)
    info_t = (jnp.where(sub == 0, row_lo, 0.0) + jnp.where(sub == 1, row_hi, 0.0)
              + jnp.where(sub == 2, jnp.where(first, w1, w2), 0.0)
              + jnp.where(sub == 3, jnp.where(first, w2, w1), 0.0))
    info_t_ref[0] = info_t
    info_ref[...] = jnp.concatenate([info_t, jnp.zeros((LANES - rows, tm), F32)], axis=0).T


def router(x, a, w_o, gain, w_router):
    t, d = x.shape
    n_experts = w_router.shape[1]
    n_tiles = t // TOK_TILE
    assert n_experts <= EXPERT_ROWS, n_experts
    w_t = jnp.pad(w_router.T, ((0, ROW_ALIGN - n_experts), (0, 0)))
    return pl.pallas_call(
        functools.partial(_router_kernel, n_experts=n_experts),
        grid=(n_tiles,),
        in_specs=[
            pl.BlockSpec((TOK_TILE, d), lambda i: (i, 0)),
            pl.BlockSpec((TOK_TILE, d), lambda i: (i, 0)),
            pl.BlockSpec((d, d), lambda i: (0, 0)),
            pl.BlockSpec((1, d), lambda i: (0, 0)),
            pl.BlockSpec((ROW_ALIGN, d), lambda i: (0, 0)),
        ],
        out_specs=[
            pl.BlockSpec((TOK_TILE, d), lambda i: (i, 0)),
            pl.BlockSpec((TOK_TILE, d), lambda i: (i, 0)),
            pl.BlockSpec((TOK_TILE, LANES), lambda i: (i, 0)),
            pl.BlockSpec((1, EXPERT_ROWS, TOK_TILE), lambda i: (i, 0, 0)),
            pl.BlockSpec((1, 1, LANES), lambda i: (i, 0, 0)),
        ],
        out_shape=[
            jax.ShapeDtypeStruct((t, d), F32),
            jax.ShapeDtypeStruct((t, d), BF16),
            jax.ShapeDtypeStruct((t, LANES), F32),
            jax.ShapeDtypeStruct((n_tiles, EXPERT_ROWS, TOK_TILE), F32),
            jax.ShapeDtypeStruct((n_tiles, 1, LANES), jnp.int32),
        ],
        compiler_params=_params("parallel"),
    )(x, a, w_o, gain.reshape(1, d), w_t)


def _slab_rows(n_experts):
    bound = MOE_TOPK * TOK_TILE + n_experts * (ROW_ALIGN - 1)
    return -(-bound // LANES) * LANES


def _copy_rows(n, max_size, make_copy, wait):
    off = jnp.int32(0)
    size = max_size
    while size >= ROW_ALIGN:
        @pl.when((n & size) != 0)
        def _(off=off, size=size):
            cp = make_copy(off, size)
            if wait:
                cp.wait()
            else:
                cp.start()
        off = off + (n & size)
        size //= 2


def _group_copies(n_ref, rs_ref, tile, n_experts, make_copy, wait):
    slab_row = jnp.int32(0)
    for e in range(n_experts):
        n = n_ref[tile * n_experts + e]
        base = rs_ref[tile * n_experts + e]

        def group_copy(off, size, slab_row=slab_row, base=base):
            return make_copy(pl.multiple_of(slab_row + off, ROW_ALIGN),
                             pl.multiple_of(base + off, ROW_ALIGN), size)

        _copy_rows(n, TOK_TILE, group_copy, wait)
        slab_row = slab_row + n


def _dispatch_kernel(rs_ref, n_ref, ps_ref, pn_ref, h_ref, info_t_ref, xs_ref,
                     slab_ref, zero_ref, sem, zero_sem, *, n_experts):
    i = pl.program_id(0)
    last = pl.num_programs(0) - 1
    slot = i % 2
    rows, tm = slab_ref.shape[1], h_ref.shape[0]
    info_t = info_t_ref[0]
    r = lax.broadcasted_iota(jnp.int32, (rows, tm), 0).astype(F32)
    onehot = ((r == info_t[0:1, :]) | (r == info_t[1:2, :])).astype(BF16)
    slab_ref[slot] = _dot(onehot, h_ref[...]).astype(BF16)

    def copies(tile, slot, wait):
        def make_copy(slab_row, buffer_row, size):
            return pltpu.make_async_copy(slab_ref.at[slot, pl.ds(slab_row, size)],
                                         xs_ref.at[pl.ds(buffer_row, size)], sem.at[slot])
        _group_copies(n_ref, rs_ref, tile, n_experts, make_copy, wait)

    def pad_copies(wait):
        for e in range(n_experts):
            def make_copy(off, size, e=e):
                start = pl.multiple_of(ps_ref[e] + off, ROW_ALIGN)
                return pltpu.make_async_copy(zero_ref.at[pl.ds(0, size)],
                                             xs_ref.at[pl.ds(start, size)], zero_sem)
            _copy_rows(pn_ref[e], zero_ref.shape[0], make_copy, wait)

        def idle_chunk(j, carry):
            start = pl.multiple_of(ps_ref[n_experts] + j * zero_ref.shape[0], ROW_ALIGN)
            cp = pltpu.make_async_copy(zero_ref, xs_ref.at[pl.ds(start, zero_ref.shape[0])],
                                       zero_sem)
            if wait:
                cp.wait()
            else:
                cp.start()
            return carry
        lax.fori_loop(0, pn_ref[n_experts], idle_chunk, 0)

    @pl.when(i > 0)
    def _():
        copies(i - 1, 1 - slot, wait=True)

    copies(i, slot, wait=False)

    @pl.when(i == last)
    def _():
        zero_ref[...] = jnp.zeros_like(zero_ref)
        pad_copies(wait=False)
        copies(i, slot, wait=True)
        pad_copies(wait=True)


def dispatch(h, info_t, row_start, n_rows, pad_start, pad_rows, *, total_rows, n_experts):
    t, d = h.shape
    n_tiles = t // TOK_TILE
    return pl.pallas_call(
        functools.partial(_dispatch_kernel, n_experts=n_experts),
        grid_spec=pltpu.PrefetchScalarGridSpec(
            num_scalar_prefetch=4,
            grid=(n_tiles,),
            in_specs=[
                pl.BlockSpec((TOK_TILE, d), lambda i, *_: (i, 0)),
                pl.BlockSpec((1, 8, TOK_TILE), lambda i, *_: (i, 0, 0)),
            ],
            out_specs=pl.BlockSpec(memory_space=pl.ANY),
            scratch_shapes=[pltpu.VMEM((2, _slab_rows(n_experts), d), BF16),
                            pltpu.VMEM((FFN_ROWS // 2, d), BF16),
                            pltpu.SemaphoreType.DMA((2,)),
                            pltpu.SemaphoreType.DMA],
        ),
        out_shape=jax.ShapeDtypeStruct((total_rows, d), BF16),
        compiler_params=_params("arbitrary"),
    )(row_start, n_rows, pad_start, pad_rows, h, info_t)


def _grouped_ffn_kernel(te_ref, na_ref, x_ref, wg_ref, wu_ref, wd_ref, o_ref, acc_ref):
    j = pl.program_id(0)
    f = pl.program_id(1)
    last_f = pl.num_programs(1) - 1
    active = j < na_ref[0]

    @pl.when(active)
    def _():
        @pl.when(f == 0)
        def _():
            acc_ref[...] = jnp.zeros_like(acc_ref)

        x = x_ref[...]
        act = _silu(_dot(x, wg_ref[...])) * _dot(x, wu_ref[...])
        acc_ref[...] += _dot(act.astype(BF16), wd_ref[...].astype(BF16))

        @pl.when(f == last_f)
        def _():
            o_ref[...] = acc_ref[...].astype(o_ref.dtype)

    @pl.when(jnp.logical_not(active) & (f == last_f))
    def _():
        o_ref[...] = jnp.zeros_like(o_ref)


def grouped_ffn(xs, w_gu, w_down, layer, tile_expert, n_active, *, f_tile):
    rows, d = xs.shape
    ff = w_down.shape[2]
    nf = ff // f_tile

    def f_idx(j, f, na):
        return jnp.where(j < na[0], f, nf - 1)

    return pl.pallas_call(
        _grouped_ffn_kernel,
        grid_spec=pltpu.PrefetchScalarGridSpec(
            num_scalar_prefetch=2,
            grid=(rows // FFN_ROWS, nf),
            in_specs=[
                pl.BlockSpec((FFN_ROWS, d), lambda j, f, te, na: (jnp.minimum(j, na[0] - 1), 0)),
                pl.BlockSpec((None, None, d, f_tile),
                             lambda j, f, te, na: (layer, te[j], 0, f_idx(j, f, na))),
                pl.BlockSpec((None, None, d, f_tile),
                             lambda j, f, te, na: (layer, te[j], 0, nf + f_idx(j, f, na))),
                pl.BlockSpec((None, None, f_tile, d),
                             lambda j, f, te, na: (layer, te[j], f_idx(j, f, na), 0)),
            ],
            out_specs=pl.BlockSpec((FFN_ROWS, d), lambda j, f, te, na: (j, 0)),
            scratch_shapes=[pltpu.VMEM((FFN_ROWS, d), F32)],
        ),
        out_shape=jax.ShapeDtypeStruct((rows, d), BF16),
        compiler_params=_params("arbitrary", "arbitrary"),
    )(tile_expert, n_active, xs, w_gu, w_gu, w_down)


def _combine_kernel(rs_ref, n_ref, x_ref, info_ref, fg_ref, y_ref, o_ref, ybuf_ref, sem,
                    *, n_experts, final_norm):
    i = pl.program_id(0)
    slot = i % 2
    rows, tm = ybuf_ref.shape[1], x_ref.shape[0]

    def copies(tile, slot, wait):
        def make_copy(slab_row, buffer_row, size):
            return pltpu.make_async_copy(y_ref.at[pl.ds(buffer_row, size)],
                                         ybuf_ref.at[slot, pl.ds(slab_row, size)], sem.at[slot])
        _group_copies(n_ref, rs_ref, tile, n_experts, make_copy, wait)

    @pl.when(i == 0)
    def _():
        ybuf_ref[...] = jnp.zeros_like(ybuf_ref)
        copies(i, slot, wait=False)

    @pl.when(i + 1 < pl.num_programs(0))
    def _():
        copies(i + 1, 1 - slot, wait=False)

    copies(i, slot, wait=True)
    y = ybuf_ref[slot]
    info = info_ref[...]
    r = lax.broadcasted_iota(jnp.int32, (tm, rows), 1).astype(F32)
    zero = jnp.zeros((), F32)
    weights = jnp.where(r == info[:, 0:1], info[:, 2:3],
                        jnp.where(r == info[:, 1:2], info[:, 3:4], zero))
    out = x_ref[...] + _dot(weights.astype(BF16), y)
    o_ref[...] = _rms(out, fg_ref[...]) if final_norm else out


def combine(x, info, y, row_start, n_rows, final_gain, *, n_experts, final_norm):
    t, d = x.shape
    n_tiles = t // TOK_TILE
    return pl.pallas_call(
        functools.partial(_combine_kernel, n_experts=n_experts, final_norm=final_norm),
        grid_spec=pltpu.PrefetchScalarGridSpec(
            num_scalar_prefetch=2,
            grid=(n_tiles,),
            in_specs=[
                pl.BlockSpec((TOK_TILE, d), lambda i, rs, n: (i, 0)),
                pl.BlockSpec((TOK_TILE, LANES), lambda i, rs, n: (i, 0)),
                pl.BlockSpec((1, d), lambda i, rs, n: (0, 0)),
                pl.BlockSpec(memory_space=pl.ANY),
            ],
            out_specs=pl.BlockSpec((TOK_TILE, d), lambda i, rs, n: (i, 0)),
            scratch_shapes=[pltpu.VMEM((2, _slab_rows(n_experts), d), BF16),
                            pltpu.SemaphoreType.DMA((2,))],
        ),
        out_shape=jax.ShapeDtypeStruct((t, d), F32),
        compiler_params=_params("arbitrary"),
    )(row_start, n_rows, x, info, final_gain.reshape(1, d), y)


def _ffn_kernel(x_ref, a_ref, wo_ref, g_ref, wg_ref, wu_ref, wd_ref, o_ref):
    x1 = x_ref[...] + _dot(a_ref[...], wo_ref[...])
    h = _rms(x1, g_ref[...]).astype(BF16)
    act = _silu(_dot(h, wg_ref[...])) * _dot(h, wu_ref[...])
    o_ref[...] = x1 + _dot(act.astype(BF16), wd_ref[...])


def ffn(x, a, w_o, gain, w_gu, w_down):
    t, d = x.shape
    ff = w_down.shape[0]
    once = pl.Buffered(1)
    rows = pl.BlockSpec((FFN_DENSE_ROWS, d), lambda i: (i, 0))
    return pl.pallas_call(
        _ffn_kernel,
        grid=(t // FFN_DENSE_ROWS,),
        in_specs=[
            rows,
            rows,
            pl.BlockSpec((d, d), lambda i: (0, 0), pipeline_mode=once),
            pl.BlockSpec((1, d), lambda i: (0, 0)),
            pl.BlockSpec((d, ff), lambda i: (0, 0), pipeline_mode=once),
            pl.BlockSpec((d, ff), lambda i: (0, 1), pipeline_mode=once),
            pl.BlockSpec((ff, d), lambda i: (0, 0), pipeline_mode=once),
        ],
        out_specs=rows,
        out_shape=jax.ShapeDtypeStruct((t, d), F32),
        compiler_params=_params("parallel"),
    )(x, a, w_o, gain.reshape(1, d), w_gu, w_gu, w_down)


def _rmsnorm_kernel(x_ref, g_ref, o_ref):
    o_ref[...] = _rms(x_ref[...], g_ref[...])


def rmsnorm(x, gain):
    t, d = x.shape
    return pl.pallas_call(
        _rmsnorm_kernel,
        grid=(t // ROW_TILE,),
        in_specs=[pl.BlockSpec((ROW_TILE, d), lambda i: (i, 0)),
                  pl.BlockSpec((1, d), lambda i: (0, 0))],
        out_specs=pl.BlockSpec((ROW_TILE, d), lambda i: (i, 0)),
        out_shape=jax.ShapeDtypeStruct((t, d), F32),
        compiler_params=_params("parallel"),
    )(x, gain.reshape(1, d))


def _mxu_tile(n, target):
    best = None
    for cand in range(MXU_WIDTH, target + 1, MXU_WIDTH):
        if n % cand == 0:
            best = cand
    assert best is not None, (n, target)
    return best


def moe_layer(x, a, w_o, gain, w_router, w_gu, w_down, layer, final_gain, *, final_norm):
    t, d = x.shape
    n_experts = w_router.shape[1]
    n_tiles = t // TOK_TILE
    x, h, info, info_t, cnt = router(x, a, w_o, gain, w_router)
    cnt = cnt.reshape(n_tiles, LANES)[:, :n_experts]
    n_rows = (cnt + ROW_ALIGN - 1) // ROW_ALIGN * ROW_ALIGN
    seg_rows = jnp.sum(n_rows, axis=0)
    seg_cap = (seg_rows + FFN_ROWS - 1) // FFN_ROWS * FFN_ROWS
    seg_end = jnp.cumsum(seg_cap)
    row_start = (seg_end - seg_cap)[None, :] + jnp.cumsum(n_rows, axis=0) - n_rows
    max_rows = (MOE_TOPK * t + n_tiles * n_experts * (ROW_ALIGN - 1)
                + n_experts * (FFN_ROWS - ROW_ALIGN))
    total_tiles = -(-max_rows // FFN_ROWS)
    n_active = (seg_end[-1:] // FFN_ROWS).astype(jnp.int32)
    tile_row = jnp.arange(total_tiles, dtype=jnp.int32) * FFN_ROWS
    tile_expert = jnp.minimum(jnp.sum(seg_end[None, :] <= tile_row[:, None], axis=1),
                              n_experts - 1).astype(jnp.int32)
    row_start = row_start.reshape(-1).astype(jnp.int32)
    n_rows = n_rows.reshape(-1).astype(jnp.int32)
    total_rows = total_tiles * FFN_ROWS
    pad_start = jnp.concatenate([seg_end - seg_cap + seg_rows, seg_end[-1:]]).astype(jnp.int32)
    pad_rows = jnp.concatenate([seg_cap - seg_rows,
                                (total_rows - seg_end[-1:]) // (FFN_ROWS // 2)]).astype(jnp.int32)
    xs = dispatch(h, info_t, row_start, n_rows, pad_start, pad_rows,
                  total_rows=total_rows, n_experts=n_experts)
    y = grouped_ffn(xs, w_gu, w_down, layer, tile_expert, n_active,
                    f_tile=_mxu_tile(w_down.shape[2], 2048))
    return combine(x, info, y, row_start, n_rows, final_gain,
                   n_experts=n_experts, final_norm=final_norm)


def kernel(x, attn_norm, ffn_norm, hgrn_w_in, hgrn_lb_logits, hgrn_out_norm, hgrn_w_o, kv_norm, w_kv, moba_w_q, moba_w_o, ffn_w_gu, ffn_w_down, moe_router, moe_w_gu, moe_w_down, final_norm):
    batch, seq, d = x.shape
    depth = attn_norm.shape[0]
    n_a = hgrn_w_in.shape[0]
    x = x.reshape(batch * seq, d)
    k = v_t = k_mean = None
    moe_gu = moe_w_gu.astype(BF16)
    for layer in range(depth):
        if layer < n_a:
            o = hgrn_layer(x, attn_norm[layer], hgrn_w_in[layer].astype(BF16), hgrn_lb_logits,
                           hgrn_out_norm[layer], layer=layer, batch=batch, seq=seq)
            w_o = hgrn_w_o[layer].astype(BF16)
        else:
            if layer == n_a:
                k, v_t, k_mean = shared_kv(x, kv_norm, w_kv[:, :d].astype(BF16),
                                           w_kv[:, d:].T.astype(BF16))
            j = layer - n_a
            q = norm_matmul(x, attn_norm[layer], moba_w_q[j].astype(BF16), F32)
            o = moba_attention(q, k, v_t, k_mean, batch=batch, seq=seq)
            w_o = moba_w_o[j].astype(BF16)
        j = layer // 2
        if layer % 2 == 0:
            x = ffn(x, o, w_o, ffn_norm[layer], ffn_w_gu[j].astype(BF16),
                    ffn_w_down[j].astype(BF16))
        else:
            last = layer == depth - 1
            x = moe_layer(x, o, w_o, ffn_norm[layer], moe_router[j], moe_gu, moe_w_down, j,
                          final_norm, final_norm=last)
    if depth % 2 == 1:
        x = rmsnorm(x, final_norm)
    return x.reshape(batch, seq, d)
```

```python
import functools

import jax
import jax.numpy as jnp
from jax import lax
from jax.experimental import pallas as pl
from jax.experimental.pallas import tpu as pltpu

F32 = jnp.float32
BF16 = jnp.bfloat16

EPS = 1e-6
HEAD_DIM = 128
HGRN_CHUNK = 64
MOBA_BLOCK = 256
MOBA_TOPK = 3
MOE_TOPK = 2
MOBA_HEADS_PER_STEP = 2
LOG2E = 1.4426950408889634
LANES = 128
MXU_WIDTH = 256
FFN_DENSE_ROWS = 512
VMEM_LIMIT = 56 * 1024 * 1024

HGRN_TILE = 512
HGRN_HEADS_PER_STEP = 4
ROW_TILE = 512
TOK_TILE = 512
FFN_ROWS = 512
ROW_ALIGN = 16
SUBLANES = 8
EXPERT_ROWS = SUBLANES


def _params(*semantics):
    return pltpu.CompilerParams(dimension_semantics=semantics, vmem_limit_bytes=VMEM_LIMIT)


def _rms(x, gain):
    return x * lax.rsqrt(jnp.mean(x * x, axis=-1, keepdims=True) + EPS) * gain


def _dot(a, b):
    return jnp.dot(a, b, preferred_element_type=F32)


def _dot_nt(a, b):
    return lax.dot_general(a, b, (((1,), (1,)), ((), ())), preferred_element_type=F32)


def _dot_tn(a, b):
    return lax.dot_general(a, b, (((0,), (0,)), ((), ())), preferred_element_type=F32)


def _silu(x):
    half = 0.5 * x
    return half + half * jnp.tanh(half)


def _norm_matmul_kernel(x_ref, g_ref, w_ref, o_ref):
    h = _rms(x_ref[...], g_ref[...]).astype(BF16)
    o_ref[...] = _dot(h, w_ref[...]).astype(o_ref.dtype)


def norm_matmul(x, gain, w, out_dtype):
    t, d = x.shape
    n = w.shape[1]
    return pl.pallas_call(
        _norm_matmul_kernel,
        grid=(t // ROW_TILE,),
        in_specs=[
            pl.BlockSpec((ROW_TILE, d), lambda i: (i, 0)),
            pl.BlockSpec((1, d), lambda i: (0, 0)),
            pl.BlockSpec((d, n), lambda i: (0, 0)),
        ],
        out_specs=pl.BlockSpec((ROW_TILE, n), lambda i: (i, 0)),
        out_shape=jax.ShapeDtypeStruct((t, n), out_dtype),
        compiler_params=_params("parallel"),
    )(x, gain.reshape(1, d), w)


def _kv_kernel(x_ref, g_ref, wk_ref, wvt_ref, k_ref, vt_ref, km_ref):
    h = _rms(x_ref[...], g_ref[...]).astype(BF16)
    k = _dot(h, wk_ref[...])
    k_ref[...] = k.astype(k_ref.dtype)
    vt_ref[0] = _dot_nt(wvt_ref[...], h).astype(vt_ref.dtype)
    km_ref[0] = jnp.mean(k, axis=0, keepdims=True)


def shared_kv(x, gain, w_k, w_v_t):
    t, d = x.shape
    nblk = t // MOBA_BLOCK
    return pl.pallas_call(
        _kv_kernel,
        grid=(nblk,),
        in_specs=[
            pl.BlockSpec((MOBA_BLOCK, d), lambda i: (i, 0)),
            pl.BlockSpec((1, d), lambda i: (0, 0)),
            pl.BlockSpec((d, d), lambda i: (0, 0)),
            pl.BlockSpec((d, d), lambda i: (0, 0)),
        ],
        out_specs=[
            pl.BlockSpec((MOBA_BLOCK, d), lambda i: (i, 0)),
            pl.BlockSpec((1, d, MOBA_BLOCK), lambda i: (i, 0, 0)),
            pl.BlockSpec((1, 1, d), lambda i: (i, 0, 0)),
        ],
        out_shape=[
            jax.ShapeDtypeStruct((t, d), BF16),
            jax.ShapeDtypeStruct((nblk, d, MOBA_BLOCK), BF16),
            jax.ShapeDtypeStruct((nblk, 1, d), F32),
        ],
        compiler_params=_params("parallel"),
    )(x, gain.reshape(1, d), w_k, w_v_t)


def _hgrn_heads(q_ref, f_ref, i_ref, g_ref, lbl_ref, gain_ref, o_ref, st_ref, *, layer, first):
    c = HGRN_CHUNK
    dk = HEAD_DIM
    heads = q_ref.shape[1] // dk
    per_head = q_ref.shape[0] // c
    n_chunks = heads * per_head
    group = MOBA_BLOCK // c

    def wide(ref):
        return jnp.concatenate([ref[n * c:(n + 1) * c, hh * dk:(hh + 1) * dk]
                                for hh in range(heads) for n in range(per_head)], axis=1)

    def tall(ref):
        return jnp.concatenate([ref[:, hh * dk:(hh + 1) * dk] for hh in range(heads)], axis=0)

    def chunk(x, n):
        return x[:, n * dk:(n + 1) * dk]

    def rows_of(x, n0, n1):
        return jnp.concatenate([chunk(x, n) for n in range(n0, n1)], axis=0)

    lanes = slice(first * dk, (first + heads) * dk)
    fz = wide(f_ref)
    t = jnp.exp(-jnp.abs(fz))
    log_sig = jnp.minimum(fz, 0.0) - jnp.log(1.0 + t)
    sig_neg = jnp.exp(log_sig - fz)
    if layer > 0:
        logits = lbl_ref[:, lanes]
        e = jnp.exp(logits - jnp.max(logits, axis=0, keepdims=True))
        p = e / jnp.sum(e, axis=0, keepdims=True)
        lb = jnp.sum(p[1:layer + 1], axis=0, keepdims=True)
        lb = jnp.concatenate([chunk(lb, hh) for hh in range(heads) for _ in range(per_head)],
                             axis=1)
        log_lb = jnp.log(lb)
        y = jnp.log(1.0 - lb) + log_sig
        log_f = jnp.maximum(log_lb, y) + jnp.log(1.0 + jnp.exp(-jnp.abs(log_lb - y)))
        k = (1.0 - lb) * sig_neg
    else:
        log_f = log_sig
        k = sig_neg

    slab = SUBLANES
    row = lax.broadcasted_iota(jnp.int32, (slab, log_f.shape[1]), 0)
    parts, above = [], None
    for s0 in range(0, c, slab):
        part = log_f[s0:s0 + slab, :]
        shift = 1
        while shift < slab:
            part = part + jnp.where(row >= shift, pltpu.roll(part, shift, axis=0), 0.0)
            shift *= 2
        if above is not None:
            part = part + above
        above = part[slab - 1:slab, :]
        parts.append(part)
    b = jnp.concatenate(parts, axis=0)
    b_mid = b[c // 2 - 1:c // 2, :]
    b_last = b[c - 1:c, :]

    q = _silu(wide(q_ref))
    qd = q * jnp.exp(b - b_mid)
    kd = k * jnp.exp(b_mid - b)
    qe = (qd * jnp.exp(b_mid)).astype(BF16)
    kl = (kd * jnp.exp(b_last - b_mid)).astype(BF16)
    qd = qd.astype(BF16)
    kd = kd.astype(BF16)
    v = wide(i_ref).astype(BF16)
    decay = jnp.exp(b_last)

    gr = group * c
    r_i = lax.broadcasted_iota(jnp.int32, (gr, gr), 0)
    c_i = lax.broadcasted_iota(jnp.int32, (gr, gr), 1)
    keep = (c_i <= r_i) & ((r_i // c) == (c_i // c))
    groups = range(0, n_chunks, group)
    scores = [_dot_nt(rows_of(qd, n0, n0 + group), rows_of(kd, n0, n0 + group)) for n0 in groups]

    grow = []
    zero = jnp.zeros((c, dk), BF16)
    for n in range(0, n_chunks, 2):
        keys = jnp.concatenate([jnp.concatenate([chunk(kl, n), zero], axis=1),
                                jnp.concatenate([zero, chunk(kl, n + 1)], axis=1)], axis=0)
        both = _dot_tn(rows_of(v, n, n + 2), keys)
        grow += [both[:, :dk], both[:, dk:]]
    intra = [_dot(jnp.where(keep, a, 0.0).astype(BF16), rows_of(v, n0, n0 + group))
             for a, n0 in zip(scores, groups)]
    before = []
    for hh in range(heads):
        st = st_ref[first + hh]
        for n in range(hh * per_head, (hh + 1) * per_head):
            before.append(st.astype(BF16))
            st = chunk(decay, n) * st + grow[n]
        st_ref[first + hh] = st
    inter = [_dot_nt(chunk(qe, n), before[n]) for n in range(n_chunks)]
    o = jnp.concatenate(intra, axis=0) + jnp.concatenate(inter, axis=0)

    o = o * lax.rsqrt(jnp.mean(o * o, axis=-1, keepdims=True) + EPS)
    o = o * _silu(tall(g_ref))
    rows = q_ref.shape[0]
    for hh in range(heads):
        cols = slice((first + hh) * dk, (first + hh + 1) * dk)
        o_ref[:, cols] = (o[hh * rows:(hh + 1) * rows] * gain_ref[:, cols]).astype(o_ref.dtype)


def _hgrn_layer_kernel(x_ref, ng_ref, w_ref, lbl_ref, gain_ref, o_ref, st_ref, *, layer):
    @pl.when(pl.program_id(1) == 0)
    def _():
        st_ref[...] = jnp.zeros_like(st_ref)

    d = x_ref.shape[1]
    hp = HGRN_HEADS_PER_STEP
    width = hp * HEAD_DIM
    h = _rms(x_ref[...], ng_ref[...]).astype(BF16)

    def project(group):
        return [_dot(h, w_ref[:, part * d + group * width:part * d + (group + 1) * width])
                for part in range(4)]

    n_groups = d // width
    nxt = project(0)
    for group in range(n_groups):
        cur = nxt
        if group + 1 < n_groups:
            nxt = project(group + 1)
        _hgrn_heads(*cur, lbl_ref, gain_ref, o_ref, st_ref, layer=layer, first=group * hp)


def hgrn_layer(x, norm_gain, w_in, lb_logits, out_gain, *, layer, batch, seq):
    t, d = x.shape
    heads = d // HEAD_DIM
    tiles = seq // HGRN_TILE
    n_layers = lb_logits.shape[0]
    return pl.pallas_call(
        functools.partial(_hgrn_layer_kernel, layer=layer),
        grid=(batch, tiles),
        in_specs=[
            pl.BlockSpec((HGRN_TILE, d), lambda b, s: (b * tiles + s, 0)),
            pl.BlockSpec((1, d), lambda b, s: (0, 0)),
            pl.BlockSpec((d, 4 * d), lambda b, s: (0, 0), pipeline_mode=pl.Buffered(1)),
            pl.BlockSpec((n_layers, d), lambda b, s: (0, 0)),
            pl.BlockSpec((1, d), lambda b, s: (0, 0)),
        ],
        out_specs=pl.BlockSpec((HGRN_TILE, d), lambda b, s: (b * tiles + s, 0)),
        out_shape=jax.ShapeDtypeStruct((t, d), BF16),
        scratch_shapes=[pltpu.VMEM((heads, HEAD_DIM, HEAD_DIM), F32)],
        compiler_params=_params("parallel", "arbitrary"),
    )(x, norm_gain.reshape(1, d), w_in, lb_logits, out_gain.reshape(1, d))


def _moba_kernel(q_ref, k_ref, vt_ref, km_ref, slope_ref, o_ref):
    bs = MOBA_BLOCK
    dh = HEAD_DIM
    n_blk = km_ref.shape[1]
    heads = q_ref.shape[1] // dh
    scale2 = HEAD_DIM ** -0.5 * LOG2E
    rel = (lax.broadcasted_iota(jnp.int32, (bs, bs), 1)
           - lax.broadcasted_iota(jnp.int32, (bs, bs), 0))
    rel_f = rel.astype(F32)

    def gate_rank(q, km, n):
        gate = lax.dot_general(km, q, (((1,), (1,)), ((), ())),
                               precision=lax.Precision.HIGHEST, preferred_element_type=F32)
        blk = lax.broadcasted_iota(jnp.int32, gate.shape, 0)
        rank = jnp.zeros(gate.shape, F32)
        for m in range(n):
            g_m = gate[m:m + 1, :]
            rank = rank + ((g_m > gate) | ((g_m == gate) & (m < blk))).astype(F32)
        return rank

    def scores(n, hh):
        cols = slice(hh * dh, (hh + 1) * dh)
        q = q_ref[n * bs:(n + 1) * bs, cols]
        q16 = (q * scale2).astype(BF16)
        rank = gate_rank(q, km_ref[0][:, cols], n) if n > MOBA_TOPK else None
        return _dot_nt(k_ref[0:(n + 1) * bs, cols], q16), rank

    def probabilities(n, hh, s_all, rank):
        slope2 = slope_ref[hh][0:1, 0:1] * LOG2E
        bias = -slope2 * rel_f
        us, shifts, maxes = [], [], []
        for kb in range(n + 1):
            u = s_all[kb * bs:(kb + 1) * bs] + bias
            if kb == n:
                u = jnp.where(rel >= 0, u, -jnp.inf)
            elif rank is not None:
                u = jnp.where(rank[kb:kb + 1, :] < MOBA_TOPK, u, -jnp.inf)
            shift = slope2 * float((n - kb) * bs)
            us.append(u)
            shifts.append(shift)
            maxes.append(jnp.max(u, axis=0, keepdims=True) - shift)
        m = functools.reduce(jnp.maximum, maxes)
        ps = [jnp.exp2(u - (m + shift)) for u, shift in zip(us, shifts)]
        return jnp.concatenate(ps, axis=0).astype(BF16)

    def output(n, hh, p_all):
        cols = slice(hh * dh, (hh + 1) * dh)
        vt = jnp.concatenate([vt_ref[kb, cols, :] for kb in range(n + 1)], axis=1)
        ones = jnp.ones((ROW_ALIGN, vt.shape[1]), BF16)
        pv = _dot(jnp.concatenate([vt, ones], axis=0), p_all)
        o_ref[n * bs:(n + 1) * bs, cols] = (pv[0:dh] / pv[dh:dh + 1]).T.astype(o_ref.dtype)

    scored = [scores(0, hh) for hh in range(heads)]
    for n in range(n_blk):
        probs = [probabilities(n, hh, *scored[hh]) for hh in range(heads)]
        if n + 1 < n_blk:
            scored = [scores(n + 1, hh) for hh in range(heads)]
        for hh in range(heads):
            output(n, hh, probs[hh])


def moba_attention(q, k, v_t, k_mean, *, batch, seq):
    t, d = q.shape
    heads = d // HEAD_DIM
    n_blk = seq // MOBA_BLOCK
    hp = MOBA_HEADS_PER_STEP
    width = hp * HEAD_DIM
    slopes = (2.0 ** (-8.0 * jnp.arange(1, heads + 1, dtype=F32) / heads))
    slopes = jnp.broadcast_to(slopes[:, None, None], (heads, 1, LANES))
    return pl.pallas_call(
        _moba_kernel,
        grid=(batch, heads // hp),
        in_specs=[
            pl.BlockSpec((seq, width), lambda b, h: (b, h)),
            pl.BlockSpec((seq, width), lambda b, h: (b, h)),
            pl.BlockSpec((n_blk, width, MOBA_BLOCK), lambda b, h: (b, h, 0)),
            pl.BlockSpec((1, n_blk, width), lambda b, h: (b, 0, h)),
            pl.BlockSpec((hp, 1, LANES), lambda b, h: (h, 0, 0)),
        ],
        out_specs=pl.BlockSpec((seq, width), lambda b, h: (b, h)),
        out_shape=jax.ShapeDtypeStruct((t, d), BF16),
        compiler_params=_params("parallel", "parallel"),
    )(q, k, v_t, k_mean.reshape(batch, n_blk, d), slopes)


def _router_kernel(x_ref, a_ref, wo_ref, g_ref, wt_ref, x1_ref, h_ref, info_ref, info_t_ref,
                   cnt_ref, *, n_experts):
    x1 = x_ref[...] + _dot(a_ref[...], wo_ref[...])
    x1_ref[...] = x1
    h = _rms(x1, g_ref[...])
    h_hi = h.astype(BF16)
    h_ref[...] = h_hi
    h_lo = (h - h_hi.astype(F32)).astype(BF16)
    wt = wt_ref[...]
    wt_hi = wt.astype(BF16)
    wt_lo = (wt - wt_hi.astype(F32)).astype(BF16)
    by_hi = _dot_nt(jnp.concatenate([wt_hi, wt_lo], axis=0), h_hi)
    by_lo = _dot_nt(wt_hi, h_lo)
    rows = EXPERT_ROWS
    logits = by_hi[0:rows] + by_hi[ROW_ALIGN:ROW_ALIGN + rows] + by_lo[0:rows]
    tm = logits.shape[1]
    sub = lax.broadcasted_iota(jnp.int32, (rows, tm), 0)
    neg = -jnp.inf
    lg = jnp.where(sub < n_experts, logits, neg)
    m1 = jnp.max(lg, axis=0, keepdims=True)
    i1 = jnp.min(jnp.where(lg == m1, sub, rows), axis=0, keepdims=True)
    lg2 = jnp.where(sub == i1, neg, lg)
    m2 = jnp.max(lg2, axis=0, keepdims=True)
    i2 = jnp.min(jnp.where(lg2 == m2, sub, rows), axis=0, keepdims=True)
    e2 = jnp.exp(m2 - m1)
    w1 = 1.0 / (1.0 + e2)
    w2 = e2 / (1.0 + e2)

    routed = ((sub == i1) | (sub == i2)).astype(F32)
    routed_pad = jnp.concatenate([routed, jnp.zeros((LANES - rows, tm), F32)], axis=0).astype(BF16)
    cnt_ref[0] = _dot_nt(jnp.ones((ROW_ALIGN, tm), BF16), routed_pad)[0:1, :].astype(jnp.int32)
    later = (lax.broadcasted_iota(jnp.int32, (tm, tm), 0)
             < lax.broadcasted_iota(jnp.int32, (tm, tm), 1)).astype(BF16)
    rank = _dot(routed_pad[0:ROW_ALIGN], later)[0:rows]
    cnt = jnp.sum(routed, axis=1, keepdims=True)
    group_rows = jnp.floor((cnt + (ROW_ALIGN - 1)) * (1.0 / ROW_ALIGN)) * ROW_ALIGN
    expert = lax.broadcasted_iota(jnp.int32, (rows, 1), 0)
    group_start = jnp.zeros((rows, 1), F32)
    for e in range(n_experts - 1):
        group_start = group_start + jnp.where(expert > e, group_rows[e:e + 1, :], 0.0)
    slab_row = group_start + rank
    first = i1 < i2
    i_lo = jnp.where(first, i1, i2)
    i_hi = jnp.where(first, i2, i1)
    row_lo = jnp.sum(jnp.where(sub == i_lo, slab_row, 0.0), axis=0, keepdims=True)
    row_hi = jnp.sum(jnp.where(sub == i_hi, slab_row, 0.0), axis=0, keepdims=True)
    info_t = (jnp.where(sub == 0, row_lo, 0.0) + jnp.where(sub == 1, row_hi, 0.0)
              + jnp.where(sub == 2, jnp.where(first, w1, w2), 0.0)
              + jnp.where(sub == 3, jnp.where(first, w2, w1), 0.0))
    info_t_ref[0] = info_t
    info_ref[...] = jnp.concatenate([info_t, jnp.zeros((LANES - rows, tm), F32)], axis=0).T


def router(x, a, w_o, gain, w_router):
    t, d = x.shape
    n_experts = w_router.shape[1]
    n_tiles = t // TOK_TILE
    assert n_experts <= EXPERT_ROWS, n_experts
    w_t = jnp.pad(w_router.T, ((0, ROW_ALIGN - n_experts), (0, 0)))
    return pl.pallas_call(
        functools.partial(_router_kernel, n_experts=n_experts),
        grid=(n_tiles,),
        in_specs=[
            pl.BlockSpec((TOK_TILE, d), lambda i: (i, 0)),
            pl.BlockSpec((TOK_TILE, d), lambda i: (i, 0)),
            pl.BlockSpec((d, d), lambda i: (0, 0)),
            pl.BlockSpec((1, d), lambda i: (0, 0)),
            pl.BlockSpec((ROW_ALIGN, d), lambda i: (0, 0)),
        ],
        out_specs=[
            pl.BlockSpec((TOK_TILE, d), lambda i: (i, 0)),
            pl.BlockSpec((TOK_TILE, d), lambda i: (i, 0)),
            pl.BlockSpec((TOK_TILE, LANES), lambda i: (i, 0)),
            pl.BlockSpec((1, EXPERT_ROWS, TOK_TILE), lambda i: (i, 0, 0)),
            pl.BlockSpec((1, 1, LANES), lambda i: (i, 0, 0)),
        ],
        out_shape=[
            jax.ShapeDtypeStruct((t, d), F32),
            jax.ShapeDtypeStruct((t, d), BF16),
            jax.ShapeDtypeStruct((t, LANES), F32),
            jax.ShapeDtypeStruct((n_tiles, EXPERT_ROWS, TOK_TILE), F32),
            jax.ShapeDtypeStruct((n_tiles, 1, LANES), jnp.int32),
        ],
        compiler_params=_params("parallel"),
    )(x, a, w_o, gain.reshape(1, d), w_t)


def _slab_rows(n_experts):
    bound = MOE_TOPK * TOK_TILE + n_experts * (ROW_ALIGN - 1)
    return -(-bound // LANES) * LANES


def _copy_rows(n, max_size, make_copy, wait):
    off = jnp.int32(0)
    size = max_size
    while size >= ROW_ALIGN:
        @pl.when((n & size) != 0)
        def _(off=off, size=size):
            cp = make_copy(off, size)
            if wait:
                cp.wait()
            else:
                cp.start()
        off = off + (n & size)
        size //= 2


def _group_copies(n_ref, rs_ref, tile, n_experts, make_copy, wait):
    slab_row = jnp.int32(0)
    for e in range(n_experts):
        n = n_ref[tile * n_experts + e]
        base = rs_ref[tile * n_experts + e]

        def group_copy(off, size, slab_row=slab_row, base=base):
            return make_copy(pl.multiple_of(slab_row + off, ROW_ALIGN),
                             pl.multiple_of(base + off, ROW_ALIGN), size)

        _copy_rows(n, TOK_TILE, group_copy, wait)
        slab_row = slab_row + n


def _dispatch_kernel(rs_ref, n_ref, ps_ref, pn_ref, h_ref, info_t_ref, xs_ref,
                     slab_ref, zero_ref, sem, zero_sem, *, n_experts):
    i = pl.program_id(0)
    last = pl.num_programs(0) - 1
    slot = i % 2
    rows, tm = slab_ref.shape[1], h_ref.shape[0]
    info_t = info_t_ref[0]
    r = lax.broadcasted_iota(jnp.int32, (rows, tm), 0).astype(F32)
    onehot = ((r == info_t[0:1, :]) | (r == info_t[1:2, :])).astype(BF16)
    slab_ref[slot] = _dot(onehot, h_ref[...]).astype(BF16)

    def copies(tile, slot, wait):
        def make_copy(slab_row, buffer_row, size):
            return pltpu.make_async_copy(slab_ref.at[slot, pl.ds(slab_row, size)],
                                         xs_ref.at[pl.ds(buffer_row, size)], sem.at[slot])
        _group_copies(n_ref, rs_ref, tile, n_experts, make_copy, wait)

    def pad_copies(wait):
        for e in range(n_experts):
            def make_copy(off, size, e=e):
                start = pl.multiple_of(ps_ref[e] + off, ROW_ALIGN)
                return pltpu.make_async_copy(zero_ref.at[pl.ds(0, size)],
                                             xs_ref.at[pl.ds(start, size)], zero_sem)
            _copy_rows(pn_ref[e], zero_ref.shape[0], make_copy, wait)

        def idle_chunk(j, carry):
            start = pl.multiple_of(ps_ref[n_experts] + j * zero_ref.shape[0], ROW_ALIGN)
            cp = pltpu.make_async_copy(zero_ref, xs_ref.at[pl.ds(start, zero_ref.shape[0])],
                                       zero_sem)
            if wait:
                cp.wait()
            else:
                cp.start()
            return carry
        lax.fori_loop(0, pn_ref[n_experts], idle_chunk, 0)

    @pl.when(i > 0)
    def _():
        copies(i - 1, 1 - slot, wait=True)

    copies(i, slot, wait=False)

    @pl.when(i == last)
    def _():
        zero_ref[...] = jnp.zeros_like(zero_ref)
        pad_copies(wait=False)
        copies(i, slot, wait=True)
        pad_copies(wait=True)


def dispatch(h, info_t, row_start, n_rows, pad_start, pad_rows, *, total_rows, n_experts):
    t, d = h.shape
    n_tiles = t // TOK_TILE
    return pl.pallas_call(
        functools.partial(_dispatch_kernel, n_experts=n_experts),
        grid_spec=pltpu.PrefetchScalarGridSpec(
            num_scalar_prefetch=4,
            grid=(n_tiles,),
            in_specs=[
                pl.BlockSpec((TOK_TILE, d), lambda i, *_: (i, 0)),
                pl.BlockSpec((1, 8, TOK_TILE), lambda i, *_: (i, 0, 0)),
            ],
            out_specs=pl.BlockSpec(memory_space=pl.ANY),
            scratch_shapes=[pltpu.VMEM((2, _slab_rows(n_experts), d), BF16),
                            pltpu.VMEM((FFN_ROWS // 2, d), BF16),
                            pltpu.SemaphoreType.DMA((2,)),
                            pltpu.SemaphoreType.DMA],
        ),
        out_shape=jax.ShapeDtypeStruct((total_rows, d), BF16),
        compiler_params=_params("arbitrary"),
    )(row_start, n_rows, pad_start, pad_rows, h, info_t)


def _grouped_ffn_kernel(te_ref, na_ref, x_ref, wg_ref, wu_ref, wd_ref, o_ref, acc_ref):
    j = pl.program_id(0)
    f = pl.program_id(1)
    last_f = pl.num_programs(1) - 1
    active = j < na_ref[0]

    @pl.when(active)
    def _():
        @pl.when(f == 0)
        def _():
            acc_ref[...] = jnp.zeros_like(acc_ref)

        x = x_ref[...]
        act = _silu(_dot(x, wg_ref[...])) * _dot(x, wu_ref[...])
        acc_ref[...] += _dot(act.astype(BF16), wd_ref[...].astype(BF16))

        @pl.when(f == last_f)
        def _():
            o_ref[...] = acc_ref[...].astype(o_ref.dtype)

    @pl.when(jnp.logical_not(active) & (f == last_f))
    def _():
        o_ref[...] = jnp.zeros_like(o_ref)


def grouped_ffn(xs, w_gu, w_down, layer, tile_expert, n_active, *, f_tile):
    rows, d = xs.shape
    ff = w_down.shape[2]
    nf = ff // f_tile

    def f_idx(j, f, na):
        return jnp.where(j < na[0], f, nf - 1)

    return pl.pallas_call(
        _grouped_ffn_kernel,
        grid_spec=pltpu.PrefetchScalarGridSpec(
            num_scalar_prefetch=2,
            grid=(rows // FFN_ROWS, nf),
            in_specs=[
                pl.BlockSpec((FFN_ROWS, d), lambda j, f, te, na: (jnp.minimum(j, na[0] - 1), 0)),
                pl.BlockSpec((None, None, d, f_tile),
                             lambda j, f, te, na: (layer, te[j], 0, f_idx(j, f, na))),
                pl.BlockSpec((None, None, d, f_tile),
                             lambda j, f, te, na: (layer, te[j], 0, nf + f_idx(j, f, na))),
                pl.BlockSpec((None, None, f_tile, d),
                             lambda j, f, te, na: (layer, te[j], f_idx(j, f, na), 0)),
            ],
            out_specs=pl.BlockSpec((FFN_ROWS, d), lambda j, f, te, na: (j, 0)),
            scratch_shapes=[pltpu.VMEM((FFN_ROWS, d), F32)],
        ),
        out_shape=jax.ShapeDtypeStruct((rows, d), BF16),
        compiler_params=_params("arbitrary", "arbitrary"),
    )(tile_expert, n_active, xs, w_gu, w_gu, w_down)


def _combine_kernel(rs_ref, n_ref, x_ref, info_ref, fg_ref, y_ref, o_ref, ybuf_ref, sem,
                    *, n_experts, final_norm):
    i = pl.program_id(0)
    slot = i % 2
    rows, tm = ybuf_ref.shape[1], x_ref.shape[0]

    def copies(tile, slot, wait):
        def make_copy(slab_row, buffer_row, size):
            return pltpu.make_async_copy(y_ref.at[pl.ds(buffer_row, size)],
                                         ybuf_ref.at[slot, pl.ds(slab_row, size)], sem.at[slot])
        _group_copies(n_ref, rs_ref, tile, n_experts, make_copy, wait)

    @pl.when(i == 0)
    def _():
        ybuf_ref[...] = jnp.zeros_like(ybuf_ref)
        copies(i, slot, wait=False)

    @pl.when(i + 1 < pl.num_programs(0))
    def _():
        copies(i + 1, 1 - slot, wait=False)

    copies(i, slot, wait=True)
    y = ybuf_ref[slot]
    info = info_ref[...]
    r = lax.broadcasted_iota(jnp.int32, (tm, rows), 1).astype(F32)
    zero = jnp.zeros((), F32)
    weights = jnp.where(r == info[:, 0:1], info[:, 2:3],
                        jnp.where(r == info[:, 1:2], info[:, 3:4], zero))
    out = x_ref[...] + _dot(weights.astype(BF16), y)
    o_ref[...] = _rms(out, fg_ref[...]) if final_norm else out


def combine(x, info, y, row_start, n_rows, final_gain, *, n_experts, final_norm):
    t, d = x.shape
    n_tiles = t // TOK_TILE
    return pl.pallas_call(
        functools.partial(_combine_kernel, n_experts=n_experts, final_norm=final_norm),
        grid_spec=pltpu.PrefetchScalarGridSpec(
            num_scalar_prefetch=2,
            grid=(n_tiles,),
            in_specs=[
                pl.BlockSpec((TOK_TILE, d), lambda i, rs, n: (i, 0)),
                pl.BlockSpec((TOK_TILE, LANES), lambda i, rs, n: (i, 0)),
                pl.BlockSpec((1, d), lambda i, rs, n: (0, 0)),
                pl.BlockSpec(memory_space=pl.ANY),
            ],
            out_specs=pl.BlockSpec((TOK_TILE, d), lambda i, rs, n: (i, 0)),
            scratch_shapes=[pltpu.VMEM((2, _slab_rows(n_experts), d), BF16),
                            pltpu.SemaphoreType.DMA((2,))],
        ),
        out_shape=jax.ShapeDtypeStruct((t, d), F32),
        compiler_params=_params("arbitrary"),
    )(row_start, n_rows, x, info, final_gain.reshape(1, d), y)


def _ffn_kernel(x_ref, a_ref, wo_ref, g_ref, wg_ref, wu_ref, wd_ref, o_ref):
    x1 = x_ref[...] + _dot(a_ref[...], wo_ref[...])
    h = _rms(x1, g_ref[...]).astype(BF16)
    act = _silu(_dot(h, wg_ref[...])) * _dot(h, wu_ref[...])
    o_ref[...] = x1 + _dot(act.astype(BF16), wd_ref[...])


def ffn(x, a, w_o, gain, w_gu, w_down):
    t, d = x.shape
    ff = w_down.shape[0]
    once = pl.Buffered(1)
    rows = pl.BlockSpec((FFN_DENSE_ROWS, d), lambda i: (i, 0))
    return pl.pallas_call(
        _ffn_kernel,
        grid=(t // FFN_DENSE_ROWS,),
        in_specs=[
            rows,
            rows,
            pl.BlockSpec((d, d), lambda i: (0, 0), pipeline_mode=once),
            pl.BlockSpec((1, d), lambda i: (0, 0)),
            pl.BlockSpec((d, ff), lambda i: (0, 0), pipeline_mode=once),
            pl.BlockSpec((d, ff), lambda i: (0, 1), pipeline_mode=once),
            pl.BlockSpec((ff, d), lambda i: (0, 0), pipeline_mode=once),
        ],
        out_specs=rows,
        out_shape=jax.ShapeDtypeStruct((t, d), F32),
        compiler_params=_params("parallel"),
    )(x, a, w_o, gain.reshape(1, d), w_gu, w_gu, w_down)


def _rmsnorm_kernel(x_ref, g_ref, o_ref):
    o_ref[...] = _rms(x_ref[...], g_ref[...])


def rmsnorm(x, gain):
    t, d = x.shape
    return pl.pallas_call(
        _rmsnorm_kernel,
        grid=(t // ROW_TILE,),
        in_specs=[pl.BlockSpec((ROW_TILE, d), lambda i: (i, 0)),
                  pl.BlockSpec((1, d), lambda i: (0, 0))],
        out_specs=pl.BlockSpec((ROW_TILE, d), lambda i: (i, 0)),
        out_shape=jax.ShapeDtypeStruct((t, d), F32),
        compiler_params=_params("parallel"),
    )(x, gain.reshape(1, d))


def _mxu_tile(n, target):
    best = None
    for cand in range(MXU_WIDTH, target + 1, MXU_WIDTH):
        if n % cand == 0:
            best = cand
    assert best is not None, (n, target)
    return best


def moe_layer(x, a, w_o, gain, w_router, w_gu, w_down, layer, final_gain, *, final_norm):
    t, d = x.shape
    n_experts = w_router.shape[1]
    n_tiles = t // TOK_TILE
    x, h, info, info_t, cnt = router(x, a, w_o, gain, w_router)
    cnt = cnt.reshape(n_tiles, LANES)[:, :n_experts]
    n_rows = (cnt + ROW_ALIGN - 1) // ROW_ALIGN * ROW_ALIGN
    seg_rows = jnp.sum(n_rows, axis=0)
    seg_cap = (seg_rows + FFN_ROWS - 1) // FFN_ROWS * FFN_ROWS
    seg_end = jnp.cumsum(seg_cap)
    row_start = (seg_end - seg_cap)[None, :] + jnp.cumsum(n_rows, axis=0) - n_rows
    max_rows = (MOE_TOPK * t + n_tiles * n_experts * (ROW_ALIGN - 1)
                + n_experts * (FFN_ROWS - ROW_ALIGN))
    total_tiles = -(-max_rows // FFN_ROWS)
    n_active = (seg_end[-1:] // FFN_ROWS).astype(jnp.int32)
    tile_row = jnp.arange(total_tiles, dtype=jnp.int32) * FFN_ROWS
    tile_expert = jnp.minimum(jnp.sum(seg_end[None, :] <= tile_row[:, None], axis=1),
                              n_experts - 1).astype(jnp.int32)
    row_start = row_start.reshape(-1).astype(jnp.int32)
    n_rows = n_rows.reshape(-1).astype(jnp.int32)
    total_rows = total_tiles * FFN_ROWS
    pad_start = jnp.concatenate([seg_end - seg_cap + seg_rows, seg_end[-1:]]).astype(jnp.int32)
    pad_rows = jnp.concatenate([seg_cap - seg_rows,
                                (total_rows - seg_end[-1:]) // (FFN_ROWS // 2)]).astype(jnp.int32)
    xs = dispatch(h, info_t, row_start, n_rows, pad_start, pad_rows,
                  total_rows=total_rows, n_experts=n_experts)
    y = grouped_ffn(xs, w_gu, w_down, layer, tile_expert, n_active,
                    f_tile=_mxu_tile(w_down.shape[2], 2048))
    return combine(x, info, y, row_start, n_rows, final_gain,
                   n_experts=n_experts, final_norm=final_norm)


def kernel(x, attn_norm, ffn_norm, hgrn_w_in, hgrn_lb_logits, hgrn_out_norm, hgrn_w_o, kv_norm, w_kv, moba_w_q, moba_w_o, ffn_w_gu, ffn_w_down, moe_router, moe_w_gu, moe_w_down, final_norm):
    batch, seq, d = x.shape
    depth = attn_norm.shape[0]
    n_a = hgrn_w_in.shape[0]
    x = x.reshape(batch * seq, d)
    k = v_t = k_mean = None
    moe_gu = moe_w_gu.astype(BF16)
    for layer in range(depth):
        if layer < n_a:
            o = hgrn_layer(x, attn_norm[layer], hgrn_w_in[layer].astype(BF16), hgrn_lb_logits,
                           hgrn_out_norm[layer], layer=layer, batch=batch, seq=seq)
            w_o = hgrn_w_o[layer].astype(BF16)
        else:
            if layer == n_a:
                k, v_t, k_mean = shared_kv(x, kv_norm, w_kv[:, :d].astype(BF16),
                                           w_kv[:, d:].T.astype(BF16))
            j = layer - n_a
            q = norm_matmul(x, attn_norm[layer], moba_w_q[j].astype(BF16), F32)
            o = moba_attention(q, k, v_t, k_mean, batch=batch, seq=seq)
            w_o = moba_w_o[j].astype(BF16)
        j = layer // 2
        if layer % 2 == 0:
            x = ffn(x, o, w_o, ffn_norm[layer], ffn_w_gu[j].astype(BF16),
                    ffn_w_down[j].astype(BF16))
        else:
            last = layer == depth - 1
            x = moe_layer(x, o, w_o, ffn_norm[layer], moe_router[j], moe_gu, moe_w_down, j,
                          final_norm, final_norm=last)
    if depth % 2 == 1:
        x = rmsnorm(x, final_norm)
    return x.reshape(batch, seq, d)
```
